```python
import jax
import jax.numpy as jnp
from jax import lax
import numpy as np

D_MODEL = 1024
BATCH = 4
SEQ = 8192
DEPTH = 4

GRID_W = 64
CTX_LEN = 256
HEAD_DIM = 64
N_HEADS_A = 8
N_KV_A = 2
N_HEADS_B = 8
N_KV_B = 2
G_A = N_HEADS_A // N_KV_A
G_B = N_HEADS_B // N_KV_B
QA = N_HEADS_A * HEAD_DIM
KVA = N_KV_A * HEAD_DIM
QB = N_HEADS_B * HEAD_DIM
KVB = N_KV_B * HEAD_DIM
IN_PROJ_WIDTH = QA + 2 * KVA + QB + 2 * KVB
IN_SPLITS = (QA, QA + KVA, QA + 2 * KVA, QA + 2 * KVA + QB, QA + 2 * KVA + QB + KVB)
MIX_OUT_WIDTH = QA + QB
WINDOW = 128
Q_BLOCK = 128
ROPE_THETA = 10000.0
ATTN_SCALE = HEAD_DIM ** -0.5
CONV_WIDTH = 3
D_FF = 2816
N_EXPERTS = 8
TOP_K = 2
D_FF_EXPERT = 3584
N_ADA = 6
EPS = 1e-6
N_ATTN_LAYERS = (DEPTH + 1) // 2
N_CONV_LAYERS = DEPTH // 2

kernel_name = 'hybrid_dit_gqa_window_shortconv_moe'


def rms_norm(x, g):
    xf = x.astype(jnp.float32)
    y = xf * lax.rsqrt(jnp.mean(xf * xf, axis=-1, keepdims=True) + EPS)
    return (y * g.astype(jnp.float32)).astype(x.dtype)


def axial_rope_tables(n_tokens):
    rows = n_tokens // GRID_W
    row, col = jnp.meshgrid(jnp.arange(rows, dtype=jnp.float32), jnp.arange(GRID_W, dtype=jnp.float32), indexing='ij')
    half = HEAD_DIM // 2
    inv_freq = ROPE_THETA ** (-jnp.arange(0, half, 2, dtype=jnp.float32) / half)
    ang = jnp.concatenate([row.reshape(-1, 1) * inv_freq, col.reshape(-1, 1) * inv_freq], axis=-1)
    return jnp.cos(ang), jnp.sin(ang)


def apply_rope(x, cos, sin):
    shape = (cos.shape[0],) + (1,) * (x.ndim - 3) + (cos.shape[1],)
    cs, sn = cos.reshape(shape), sin.reshape(shape)
    xr = x.astype(jnp.float32).reshape(x.shape[:-1] + (HEAD_DIM // 2, 2))
    x0, x1 = xr[..., 0], xr[..., 1]
    out = jnp.stack([x0 * cs - x1 * sn, x0 * sn + x1 * cs], axis=-1)
    return out.reshape(x.shape).astype(x.dtype)


def softmax_with_sink(s, sink):
    sk = sink.astype(jnp.float32)[:, :, None, None]
    m = jnp.maximum(jnp.max(s, axis=-1, keepdims=True), sk)
    p = jnp.exp(s - m)
    return p / (jnp.sum(p, axis=-1, keepdims=True) + jnp.exp(sk - m))


def project_heads(h, w_in, qn_a, kn_a, qn_b, kn_b):
    bsz, n, _ = h.shape
    qa, ka, va, qb, kb, vb = jnp.split(h @ w_in, IN_SPLITS, axis=-1)
    qa = rms_norm(qa.reshape(bsz, n, N_KV_A, G_A, HEAD_DIM), qn_a)
    ka = rms_norm(ka.reshape(bsz, n, N_KV_A, HEAD_DIM), kn_a)
    va = va.reshape(bsz, n, N_KV_A, HEAD_DIM)
    qb = rms_norm(qb.reshape(bsz, n, N_KV_B, G_B, HEAD_DIM), qn_b)
    kb = rms_norm(kb.reshape(bsz, n, N_KV_B, HEAD_DIM), kn_b)
    vb = vb.reshape(bsz, n, N_KV_B, HEAD_DIM)
    return qa, ka, va, qb, kb, vb


def context_attention(q, k, v, sink):
    s = jnp.einsum('bqhgd,bkhd->bhgqk', q, k).astype(jnp.float32) * ATTN_SCALE
    p = jax.nn.softmax(s, axis=-1) if sink is None else softmax_with_sink(s, sink)
    return jnp.einsum('bhgqk,bkhd->bqhgd', p.astype(v.dtype), v)


def global_attention(q, k_lat, v_lat, k_ctx, v_ctx):
    bsz, n = q.shape[:2]
    nb = n // Q_BLOCK
    k = jnp.concatenate([k_ctx, k_lat], axis=1)
    v = jnp.concatenate([v_ctx, v_lat], axis=1)
    qb = jnp.moveaxis(q.reshape((bsz, nb, Q_BLOCK) + q.shape[2:]), 1, 0)

    def block(q_blk):
        s = jnp.einsum('bqhgd,bkhd->bhgqk', q_blk, k).astype(jnp.float32) * ATTN_SCALE
        p = jax.nn.softmax(s, axis=-1)
        return jnp.einsum('bhgqk,bkhd->bqhgd', p.astype(v.dtype), v)

    out = lax.map(block, qb)
    return jnp.moveaxis(out, 0, 1).reshape(q.shape)


def window_attention(q, k_lat, v_lat, k_ctx, v_ctx, sink):
    bsz, n = q.shape[:2]
    nb = n // Q_BLOCK
    span = Q_BLOCK + 2 * WINDOW
    pad = ((0, 0), (WINDOW, WINDOW), (0, 0), (0, 0))
    kp = jnp.pad(k_lat, pad)
    vp = jnp.pad(v_lat, pad)
    qb = jnp.moveaxis(q.reshape((bsz, nb, Q_BLOCK) + q.shape[2:]), 1, 0)
    rel = jnp.arange(Q_BLOCK)[:, None] - jnp.arange(span)[None, :] + WINDOW
    band = jnp.abs(rel) <= WINDOW
    s_ctx_all = None

    def block(args):
        bi, q_blk = args
        start = bi * Q_BLOCK
        kb = lax.dynamic_slice_in_dim(kp, start, span, axis=1)
        vb = lax.dynamic_slice_in_dim(vp, start, span, axis=1)
        kpos = start - WINDOW + jnp.arange(span)
        valid = band & ((kpos >= 0) & (kpos < n))[None, :]
        s_loc = jnp.einsum('bqhgd,bkhd->bhgqk', q_blk, kb).astype(jnp.float32) * ATTN_SCALE
        s_loc = jnp.where(valid, s_loc, -jnp.inf)
        s_ctx = jnp.einsum('bqhgd,bkhd->bhgqk', q_blk, k_ctx).astype(jnp.float32) * ATTN_SCALE
        p = softmax_with_sink(jnp.concatenate([s_ctx, s_loc], axis=-1), sink)
        vv = jnp.concatenate([v_ctx, vb], axis=1)
        return jnp.einsum('bhgqk,bkhd->bqhgd', p.astype(vv.dtype), vv)

    out = lax.map(block, (jnp.arange(nb), qb))
    return jnp.moveaxis(out, 0, 1).reshape(q.shape)


def attn_mixer(h_lat, h_ctx, w_in, w_out, qn_a, kn_a, qn_b, kn_b, sink, cos, sin, with_ctx_out):
    bsz, n = h_lat.shape[:2]
    qa_l, ka_l, va_l, qb_l, kb_l, vb_l = project_heads(h_lat, w_in, qn_a, kn_a, qn_b, kn_b)
    qa_c, ka_c, va_c, qb_c, kb_c, vb_c = project_heads(h_ctx, w_in, qn_a, kn_a, qn_b, kn_b)
    qa_l, ka_l, qb_l, kb_l = [apply_rope(t, cos, sin) for t in (qa_l, ka_l, qb_l, kb_l)]
    sink = sink.reshape(N_KV_B, G_B)
    o_a = global_attention(qa_l, ka_l, va_l, ka_c, va_c).reshape(bsz, n, QA)
    o_b = window_attention(qb_l, kb_l, vb_l, kb_c, vb_c, sink).reshape(bsz, n, QB)
    out_lat = jnp.concatenate([o_a, o_b], axis=-1) @ w_out
    if not with_ctx_out:
        return out_lat, None
    n_ctx = h_ctx.shape[1]
    oa_c = context_attention(qa_c, ka_c, va_c, None).reshape(bsz, n_ctx, QA)
    ob_c = context_attention(qb_c, kb_c, vb_c, sink).reshape(bsz, n_ctx, QB)
    out_ctx = jnp.concatenate([oa_c, ob_c], axis=-1) @ w_out
    return out_lat, out_ctx


def short_conv_mixer(h, w_in, conv_w, w_out):
    n = h.shape[1]
    b_gate, c_gate, u = jnp.split(h @ w_in, 3, axis=-1)
    z = c_gate * u
    r = CONV_WIDTH // 2
    zp = jnp.pad(z, ((0, 0), (r, r), (0, 0)))
    conv = zp[:, 0:n] * conv_w[0]
    for j in range(1, CONV_WIDTH):
        conv = conv + zp[:, j:j + n] * conv_w[j]
    return (b_gate * conv) @ w_out


def swiglu(h, w_gate, w_up, w_down):
    return (jax.nn.silu(h @ w_gate) * (h @ w_up)) @ w_down


def moe_swiglu(h, router_w, w_gate, w_up, w_down):
    logits = (h @ router_w).astype(jnp.float32)
    top_vals, top_idx = lax.top_k(logits, TOP_K)
    weights = jax.nn.softmax(top_vals, axis=-1)
    gates = jnp.sum(jax.nn.one_hot(top_idx, N_EXPERTS, dtype=jnp.float32) * weights[..., None], axis=-2)
    gates = gates.astype(h.dtype)
    out = gates[..., 0:1] * swiglu(h, w_gate[0], w_up[0], w_down[0])
    for e in range(1, N_EXPERTS):
        out = out + gates[..., e:e + 1] * swiglu(h, w_gate[e], w_up[e], w_down[e])
    return out


def setup_inputs(seed: int = 0) -> dict:
    key = jax.random.key(seed)
    ks = jax.random.split(key, 25)
    f32 = jnp.float32
    D = D_MODEL
    NE, NO = N_ATTN_LAYERS, N_CONV_LAYERS

    def nrm(k, shape, scale):
        return jax.random.normal(k, shape, f32) * scale

    return {
        'x': nrm(ks[0], (BATCH, SEQ, D), 1.0),
        'c': nrm(ks[1], (BATCH, D), 1.0),
        'ctx': nrm(ks[2], (BATCH, CTX_LEN, D), 1.0),
        'c_ctx': nrm(ks[3], (D,), 1.0),
        'ada_w': nrm(ks[4], (DEPTH, D, N_ADA * D), 0.5 * D ** -0.5),
        'ada_b': nrm(ks[5], (DEPTH, N_ADA * D), 0.01),
        'norm1_g': 1.0 + nrm(ks[6], (DEPTH, D), 0.05),
        'norm2_g': 1.0 + nrm(ks[7], (DEPTH, D), 0.05),
        'attn_w_in': nrm(ks[8], (NE, D, IN_PROJ_WIDTH), D ** -0.5),
        'attn_w_out': nrm(ks[9], (NE, MIX_OUT_WIDTH, D), MIX_OUT_WIDTH ** -0.5),
        'qnorm_a': 1.0 + nrm(ks[10], (NE, HEAD_DIM), 0.05),
        'knorm_a': 1.0 + nrm(ks[11], (NE, HEAD_DIM), 0.05),
        'qnorm_b': 1.0 + nrm(ks[12], (NE, HEAD_DIM), 0.05),
        'knorm_b': 1.0 + nrm(ks[13], (NE, HEAD_DIM), 0.05),
        'sink_b': nrm(ks[14], (NE, N_HEADS_B), 0.5),
        'ffn_w_gate': nrm(ks[15], (NE, D, D_FF), D ** -0.5),
        'ffn_w_up': nrm(ks[16], (NE, D, D_FF), D ** -0.5),
        'ffn_w_down': nrm(ks[17], (NE, D_FF, D), D_FF ** -0.5),
        'conv_w_in': nrm(ks[18], (NO, D, 3 * D), D ** -0.5),
        'conv_w': nrm(ks[19], (NO, CONV_WIDTH, D), CONV_WIDTH ** -0.5),
        'conv_w_out': nrm(ks[20], (NO, D, D), D ** -0.5),
        'router_w': nrm(ks[21], (NO, D, N_EXPERTS), D ** -0.5),
        'moe_w_gate': nrm(ks[22], (NO, N_EXPERTS, D, D_FF_EXPERT), D ** -0.5),
        'moe_w_up': nrm(ks[23], (NO, N_EXPERTS, D, D_FF_EXPERT), D ** -0.5),
        'moe_w_down': nrm(ks[24], (NO, N_EXPERTS, D_FF_EXPERT, D), D_FF_EXPERT ** -0.5),
    }


def reference(x, c, ctx, c_ctx, ada_w, ada_b, norm1_g, norm2_g, attn_w_in, attn_w_out, qnorm_a, knorm_a, qnorm_b, knorm_b, sink_b, ffn_w_gate, ffn_w_up, ffn_w_down, conv_w_in, conv_w, conv_w_out, router_w, moe_w_gate, moe_w_up, moe_w_down):
    n_tok = x.shape[1]
    cos, sin = axial_rope_tables(n_tok)
    silu_c = jax.nn.silu(c)
    silu_cc = jax.nn.silu(c_ctx)
    xc = ctx
    for layer in range(DEPTH):
        i = layer // 2
        ctx_needed = any(j % 2 == 0 for j in range(layer + 1, DEPTH))
        sh1, sc1, g1, sh2, sc2, g2 = jnp.split((silu_c @ ada_w[layer] + ada_b[layer])[:, None, :], N_ADA, axis=-1)
        csh1, csc1, cg1, csh2, csc2, cg2 = jnp.split(silu_cc @ ada_w[layer] + ada_b[layer], N_ADA, axis=-1)

        h = rms_norm(x, norm1_g[layer]) * (1 + sc1) + sh1
        if layer % 2 == 0:
            hc = rms_norm(xc, norm1_g[layer]) * (1 + csc1) + csh1
            mix, mix_c = attn_mixer(h, hc, attn_w_in[i], attn_w_out[i], qnorm_a[i], knorm_a[i], qnorm_b[i], knorm_b[i], sink_b[i], cos, sin, ctx_needed)
        else:
            mix = short_conv_mixer(h, conv_w_in[i], conv_w[i], conv_w_out[i])
            if ctx_needed:
                hc = rms_norm(xc, norm1_g[layer]) * (1 + csc1) + csh1
                mix_c = short_conv_mixer(hc, conv_w_in[i], conv_w[i], conv_w_out[i])
        x = x + g1 * mix
        if ctx_needed:
            xc = xc + cg1 * mix_c

        if layer % 2 == 0:
            ffn = lambda t: swiglu(t, ffn_w_gate[i], ffn_w_up[i], ffn_w_down[i])
        else:
            ffn = lambda t: moe_swiglu(t, router_w[i], moe_w_gate[i], moe_w_up[i], moe_w_down[i])
        h = rms_norm(x, norm2_g[layer]) * (1 + sc2) + sh2
        x = x + g2 * ffn(h)
        if ctx_needed:
            hc = rms_norm(xc, norm2_g[layer]) * (1 + csc2) + csh2
            xc = xc + cg2 * ffn(hc)
    return x
```

```python
import functools

import jax
import jax.numpy as jnp
from jax import lax
from jax.experimental import pallas as pl
from jax.experimental.pallas import tpu as pltpu

F32 = jnp.float32
BF16 = jnp.bfloat16

HEAD_DIM = 64
N_KV = 2
GROUP = 4
N_HEADS = N_KV * GROUP
Q_WIDTH = N_HEADS * HEAD_DIM
KV_WIDTH = N_KV * HEAD_DIM
IN_PROJ_WIDTH = 2 * (Q_WIDTH + 2 * KV_WIDTH)
GRID_W = 64
WINDOW = 128
ROPE_THETA = 10000.0
ATTN_SCALE = HEAD_DIM ** -0.5
EPS = 1e-6
N_ADA = 6
LOG2E = 1.4426950408889634
LANES = 128
CHUNK = 256
NEG_BIG = -1e30
VMEM_LIMIT_BYTES = 56 * 1024 * 1024


def _params(*sem):
    return pltpu.CompilerParams(dimension_semantics=sem, vmem_limit_bytes=VMEM_LIMIT_BYTES)


def _pick(n, candidates):
    for t in candidates:
        if n % t == 0:
            return t
    return n


def _sigmoid(a):
    return 1.0 / (1.0 + jnp.exp(-a))


def _norm_mod(x, g, sc, sh):
    ms = jnp.mean(x * x, axis=-1, keepdims=True)
    return x * lax.rsqrt(ms + EPS) * g * (1.0 + sc) + sh


def _mod_kernel(c_ref, w_ref, b_ref, o_ref):
    c = c_ref[...]
    s = c * _sigmoid(c)
    o_ref[...] = jnp.dot(s, w_ref[...], preferred_element_type=F32,
                         precision=lax.Precision.HIGHEST) + b_ref[...]


def _modulation(cc, ada_w, ada_b):
    depth, d, n = ada_w.shape
    tn = _pick(n, (1536, 1024, 512))
    rows = cc.shape[0]
    return pl.pallas_call(
        _mod_kernel,
        grid=(depth, n // tn),
        in_specs=[
            pl.BlockSpec((rows, d), lambda l, j: (0, 0)),
            pl.BlockSpec((None, d, tn), lambda l, j: (l, 0, j)),
            pl.BlockSpec((None, 1, tn), lambda l, j: (l, 0, j)),
        ],
        out_specs=pl.BlockSpec((None, rows, tn), lambda l, j: (l, 0, j)),
        out_shape=jax.ShapeDtypeStruct((depth, rows, n), F32),
        compiler_params=_params("arbitrary", "arbitrary"),
    )(cc, ada_w, ada_b.reshape(depth, 1, n))


def _vec_spec(arr):
    d = arr.shape[-1]
    if arr.shape[0] == 1:
        return pl.BlockSpec((None, 1, d), lambda b, i, *_: (0, 0, 0))
    return pl.BlockSpec((None, 1, d), lambda b, i, *_: (b, 0, 0))


def _attn_in_kernel(x_ref, g_ref, sc_ref, sh_ref, w_ref, gains_ref, cos_ref, sin_ref, bd_ref,
                    qa_ref, ka_ref, va_ref, qb_ref, kb_ref, vb_ref, *, rope, tm):
    h = _norm_mod(x_ref[...], g_ref[...], sc_ref[...], sh_ref[...]).astype(BF16)
    y = jnp.dot(h, w_ref[...], preferred_element_type=F32)
    lane = lax.broadcasted_iota(jnp.int32, (tm, LANES), 1)
    even = (lane & 1) == 0
    bd = bd_ref[...]

    def head_norm(yc, gain):
        ss = jnp.dot((yc * yc).astype(BF16), bd, preferred_element_type=F32)
        t = yc * lax.rsqrt(ss * (1.0 / HEAD_DIM) + EPS) * gain
        if rope:
            partner = jnp.where(even, pltpu.roll(t, LANES - 1, 1), pltpu.roll(t, 1, 1))
            t = t * cos_ref[...] + partner * sin_ref[...]
        return t

    groups = ((0, qa_ref, ka_ref, va_ref, 0), (Q_WIDTH + 2 * KV_WIDTH, qb_ref, kb_ref, vb_ref, 2))
    for col0, q_ref, k_ref, v_ref, grow in groups:
        gq = gains_ref[grow:grow + 1, :]
        gk = gains_ref[grow + 1:grow + 2, :]
        for c in range(Q_WIDTH // LANES):
            t = head_norm(y[:, col0 + c * LANES:col0 + (c + 1) * LANES], gq)
            q_ref[c * LANES:(c + 1) * LANES, :] = t.T.astype(BF16)
        kcol = col0 + Q_WIDTH
        k_ref[...] = head_norm(y[:, kcol:kcol + KV_WIDTH], gk).astype(BF16)
        vt = y[:, kcol + KV_WIDTH:kcol + 2 * KV_WIDTH].T
        for j in range(tm // CHUNK):
            v_ref[j] = vt[:, j * CHUNK:(j + 1) * CHUNK].astype(BF16)


def _attn_in(x, g, sc, sh, w_bf16, gains, cos_t, sin_t, bd, *, rope):
    b, t, d = x.shape
    tm = _pick(t, (512, 256))
    nch = t // CHUNK
    kern = functools.partial(_attn_in_kernel, rope=rope, tm=tm)
    q_spec = pl.BlockSpec((None, Q_WIDTH, tm), lambda bb, i: (bb, 0, i))
    k_spec = pl.BlockSpec((None, tm, KV_WIDTH), lambda bb, i: (bb, i, 0))
    v_spec = pl.BlockSpec((None, tm // CHUNK, KV_WIDTH, CHUNK), lambda bb, i: (bb, i, 0, 0))
    q_shape = jax.ShapeDtypeStruct((b, Q_WIDTH, t), BF16)
    k_shape = jax.ShapeDtypeStruct((b, t, KV_WIDTH), BF16)
    v_shape = jax.ShapeDtypeStruct((b, nch, KV_WIDTH, CHUNK), BF16)
    return pl.pallas_call(
        kern,
        grid=(b, t // tm),
        in_specs=[
            pl.BlockSpec((None, tm, d), lambda bb, i: (bb, i, 0)),
            pl.BlockSpec((1, d), lambda bb, i: (0, 0)),
            _vec_spec(sc), _vec_spec(sh),
            pl.BlockSpec((d, IN_PROJ_WIDTH), lambda bb, i: (0, 0)),
            pl.BlockSpec((4, LANES), lambda bb, i: (0, 0)),
            pl.BlockSpec((tm, LANES), lambda bb, i: (i, 0)),
            pl.BlockSpec((tm, LANES), lambda bb, i: (i, 0)),
            pl.BlockSpec((LANES, LANES), lambda bb, i: (0, 0)),
        ],
        out_specs=[q_spec, k_spec, v_spec, q_spec, k_spec, v_spec],
        out_shape=[q_shape, k_shape, v_shape, q_shape, k_shape, v_shape],
        compiler_params=_params("parallel", "parallel"),
    )(x, g, sc, sh, w_bf16, gains, cos_t, sin_t, bd)


def _attn_kernel(q_ref, k_ref, v_ref, sink_ref, o_ref, *, mode, use_sink, n_lat, tq):
    i = pl.program_id(1)
    zeros = jnp.zeros((HEAD_DIM, tq), BF16)
    if mode == "window":
        row = lax.broadcasted_iota(jnp.int32, (CHUNK, tq), 0)
        col = lax.broadcasted_iota(jnp.int32, (CHUNK, tq), 1)
        rel0 = col - row

    for h in range(N_HEADS):
        kv = h // GROUP
        qh = q_ref[h * HEAD_DIM:(h + 1) * HEAD_DIM, :]
        qpad = jnp.concatenate([qh, zeros] if kv == 0 else [zeros, qh], axis=0)

        def update(carry, c, mask, qpad=qpad, kv=kv):
            m, l, acc = carry
            start = c * CHUNK if isinstance(c, int) else pl.multiple_of(c * CHUNK, CHUNK)
            s = jnp.dot(k_ref[pl.ds(start, CHUNK), :], qpad, preferred_element_type=F32)
            if mask is not None:
                s = jnp.where(mask, s, -jnp.inf)
            m_new = jnp.maximum(m, jnp.max(s, axis=0, keepdims=True))
            alpha = jnp.exp2(m - m_new)
            p = jnp.exp2(s - m_new)
            l = alpha * l + jnp.sum(p, axis=0, keepdims=True)
            vc = v_ref[c, kv * HEAD_DIM:(kv + 1) * HEAD_DIM, :]
            acc = alpha * acc + jnp.dot(vc, p.astype(BF16), preferred_element_type=F32)
            return m_new, l, acc

        if use_sink:
            m0 = jnp.full((1, tq), sink_ref[h] * LOG2E, F32)
            l0 = jnp.ones((1, tq), F32)
        else:
            m0 = jnp.full((1, tq), NEG_BIG, F32)
            l0 = jnp.zeros((1, tq), F32)
        carry = (m0, l0, jnp.zeros((HEAD_DIM, tq), F32))

        carry = update(carry, 0, None)
        if mode == "global":
            carry = lax.fori_loop(1, 1 + n_lat, lambda c, cr: update(cr, c, None), carry)
        elif mode == "window":
            for d in (-1, 0, 1):
                cl = i + d
                ok = jnp.logical_and(cl >= 0, cl < n_lat)
                c = 1 + jnp.clip(cl, 0, n_lat - 1)
                rel = rel0 - d * CHUNK
                mask = jnp.logical_and(jnp.abs(rel) <= WINDOW, ok)
                carry = update(carry, c, mask)
        m, l, acc = carry
        o_ref[h * HEAD_DIM:(h + 1) * HEAD_DIM, :] = (acc * (1.0 / l)).astype(BF16)


def _attention(q_t, k_all, v_all, sink, *, mode, use_sink):
    b, _, tq_total = q_t.shape
    nk = k_all.shape[1]
    nc = v_all.shape[1]
    tq = CHUNK
    kern = functools.partial(_attn_kernel, mode=mode, use_sink=use_sink, n_lat=nc - 1, tq=tq)
    return pl.pallas_call(
        kern,
        grid=(b, tq_total // tq),
        in_specs=[
            pl.BlockSpec((None, Q_WIDTH, tq), lambda bb, i: (bb, 0, i)),
            pl.BlockSpec((None, nk, KV_WIDTH), lambda bb, i: (bb, 0, 0)),
            pl.BlockSpec((None, nc, KV_WIDTH, CHUNK), lambda bb, i: (bb, 0, 0, 0)),
            pl.BlockSpec(memory_space=pltpu.SMEM),
        ],
        out_specs=pl.BlockSpec((None, Q_WIDTH, tq), lambda bb, i: (bb, 0, i)),
        out_shape=jax.ShapeDtypeStruct((b, Q_WIDTH, tq_total), BF16),
        compiler_params=_params("parallel", "parallel"),
    )(q_t, k_all, v_all, sink)


_TN_DIMS = (((0,), (0,)), ((), ()))


def _attn_out_kernel(oa_ref, ob_ref, w_ref, x_ref, g1_ref, n2_ref, sc2_ref, sh2_ref, x1_ref, h2_ref):
    mix = lax.dot_general(oa_ref[...], w_ref[0:Q_WIDTH, :], _TN_DIMS, preferred_element_type=F32)
    mix = mix + lax.dot_general(ob_ref[...], w_ref[Q_WIDTH:2 * Q_WIDTH, :], _TN_DIMS,
                                preferred_element_type=F32)
    x1 = x_ref[...] + g1_ref[...] * mix
    x1_ref[...] = x1
    h2_ref[...] = _norm_mod(x1, n2_ref[...], sc2_ref[...], sh2_ref[...]).astype(BF16)


def _attn_out(oa, ob, w_bf16, x, g1, n2, sc2, sh2):
    b, t, d = x.shape
    tm = _pick(t, (512, 256))
    o_spec = pl.BlockSpec((None, Q_WIDTH, tm), lambda bb, i: (bb, 0, i))
    x_spec = pl.BlockSpec((None, tm, d), lambda bb, i: (bb, i, 0))
    return pl.pallas_call(
        _attn_out_kernel,
        grid=(b, t // tm),
        in_specs=[o_spec, o_spec,
                  pl.BlockSpec((2 * Q_WIDTH, d), lambda bb, i: (0, 0)),
                  x_spec, _vec_spec(g1),
                  pl.BlockSpec((1, d), lambda bb, i: (0, 0)),
                  _vec_spec(sc2), _vec_spec(sh2)],
        out_specs=[x_spec, x_spec],
        out_shape=[jax.ShapeDtypeStruct((b, t, d), F32), jax.ShapeDtypeStruct((b, t, d), BF16)],
        compiler_params=_params("parallel", "parallel"),
    )(oa, ob, w_bf16, x, g1, n2, sc2, sh2)


def _ffn_kernel(*refs, gated, tm):
    if gated:
        h_ref, wg_ref, wu_ref, wd_ref, gate_ref, x_ref, g2_ref, o_ref, acc_ref = refs
    else:
        h_ref, wg_ref, wu_ref, wd_ref, x_ref, g2_ref, o_ref, acc_ref = refs
    e = pl.program_id(2)
    f = pl.program_id(3)
    first = jnp.logical_and(e == 0, f == 0)
    last = jnp.logical_and(e == pl.num_programs(2) - 1, f == pl.num_programs(3) - 1)

    @pl.when(first)
    def _():
        acc_ref[...] = jnp.zeros_like(acc_ref)

    h = h_ref[...]
    a = jnp.dot(h, wg_ref[...], preferred_element_type=F32)
    u = jnp.dot(h, wu_ref[...], preferred_element_type=F32)
    act = (a * _sigmoid(a) * u).astype(BF16)
    y = jnp.dot(act, wd_ref[...], preferred_element_type=F32)
    if gated:
        lane = lax.broadcasted_iota(jnp.int32, (tm, LANES), 1)
        ge = jnp.sum(jnp.where(lane == e, gate_ref[...], 0.0), axis=-1, keepdims=True)
        y = y * ge
    acc_ref[...] += y

    @pl.when(last)
    def _():
        o_ref[...] = x_ref[...] + g2_ref[...] * acc_ref[...]


def _ffn(h2, wg, wu, wd, gates, x1, g2):
    b, t, d = x1.shape
    n_e, _, f = wg.shape
    tm = _pick(t, (512, 256))
    tf = _pick(f, (1792, 1408, 1024, 512))
    gated = gates is not None
    x_spec = pl.BlockSpec((None, tm, d), lambda bb, i, e, j: (bb, i, 0))
    in_specs = [x_spec,
                pl.BlockSpec((None, d, tf), lambda bb, i, e, j: (e, 0, j)),
                pl.BlockSpec((None, d, tf), lambda bb, i, e, j: (e, 0, j)),
                pl.BlockSpec((None, tf, d), lambda bb, i, e, j: (e, j, 0))]
    args = [h2, wg, wu, wd]
    if gated:
        in_specs.append(pl.BlockSpec((None, tm, LANES), lambda bb, i, e, j: (bb, i, 0)))
        args.append(gates)
    in_specs += [x_spec, _vec_spec(g2)]
    args += [x1, g2]
    return pl.pallas_call(
        functools.partial(_ffn_kernel, gated=gated, tm=tm),
        grid=(b, t // tm, n_e, f // tf),
        in_specs=in_specs,
        out_specs=x_spec,
        out_shape=jax.ShapeDtypeStruct((b, t, d), F32),
        scratch_shapes=[pltpu.VMEM((tm, d), F32)],
        compiler_params=_params("parallel", "parallel", "arbitrary", "arbitrary"),
    )(*args)


def _conv_in_kernel(x_ref, g_ref, sc_ref, sh_ref, w_ref, bg_ref, z_ref, *, d):
    h = _norm_mod(x_ref[...], g_ref[...], sc_ref[...], sh_ref[...]).astype(BF16)
    y = jnp.dot(h, w_ref[...], preferred_element_type=F32)
    bg_ref[...] = y[:, 0:d].astype(BF16)
    z_ref[...] = (y[:, d:2 * d] * y[:, 2 * d:3 * d]).astype(BF16)


def _conv_in(x, g, sc, sh, w_bf16):
    b, t, d = x.shape
    tm = _pick(t, (512, 256))
    x_spec = pl.BlockSpec((None, tm, d), lambda bb, i: (bb, i, 0))
    return pl.pallas_call(
        functools.partial(_conv_in_kernel, d=d),
        grid=(b, t // tm),
        in_specs=[x_spec, pl.BlockSpec((1, d), lambda bb, i: (0, 0)), _vec_spec(sc), _vec_spec(sh),
                  pl.BlockSpec((d, 3 * d), lambda bb, i: (0, 0))],
        out_specs=[x_spec, x_spec],
        out_shape=[jax.ShapeDtypeStruct((b, t, d), BF16), jax.ShapeDtypeStruct((b, t, d), BF16)],
        compiler_params=_params("parallel", "parallel"),
    )(x, g, sc, sh, w_bf16)


HALO = 16


def _conv_out_kernel(z_ref, zp_ref, zn_ref, bg_ref, cw_ref, w_ref, x_ref, g1_ref, n2_ref, sc2_ref,
                     sh2_ref, rw_ref, x1_ref, h2_ref, gate_ref, *, tm, n_experts):
    i = pl.program_id(1)
    z = z_ref[...].astype(F32)
    prev = jnp.where(i > 0, zp_ref[HALO - 1:HALO, :].astype(F32), 0.0)
    nxt = jnp.where(i < pl.num_programs(1) - 1, zn_ref[0:1, :].astype(F32), 0.0)
    row = lax.broadcasted_iota(jnp.int32, z.shape, 0)
    z_dn = jnp.where(row == 0, prev, pltpu.roll(z, 1, 0))
    z_up = jnp.where(row == tm - 1, nxt, pltpu.roll(z, tm - 1, 0))
    conv = z_dn * cw_ref[0:1, :] + z * cw_ref[1:2, :] + z_up * cw_ref[2:3, :]
    v = (bg_ref[...].astype(F32) * conv).astype(BF16)
    mix = jnp.dot(v, w_ref[...], preferred_element_type=F32)
    x1 = x_ref[...] + g1_ref[...] * mix
    x1_ref[...] = x1
    h2 = _norm_mod(x1, n2_ref[...], sc2_ref[...], sh2_ref[...])
    h2_ref[...] = h2.astype(BF16)

    logits = jnp.dot(h2, rw_ref[...], preferred_element_type=F32, precision=lax.Precision.HIGHEST)
    lane = lax.broadcasted_iota(jnp.int32, logits.shape, 1)
    lanef = lane.astype(F32)
    logits = jnp.where(lane < n_experts, logits, -jnp.inf)
    m1 = jnp.max(logits, axis=-1, keepdims=True)
    i1 = jnp.min(jnp.where(logits == m1, lanef, float(LANES)), axis=-1, keepdims=True)
    rest = jnp.where(lanef == i1, -jnp.inf, logits)
    m2 = jnp.max(rest, axis=-1, keepdims=True)
    i2 = jnp.min(jnp.where(rest == m2, lanef, float(LANES)), axis=-1, keepdims=True)
    e2 = jnp.exp(m2 - m1)
    w1 = 1.0 / (1.0 + e2)
    w2 = e2 * w1
    gate_ref[...] = jnp.where(lanef == i1, w1, 0.0) + jnp.where(lanef == i2, w2, 0.0)


def _conv_out(z, bg, conv_w, w_bf16, x, g1, n2, sc2, sh2, router_pad, n_experts):
    b, t, d = x.shape
    tm = _pick(t, (512, 256))
    hb = tm // HALO
    n_halo = t // HALO
    x_spec = pl.BlockSpec((None, tm, d), lambda bb, i: (bb, i, 0))
    prev_spec = pl.BlockSpec((None, HALO, d), lambda bb, i: (bb, jnp.maximum(i * hb - 1, 0), 0))
    next_spec = pl.BlockSpec((None, HALO, d), lambda bb, i: (bb, jnp.minimum((i + 1) * hb, n_halo - 1), 0))
    full = lambda r, c: pl.BlockSpec((r, c), lambda bb, i: (0, 0))
    return pl.pallas_call(
        functools.partial(_conv_out_kernel, tm=tm, n_experts=n_experts),
        grid=(b, t // tm),
        in_specs=[x_spec, prev_spec, next_spec, x_spec, full(3, d), full(d, d), x_spec, _vec_spec(g1),
                  full(1, d), _vec_spec(sc2), _vec_spec(sh2), full(d, LANES)],
        out_specs=[x_spec, x_spec, pl.BlockSpec((None, tm, LANES), lambda bb, i: (bb, i, 0))],
        out_shape=[jax.ShapeDtypeStruct((b, t, d), F32), jax.ShapeDtypeStruct((b, t, d), BF16),
                   jax.ShapeDtypeStruct((b, t, LANES), F32)],
        compiler_params=_params("parallel", "parallel"),
    )(z, z, z, bg, conv_w, w_bf16, x, g1, n2, sc2, sh2, router_pad)


def _rope_tables(n_tokens):
    rows = n_tokens // GRID_W
    row, col = jnp.meshgrid(jnp.arange(rows, dtype=F32), jnp.arange(GRID_W, dtype=F32), indexing="ij")
    half = HEAD_DIM // 2
    inv_freq = ROPE_THETA ** (-jnp.arange(0, half, 2, dtype=F32) / half)
    ang = jnp.concatenate([row.reshape(-1, 1) * inv_freq, col.reshape(-1, 1) * inv_freq], axis=-1)
    cos = jnp.repeat(jnp.cos(ang), 2, axis=-1)
    sign = jnp.tile(jnp.array([-1.0, 1.0], F32), HEAD_DIM // 2)
    sin = jnp.repeat(jnp.sin(ang), 2, axis=-1) * sign
    reps = LANES // HEAD_DIM
    return jnp.tile(cos, (1, reps)), jnp.tile(sin, (1, reps))


def kernel(x, c, ctx, c_ctx, ada_w, ada_b, norm1_g, norm2_g, attn_w_in, attn_w_out, qnorm_a, knorm_a, qnorm_b, knorm_b, sink_b, ffn_w_gate, ffn_w_up, ffn_w_down, conv_w_in, conv_w, conv_w_out, router_w, moe_w_gate, moe_w_up, moe_w_down):
    bsz, n_tok, d = x.shape
    n_ctx = ctx.shape[1]
    depth = ada_w.shape[0]
    n_experts = router_w.shape[-1]
    assert n_ctx == CHUNK and n_tok % CHUNK == 0 and bsz + 1 <= 8
    assert attn_w_in.shape[-1] == IN_PROJ_WIDTH and d % LANES == 0 and n_experts <= LANES

    cc = jnp.zeros((8, d), F32).at[:bsz].set(c).at[bsz].set(c_ctx)
    mod = _modulation(cc, ada_w, ada_b)

    def mod_vec(layer, j, is_ctx):
        m = mod[layer, :, j * d:(j + 1) * d]
        return m[bsz:bsz + 1].reshape(1, 1, d) if is_ctx else m[:bsz].reshape(bsz, 1, d)

    cos_l, sin_l = _rope_tables(n_tok)
    cos_c = jnp.ones((n_ctx, LANES), F32)
    sin_c = jnp.zeros((n_ctx, LANES), F32)
    head_id = jnp.arange(LANES) // HEAD_DIM
    bd = (head_id[:, None] == head_id[None, :]).astype(BF16)
    no_sink = jnp.zeros((N_HEADS,), F32)
    tile2 = lambda v: jnp.tile(v, LANES // HEAD_DIM)

    xc = ctx
    for layer in range(depth):
        i = layer // 2
        ctx_needed = any(j % 2 == 0 for j in range(layer + 1, depth))
        n1 = norm1_g[layer].reshape(1, d)
        n2 = norm2_g[layer].reshape(1, d)
        mv = lambda j, is_ctx: mod_vec(layer, j, is_ctx)

        if layer % 2 == 0:
            w_in = attn_w_in[i].astype(BF16)
            w_out = attn_w_out[i].astype(BF16)
            qscale = ATTN_SCALE * LOG2E
            gains = jnp.stack([tile2(qnorm_a[i]) * qscale, tile2(knorm_a[i]),
                               tile2(qnorm_b[i]) * qscale, tile2(knorm_b[i])]).astype(F32)
            sink = sink_b[i].astype(F32)
            qa_l, ka_l, va_l, qb_l, kb_l, vb_l = _attn_in(
                x, n1, mv(1, False), mv(0, False), w_in, gains, cos_l, sin_l, bd, rope=True)
            qa_c, ka_c, va_c, qb_c, kb_c, vb_c = _attn_in(
                xc, n1, mv(1, True), mv(0, True), w_in, gains, cos_c, sin_c, bd, rope=False)
            ka_all = jnp.concatenate([ka_c, ka_l], axis=1)
            va_all = jnp.concatenate([va_c, va_l], axis=1)
            kb_all = jnp.concatenate([kb_c, kb_l], axis=1)
            vb_all = jnp.concatenate([vb_c, vb_l], axis=1)
            oa = _attention(qa_l, ka_all, va_all, no_sink, mode="global", use_sink=False)
            ob = _attention(qb_l, kb_all, vb_all, sink, mode="window", use_sink=True)
            x, h2 = _attn_out(oa, ob, w_out, x, mv(2, False), n2, mv(4, False), mv(3, False))
            if ctx_needed:
                oa_c = _attention(qa_c, ka_c, va_c, no_sink, mode="ctx", use_sink=False)
                ob_c = _attention(qb_c, kb_c, vb_c, sink, mode="ctx", use_sink=True)
                xc, h2c = _attn_out(oa_c, ob_c, w_out, xc, mv(2, True), n2, mv(4, True), mv(3, True))
            wg = ffn_w_gate[i].astype(BF16)[None]
            wu = ffn_w_up[i].astype(BF16)[None]
            wd = ffn_w_down[i].astype(BF16)[None]
            x = _ffn(h2, wg, wu, wd, None, x, mv(5, False))
            if ctx_needed:
                xc = _ffn(h2c, wg, wu, wd, None, xc, mv(5, True))
        else:
            w_in = conv_w_in[i].astype(BF16)
            w_out = conv_w_out[i].astype(BF16)
            cw = conv_w[i].astype(F32)
            router_pad = jnp.zeros((d, LANES), F32).at[:, :n_experts].set(router_w[i])
            wg = moe_w_gate[i].astype(BF16)
            wu = moe_w_up[i].astype(BF16)
            wd = moe_w_down[i].astype(BF16)
            bg, z = _conv_in(x, n1, mv(1, False), mv(0, False), w_in)
            x, h2, gates = _conv_out(z, bg, cw, w_out, x, mv(2, False), n2, mv(4, False), mv(3, False),
                                     router_pad, n_experts)
            x = _ffn(h2, wg, wu, wd, gates, x, mv(5, False))
            if ctx_needed:
                bg, z = _conv_in(xc, n1, mv(1, True), mv(0, True), w_in)
                xc, h2c, gates_c = _conv_out(z, bg, cw, w_out, xc, mv(2, True), n2, mv(4, True),
                                             mv(3, True), router_pad, n_experts)
                xc = _ffn(h2c, wg, wu, wd, gates_c, xc, mv(5, True))
    return x
```

```python
import functools

import jax
import jax.numpy as jnp
from jax import lax
from jax.experimental import pallas as pl
from jax.experimental.pallas import tpu as pltpu

F32 = jnp.float32
BF16 = jnp.bfloat16

HEAD_DIM = 64
N_KV = 2
GROUP = 4
N_HEADS = N_KV * GROUP
Q_WIDTH = N_HEADS * HEAD_DIM
KV_WIDTH = N_KV * HEAD_DIM
IN_PROJ_WIDTH = 2 * (Q_WIDTH + 2 * KV_WIDTH)
GRID_W = 64
WINDOW = 128
ROPE_THETA = 10000.0
ATTN_SCALE = HEAD_DIM ** -0.5
EPS = 1e-6
N_ADA = 6
LOG2E = 1.4426950408889634
LANES = 128
CHUNK = 256
NEG_BIG = -1e30
VMEM_LIMIT_BYTES = 56 * 1024 * 1024


def _params(*sem):
    return pltpu.CompilerParams(dimension_semantics=sem, vmem_limit_bytes=VMEM_LIMIT_BYTES)


def _pick(n, candidates):
    for t in candidates:
        if n % t == 0:
            return t
    return n


def _sigmoid(a):
    return 1.0 / (1.0 + jnp.exp(-a))


def _norm_mod(x, g, sc, sh):
    ms = jnp.mean(x * x, axis=-1, keepdims=True)
    return x * lax.rsqrt(ms + EPS) * g * (1.0 + sc) + sh


def _mod_kernel(c_ref, w_ref, b_ref, o_ref):
    c = c_ref[...]
    s = c * _sigmoid(c)
    o_ref[...] = jnp.dot(s, w_ref[...], preferred_element_type=F32,
                         precision=lax.Precision.HIGHEST) + b_ref[...]


def _modulation(cc, ada_w, ada_b):
    depth, d, n = ada_w.shape
    tn = _pick(n, (1536, 1024, 512))
    rows = cc.shape[0]
    return pl.pallas_call(
        _mod_kernel,
        grid=(depth, n // tn),
        in_specs=[
            pl.BlockSpec((rows, d), lambda l, j: (0, 0)),
            pl.BlockSpec((None, d, tn), lambda l, j: (l, 0, j)),
            pl.BlockSpec((None, 1, tn), lambda l, j: (l, 0, j)),
        ],
        out_specs=pl.BlockSpec((None, rows, tn), lambda l, j: (l, 0, j)),
        out_shape=jax.ShapeDtypeStruct((depth, rows, n), F32),
        compiler_params=_params("arbitrary", "arbitrary"),
        name="modulation",
    )(cc, ada_w, ada_b.reshape(depth, 1, n))


def _vec_spec(arr):
    d = arr.shape[-1]
    if arr.shape[0] == 1:
        return pl.BlockSpec((None, 1, d), lambda b, i, *_: (0, 0, 0))
    return pl.BlockSpec((None, 1, d), lambda b, i, *_: (b, 0, 0))


def _attn_in_kernel(x_ref, g_ref, sc_ref, sh_ref, w_ref, gains_ref, cos_ref, sin_ref, bd_ref,
                    qa_ref, ka_ref, va_ref, qb_ref, kb_ref, vb_ref, *, rope, tm):
    h = _norm_mod(x_ref[...], g_ref[...], sc_ref[...], sh_ref[...]).astype(BF16)
    y = jnp.dot(h, w_ref[...], preferred_element_type=F32)
    lane = lax.broadcasted_iota(jnp.int32, (tm, LANES), 1)
    even = (lane & 1) == 0
    bd = bd_ref[...]

    def head_norm(yc, gain):
        ss = jnp.dot((yc * yc).astype(BF16), bd, preferred_element_type=F32)
        t = yc * lax.rsqrt(ss * (1.0 / HEAD_DIM) + EPS) * gain
        if rope:
            partner = jnp.where(even, pltpu.roll(t, LANES - 1, 1), pltpu.roll(t, 1, 1))
            t = t * cos_ref[...] + partner * sin_ref[...]
        return t

    groups = ((0, qa_ref, ka_ref, va_ref, 0), (Q_WIDTH + 2 * KV_WIDTH, qb_ref, kb_ref, vb_ref, 2))
    for col0, q_ref, k_ref, v_ref, grow in groups:
        gq = gains_ref[grow:grow + 1, :]
        gk = gains_ref[grow + 1:grow + 2, :]
        for c in range(Q_WIDTH // LANES):
            t = head_norm(y[:, col0 + c * LANES:col0 + (c + 1) * LANES], gq)
            q_ref[c * LANES:(c + 1) * LANES, :] = t.T.astype(BF16)
        kcol = col0 + Q_WIDTH
        k_ref[...] = head_norm(y[:, kcol:kcol + KV_WIDTH], gk).astype(BF16)
        vt = y[:, kcol + KV_WIDTH:kcol + 2 * KV_WIDTH].T
        for j in range(tm // CHUNK):
            v_ref[j] = vt[:, j * CHUNK:(j + 1) * CHUNK].astype(BF16)


def _attn_in(x, g, sc, sh, w_bf16, gains, cos_t, sin_t, bd, *, rope):
    b, t, d = x.shape
    tm = _pick(t, (512, 256))
    nch = t // CHUNK
    kern = functools.partial(_attn_in_kernel, rope=rope, tm=tm)
    q_spec = pl.BlockSpec((None, Q_WIDTH, tm), lambda bb, i: (bb, 0, i))
    k_spec = pl.BlockSpec((None, tm, KV_WIDTH), lambda bb, i: (bb, i, 0))
    v_spec = pl.BlockSpec((None, tm // CHUNK, KV_WIDTH, CHUNK), lambda bb, i: (bb, i, 0, 0))
    q_shape = jax.ShapeDtypeStruct((b, Q_WIDTH, t), BF16)
    k_shape = jax.ShapeDtypeStruct((b, t, KV_WIDTH), BF16)
    v_shape = jax.ShapeDtypeStruct((b, nch, KV_WIDTH, CHUNK), BF16)
    return pl.pallas_call(
        kern,
        grid=(b, t // tm),
        in_specs=[
            pl.BlockSpec((None, tm, d), lambda bb, i: (bb, i, 0)),
            pl.BlockSpec((1, d), lambda bb, i: (0, 0)),
            _vec_spec(sc), _vec_spec(sh),
            pl.BlockSpec((d, IN_PROJ_WIDTH), lambda bb, i: (0, 0)),
            pl.BlockSpec((4, LANES), lambda bb, i: (0, 0)),
            pl.BlockSpec((tm, LANES), lambda bb, i: (i, 0)),
            pl.BlockSpec((tm, LANES), lambda bb, i: (i, 0)),
            pl.BlockSpec((LANES, LANES), lambda bb, i: (0, 0)),
        ],
        out_specs=[q_spec, k_spec, v_spec, q_spec, k_spec, v_spec],
        out_shape=[q_shape, k_shape, v_shape, q_shape, k_shape, v_shape],
        compiler_params=_params("parallel", "parallel"),
        name="attn_in",
    )(x, g, sc, sh, w_bf16, gains, cos_t, sin_t, bd)


def _attn_kernel(q_ref, k_ref, v_ref, sink_ref, o_ref, qpad_ref, s_ref, cmax_ref, m_ref, l_ref,
                 acc_ref, *, mode, use_sink, n_lat, tq):
    i = pl.program_id(1)
    w = GROUP * tq
    zeros = jnp.zeros((HEAD_DIM, w), BF16)
    for kv in range(N_KV):
        heads = range(kv * GROUP, (kv + 1) * GROUP)
        q4 = jnp.concatenate([q_ref[h * HEAD_DIM:(h + 1) * HEAD_DIM, :] for h in heads], axis=1)
        qpad_ref[kv] = jnp.concatenate([q4, zeros] if kv == 0 else [zeros, q4], axis=0)
        if use_sink:
            m_ref[kv] = jnp.concatenate(
                [jnp.full((1, tq), sink_ref[h] * LOG2E, F32) for h in heads], axis=1)
            l_ref[kv] = jnp.ones((1, w), F32)
        else:
            m_ref[kv] = jnp.full((1, w), NEG_BIG, F32)
            l_ref[kv] = jnp.zeros((1, w), F32)
        acc_ref[kv] = jnp.zeros((HEAD_DIM, w), F32)

    def scores(c, slot, mask):
        start = c * CHUNK if isinstance(c, int) else pl.multiple_of(c * CHUNK, CHUNK)
        kc = k_ref[pl.ds(start, CHUNK), :]
        for kv in range(N_KV):
            s = jnp.dot(kc, qpad_ref[kv], preferred_element_type=F32)
            if mask is not None:
                s = jnp.where(mask, s, -jnp.inf)
            s_ref[slot, kv] = s
            cmax_ref[slot, kv] = jnp.max(s, axis=0, keepdims=True)

    def absorb(c, slot):
        for kv in range(N_KV):
            m = m_ref[kv]
            m_new = jnp.maximum(m, cmax_ref[slot, kv])
            alpha = jnp.exp2(m - m_new)
            p = jnp.exp2(s_ref[slot, kv] - m_new)
            l_ref[kv] = alpha * l_ref[kv] + jnp.sum(p, axis=0, keepdims=True)
            m_ref[kv] = m_new
            vc = v_ref[c, kv * HEAD_DIM:(kv + 1) * HEAD_DIM, :]
            acc_ref[kv] = alpha * acc_ref[kv] + jnp.dot(vc, p.astype(BF16),
                                                        preferred_element_type=F32)

    scores(0, 0, None)
    if mode == "global":
        def body(j, carry):
            c = 2 * j
            scores(c + 1, 1, None)
            absorb(c, 0)
            scores(c + 2, 0, None)
            absorb(c + 1, 1)
            return carry
        lax.fori_loop(0, n_lat // 2, body, 0)
        c_end = 2 * (n_lat // 2)
        if n_lat % 2:
            scores(c_end + 1, 1, None)
            absorb(c_end, 0)
            absorb(c_end + 1, 1)
        else:
            absorb(c_end, 0)
    elif mode == "window":
        row = lax.broadcasted_iota(jnp.int32, (CHUNK, w), 0)
        col = lax.broadcasted_iota(jnp.int32, (CHUNK, w), 1) & (tq - 1)
        rel0 = col - row
        prev_c, prev_slot = 0, 0
        for d in (-1, 0, 1):
            cl = i + d
            ok = jnp.logical_and(cl >= 0, cl < n_lat)
            c = 1 + jnp.clip(cl, 0, n_lat - 1)
            mask = jnp.logical_and(jnp.abs(rel0 - d * CHUNK) <= WINDOW, ok)
            scores(c, 1 - prev_slot, mask)
            absorb(prev_c, prev_slot)
            prev_c, prev_slot = c, 1 - prev_slot
        absorb(prev_c, prev_slot)
    else:
        absorb(0, 0)

    for kv in range(N_KV):
        out = acc_ref[kv] * (1.0 / l_ref[kv])
        for g in range(GROUP):
            h = kv * GROUP + g
            o_ref[h * HEAD_DIM:(h + 1) * HEAD_DIM, :] = out[:, g * tq:(g + 1) * tq].astype(BF16)


def _attention(q_t, k_all, v_all, sink, *, mode, use_sink):
    b, _, tq_total = q_t.shape
    nk = k_all.shape[1]
    nc = v_all.shape[1]
    tq = CHUNK
    kern = functools.partial(_attn_kernel, mode=mode, use_sink=use_sink, n_lat=nc - 1, tq=tq)
    return pl.pallas_call(
        kern,
        grid=(b, tq_total // tq),
        in_specs=[
            pl.BlockSpec((None, Q_WIDTH, tq), lambda bb, i: (bb, 0, i)),
            pl.BlockSpec((None, nk, KV_WIDTH), lambda bb, i: (bb, 0, 0)),
            pl.BlockSpec((None, nc, KV_WIDTH, CHUNK), lambda bb, i: (bb, 0, 0, 0)),
            pl.BlockSpec(memory_space=pltpu.SMEM),
        ],
        out_specs=pl.BlockSpec((None, Q_WIDTH, tq), lambda bb, i: (bb, 0, i)),
        out_shape=jax.ShapeDtypeStruct((b, Q_WIDTH, tq_total), BF16),
        scratch_shapes=[pltpu.VMEM((N_KV, KV_WIDTH, GROUP * tq), BF16),
                        pltpu.VMEM((2, N_KV, CHUNK, GROUP * tq), F32),
                        pltpu.VMEM((2, N_KV, 1, GROUP * tq), F32),
                        pltpu.VMEM((N_KV, 1, GROUP * tq), F32),
                        pltpu.VMEM((N_KV, 1, GROUP * tq), F32),
                        pltpu.VMEM((N_KV, HEAD_DIM, GROUP * tq), F32)],
        compiler_params=_params("parallel", "parallel"),
        name="attn_" + mode,
    )(q_t, k_all, v_all, sink)


_TN_DIMS = (((0,), (0,)), ((), ()))


def _attn_out_kernel(oa_ref, ob_ref, w_ref, x_ref, g1_ref, n2_ref, sc2_ref, sh2_ref, x1_ref, h2_ref):
    mix = lax.dot_general(oa_ref[...], w_ref[0:Q_WIDTH, :], _TN_DIMS, preferred_element_type=F32)
    mix = mix + lax.dot_general(ob_ref[...], w_ref[Q_WIDTH:2 * Q_WIDTH, :], _TN_DIMS,
                                preferred_element_type=F32)
    x1 = x_ref[...] + g1_ref[...] * mix
    x1_ref[...] = x1
    h2_ref[...] = _norm_mod(x1, n2_ref[...], sc2_ref[...], sh2_ref[...]).astype(BF16)


def _attn_out(oa, ob, w_bf16, x, g1, n2, sc2, sh2):
    b, t, d = x.shape
    tm = _pick(t, (512, 256))
    o_spec = pl.BlockSpec((None, Q_WIDTH, tm), lambda bb, i: (bb, 0, i))
    x_spec = pl.BlockSpec((None, tm, d), lambda bb, i: (bb, i, 0))
    return pl.pallas_call(
        _attn_out_kernel,
        grid=(b, t // tm),
        in_specs=[o_spec, o_spec,
                  pl.BlockSpec((2 * Q_WIDTH, d), lambda bb, i: (0, 0)),
                  x_spec, _vec_spec(g1),
                  pl.BlockSpec((1, d), lambda bb, i: (0, 0)),
                  _vec_spec(sc2), _vec_spec(sh2)],
        out_specs=[x_spec, x_spec],
        out_shape=[jax.ShapeDtypeStruct((b, t, d), F32), jax.ShapeDtypeStruct((b, t, d), BF16)],
        compiler_params=_params("parallel", "parallel"),
        name="attn_out",
    )(oa, ob, w_bf16, x, g1, n2, sc2, sh2)


def _ffn_kernel(*refs, gated, tm):
    if gated:
        h_ref, wg_ref, wu_ref, wd_ref, gate_ref, x_ref, g2_ref, o_ref, acc_ref = refs
    else:
        h_ref, wg_ref, wu_ref, wd_ref, x_ref, g2_ref, o_ref, acc_ref = refs
    e = pl.program_id(2)
    f = pl.program_id(3)
    first = jnp.logical_and(e == 0, f == 0)
    last = jnp.logical_and(e == pl.num_programs(2) - 1, f == pl.num_programs(3) - 1)

    @pl.when(first)
    def _():
        acc_ref[...] = jnp.zeros_like(acc_ref)

    h = h_ref[...]
    a = jnp.dot(h, wg_ref[...], preferred_element_type=F32)
    u = jnp.dot(h, wu_ref[...], preferred_element_type=F32)
    act = (a * _sigmoid(a) * u).astype(BF16)
    y = jnp.dot(act, wd_ref[...], preferred_element_type=F32)
    if gated:
        lane = lax.broadcasted_iota(jnp.int32, (tm, LANES), 1)
        ge = jnp.sum(jnp.where(lane == e, gate_ref[...], 0.0), axis=-1, keepdims=True)
        y = y * ge
    acc_ref[...] += y

    @pl.when(last)
    def _():
        o_ref[...] = x_ref[...] + g2_ref[...] * acc_ref[...]


def _ffn(h2, wg, wu, wd, gates, x1, g2):
    b, t, d = x1.shape
    n_e, _, f = wg.shape
    tm = _pick(t, (512, 256))
    tf = _pick(f, (1792, 1408, 1024, 512))
    gated = gates is not None
    x_spec = pl.BlockSpec((None, tm, d), lambda bb, i, e, j: (bb, i, 0))
    in_specs = [x_spec,
                pl.BlockSpec((None, d, tf), lambda bb, i, e, j: (e, 0, j)),
                pl.BlockSpec((None, d, tf), lambda bb, i, e, j: (e, 0, j)),
                pl.BlockSpec((None, tf, d), lambda bb, i, e, j: (e, j, 0))]
    args = [h2, wg, wu, wd]
    if gated:
        in_specs.append(pl.BlockSpec((None, tm, LANES), lambda bb, i, e, j: (bb, i, 0)))
        args.append(gates)
    in_specs += [x_spec, _vec_spec(g2)]
    args += [x1, g2]
    return pl.pallas_call(
        functools.partial(_ffn_kernel, gated=gated, tm=tm),
        grid=(b, t // tm, n_e, f // tf),
        in_specs=in_specs,
        out_specs=x_spec,
        out_shape=jax.ShapeDtypeStruct((b, t, d), F32),
        scratch_shapes=[pltpu.VMEM((tm, d), F32)],
        compiler_params=_params("parallel", "parallel", "arbitrary", "arbitrary"),
        name="ffn_gated" if gated else "ffn_dense",
    )(*args)


def _conv_in_kernel(x_ref, g_ref, sc_ref, sh_ref, w_ref, bg_ref, z_ref, *, d):
    h = _norm_mod(x_ref[...], g_ref[...], sc_ref[...], sh_ref[...]).astype(BF16)
    y = jnp.dot(h, w_ref[...], preferred_element_type=F32)
    bg_ref[...] = y[:, 0:d].astype(BF16)
    z_ref[...] = (y[:, d:2 * d] * y[:, 2 * d:3 * d]).astype(BF16)


def _conv_in(x, g, sc, sh, w_bf16):
    b, t, d = x.shape
    tm = _pick(t, (512, 256))
    x_spec = pl.BlockSpec((None, tm, d), lambda bb, i: (bb, i, 0))
    return pl.pallas_call(
        functools.partial(_conv_in_kernel, d=d),
        grid=(b, t // tm),
        in_specs=[x_spec, pl.BlockSpec((1, d), lambda bb, i: (0, 0)), _vec_spec(sc), _vec_spec(sh),
                  pl.BlockSpec((d, 3 * d), lambda bb, i: (0, 0))],
        out_specs=[x_spec, x_spec],
        out_shape=[jax.ShapeDtypeStruct((b, t, d), BF16), jax.ShapeDtypeStruct((b, t, d), BF16)],
        compiler_params=_params("parallel", "parallel"),
        name="conv_in",
    )(x, g, sc, sh, w_bf16)


HALO = 16


def _conv_out_kernel(z_ref, zp_ref, zn_ref, bg_ref, cw_ref, w_ref, x_ref, g1_ref, n2_ref, sc2_ref,
                     sh2_ref, rw_ref, x1_ref, h2_ref, gate_ref, *, tm, n_experts):
    i = pl.program_id(1)
    z = z_ref[...].astype(F32)
    prev = jnp.where(i > 0, zp_ref[HALO - 1:HALO, :].astype(F32), 0.0)
    nxt = jnp.where(i < pl.num_programs(1) - 1, zn_ref[0:1, :].astype(F32), 0.0)
    row = lax.broadcasted_iota(jnp.int32, z.shape, 0)
    z_dn = jnp.where(row == 0, prev, pltpu.roll(z, 1, 0))
    z_up = jnp.where(row == tm - 1, nxt, pltpu.roll(z, tm - 1, 0))
    conv = z_dn * cw_ref[0:1, :] + z * cw_ref[1:2, :] + z_up * cw_ref[2:3, :]
    v = (bg_ref[...].astype(F32) * conv).astype(BF16)
    mix = jnp.dot(v, w_ref[...], preferred_element_type=F32)
    x1 = x_ref[...] + g1_ref[...] * mix
    x1_ref[...] = x1
    h2 = _norm_mod(x1, n2_ref[...], sc2_ref[...], sh2_ref[...])
    h2_ref[...] = h2.astype(BF16)

    logits = jnp.dot(h2, rw_ref[...], preferred_element_type=F32, precision=lax.Precision.HIGHEST)
    lane = lax.broadcasted_iota(jnp.int32, logits.shape, 1)
    lanef = lane.astype(F32)
    logits = jnp.where(lane < n_experts, logits, -jnp.inf)
    m1 = jnp.max(logits, axis=-1, keepdims=True)
    i1 = jnp.min(jnp.where(logits == m1, lanef, float(LANES)), axis=-1, keepdims=True)
    rest = jnp.where(lanef == i1, -jnp.inf, logits)
    m2 = jnp.max(rest, axis=-1, keepdims=True)
    i2 = jnp.min(jnp.where(rest == m2, lanef, float(LANES)), axis=-1, keepdims=True)
    e2 = jnp.exp(m2 - m1)
    w1 = 1.0 / (1.0 + e2)
    w2 = e2 * w1
    gate_ref[...] = jnp.where(lanef == i1, w1, 0.0) + jnp.where(lanef == i2, w2, 0.0)


def _conv_out(z, bg, conv_w, w_bf16, x, g1, n2, sc2, sh2, router_pad, n_experts):
    b, t, d = x.shape
    tm = _pick(t, (512, 256))
    hb = tm // HALO
    n_halo = t // HALO
    x_spec = pl.BlockSpec((None, tm, d), lambda bb, i: (bb, i, 0))
    prev_spec = pl.BlockSpec((None, HALO, d), lambda bb, i: (bb, jnp.maximum(i * hb - 1, 0), 0))
    next_spec = pl.BlockSpec((None, HALO, d), lambda bb, i: (bb, jnp.minimum((i + 1) * hb, n_halo - 1), 0))
    full = lambda r, c: pl.BlockSpec((r, c), lambda bb, i: (0, 0))
    return pl.pallas_call(
        functools.partial(_conv_out_kernel, tm=tm, n_experts=n_experts),
        grid=(b, t // tm),
        in_specs=[x_spec, prev_spec, next_spec, x_spec, full(3, d), full(d, d), x_spec, _vec_spec(g1),
                  full(1, d), _vec_spec(sc2), _vec_spec(sh2), full(d, LANES)],
        out_specs=[x_spec, x_spec, pl.BlockSpec((None, tm, LANES), lambda bb, i: (bb, i, 0))],
        out_shape=[jax.ShapeDtypeStruct((b, t, d), F32), jax.ShapeDtypeStruct((b, t, d), BF16),
                   jax.ShapeDtypeStruct((b, t, LANES), F32)],
        compiler_params=_params("parallel", "parallel"),
        name="conv_out",
    )(z, z, z, bg, conv_w, w_bf16, x, g1, n2, sc2, sh2, router_pad)


def _rope_tables(n_tokens):
    rows = n_tokens // GRID_W
    row, col = jnp.meshgrid(jnp.arange(rows, dtype=F32), jnp.arange(GRID_W, dtype=F32), indexing="ij")
    half = HEAD_DIM // 2
    inv_freq = ROPE_THETA ** (-jnp.arange(0, half, 2, dtype=F32) / half)
    ang = jnp.concatenate([row.reshape(-1, 1) * inv_freq, col.reshape(-1, 1) * inv_freq], axis=-1)
    cos = jnp.repeat(jnp.cos(ang), 2, axis=-1)
    sign = jnp.tile(jnp.array([-1.0, 1.0], F32), HEAD_DIM // 2)
    sin = jnp.repeat(jnp.sin(ang), 2, axis=-1) * sign
    reps = LANES // HEAD_DIM
    return jnp.tile(cos, (1, reps)), jnp.tile(sin, (1, reps))


def kernel(x, c, ctx, c_ctx, ada_w, ada_b, norm1_g, norm2_g, attn_w_in, attn_w_out, qnorm_a, knorm_a, qnorm_b, knorm_b, sink_b, ffn_w_gate, ffn_w_up, ffn_w_down, conv_w_in, conv_w, conv_w_out, router_w, moe_w_gate, moe_w_up, moe_w_down):
    bsz, n_tok, d = x.shape
    n_ctx = ctx.shape[1]
    depth = ada_w.shape[0]
    n_experts = router_w.shape[-1]
    assert n_ctx == CHUNK and n_tok % CHUNK == 0 and bsz + 1 <= 8
    assert attn_w_in.shape[-1] == IN_PROJ_WIDTH and d % LANES == 0 and n_experts <= LANES

    cc = jnp.zeros((8, d), F32).at[:bsz].set(c).at[bsz].set(c_ctx)
    mod = _modulation(cc, ada_w, ada_b)

    def mod_vec(layer, j, is_ctx):
        m = mod[layer, :, j * d:(j + 1) * d]
        return m[bsz:bsz + 1].reshape(1, 1, d) if is_ctx else m[:bsz].reshape(bsz, 1, d)

    cos_l, sin_l = _rope_tables(n_tok)
    cos_c = jnp.ones((n_ctx, LANES), F32)
    sin_c = jnp.zeros((n_ctx, LANES), F32)
    head_id = jnp.arange(LANES) // HEAD_DIM
    bd = (head_id[:, None] == head_id[None, :]).astype(BF16)
    no_sink = jnp.zeros((N_HEADS,), F32)
    tile2 = lambda v: jnp.tile(v, LANES // HEAD_DIM)

    xc = ctx
    for layer in range(depth):
        i = layer // 2
        ctx_needed = any(j % 2 == 0 for j in range(layer + 1, depth))
        n1 = norm1_g[layer].reshape(1, d)
        n2 = norm2_g[layer].reshape(1, d)
        mv = lambda j, is_ctx: mod_vec(layer, j, is_ctx)

        if layer % 2 == 0:
            w_in = attn_w_in[i].astype(BF16)
            w_out = attn_w_out[i].astype(BF16)
            qscale = ATTN_SCALE * LOG2E
            gains = jnp.stack([tile2(qnorm_a[i]) * qscale, tile2(knorm_a[i]),
                               tile2(qnorm_b[i]) * qscale, tile2(knorm_b[i])]).astype(F32)
            sink = sink_b[i].astype(F32)
            qa_l, ka_l, va_l, qb_l, kb_l, vb_l = _attn_in(
                x, n1, mv(1, False), mv(0, False), w_in, gains, cos_l, sin_l, bd, rope=True)
            qa_c, ka_c, va_c, qb_c, kb_c, vb_c = _attn_in(
                xc, n1, mv(1, True), mv(0, True), w_in, gains, cos_c, sin_c, bd, rope=False)
            ka_all = jnp.concatenate([ka_c, ka_l], axis=1)
            va_all = jnp.concatenate([va_c, va_l], axis=1)
            kb_all = jnp.concatenate([kb_c, kb_l], axis=1)
            vb_all = jnp.concatenate([vb_c, vb_l], axis=1)
            oa = _attention(qa_l, ka_all, va_all, no_sink, mode="global", use_sink=False)
            ob = _attention(qb_l, kb_all, vb_all, sink, mode="window", use_sink=True)
            x, h2 = _attn_out(oa, ob, w_out, x, mv(2, False), n2, mv(4, False), mv(3, False))
            if ctx_needed:
                oa_c = _attention(qa_c, ka_c, va_c, no_sink, mode="ctx", use_sink=False)
                ob_c = _attention(qb_c, kb_c, vb_c, sink, mode="ctx", use_sink=True)
                xc, h2c = _attn_out(oa_c, ob_c, w_out, xc, mv(2, True), n2, mv(4, True), mv(3, True))
            wg = ffn_w_gate[i].astype(BF16)[None]
            wu = ffn_w_up[i].astype(BF16)[None]
            wd = ffn_w_down[i].astype(BF16)[None]
            x = _ffn(h2, wg, wu, wd, None, x, mv(5, False))
            if ctx_needed:
                xc = _ffn(h2c, wg, wu, wd, None, xc, mv(5, True))
        else:
            w_in = conv_w_in[i].astype(BF16)
            w_out = conv_w_out[i].astype(BF16)
            cw = conv_w[i].astype(F32)
            router_pad = jnp.zeros((d, LANES), F32).at[:, :n_experts].set(router_w[i])
            wg = moe_w_gate[i].astype(BF16)
            wu = moe_w_up[i].astype(BF16)
            wd = moe_w_down[i].astype(BF16)
            bg, z = _conv_in(x, n1, mv(1, False), mv(0, False), w_in)
            x, h2, gates = _conv_out(z, bg, cw, w_out, x, mv(2, False), n2, mv(4, False), mv(3, False),
                                     router_pad, n_experts)
            x = _ffn(h2, wg, wu, wd, gates, x, mv(5, False))
            if ctx_needed:
                bg, z = _conv_in(xc, n1, mv(1, True), mv(0, True), w_in)
                xc, h2c, gates_c = _conv_out(z, bg, cw, w_out, xc, mv(2, True), n2, mv(4, True),
                                             mv(3, True), router_pad, n_experts)
                xc = _ffn(h2c, wg, wu, wd, gates_c, xc, mv(5, True))
    return x
```

```python
import functools

import jax
import jax.numpy as jnp
from jax import lax
from jax.experimental import pallas as pl
from jax.experimental.pallas import tpu as pltpu

F32 = jnp.float32
BF16 = jnp.bfloat16

HEAD_DIM = 64
N_KV = 2
GROUP = 4
N_HEADS = N_KV * GROUP
Q_WIDTH = N_HEADS * HEAD_DIM
KV_WIDTH = N_KV * HEAD_DIM
IN_PROJ_WIDTH = 2 * (Q_WIDTH + 2 * KV_WIDTH)
GRID_W = 64
WINDOW = 128
ROPE_THETA = 10000.0
ATTN_SCALE = HEAD_DIM ** -0.5
EPS = 1e-6
N_ADA = 6
LOG2E = 1.4426950408889634
LANES = 128
CHUNK = 256
NEG_BIG = -1e30
VMEM_LIMIT_BYTES = 56 * 1024 * 1024


def _params(*sem):
    return pltpu.CompilerParams(dimension_semantics=sem, vmem_limit_bytes=VMEM_LIMIT_BYTES)


def _pick(n, candidates):
    for t in candidates:
        if n % t == 0:
            return t
    return n


def _sigmoid(a):
    return 1.0 / (1.0 + jnp.exp(-a))


def _norm_mod(x, g, sc, sh):
    ms = jnp.mean(x * x, axis=-1, keepdims=True)
    return x * lax.rsqrt(ms + EPS) * g * (1.0 + sc) + sh


def _mod_kernel(c_ref, w_ref, b_ref, o_ref):
    c = c_ref[...]
    s = c * _sigmoid(c)
    o_ref[...] = jnp.dot(s, w_ref[...], preferred_element_type=F32,
                         precision=lax.Precision.HIGHEST) + b_ref[...]


def _modulation(cc, ada_w, ada_b):
    depth, d, n = ada_w.shape
    tn = _pick(n, (1536, 1024, 512))
    rows = cc.shape[0]
    return pl.pallas_call(
        _mod_kernel,
        grid=(depth, n // tn),
        in_specs=[
            pl.BlockSpec((rows, d), lambda l, j: (0, 0)),
            pl.BlockSpec((None, d, tn), lambda l, j: (l, 0, j)),
            pl.BlockSpec((None, 1, tn), lambda l, j: (l, 0, j)),
        ],
        out_specs=pl.BlockSpec((None, rows, tn), lambda l, j: (l, 0, j)),
        out_shape=jax.ShapeDtypeStruct((depth, rows, n), F32),
        compiler_params=_params("arbitrary", "arbitrary"),
        name="modulation",
    )(cc, ada_w, ada_b.reshape(depth, 1, n))


def _vec_spec(arr):
    d = arr.shape[-1]
    if arr.shape[0] == 1:
        return pl.BlockSpec((None, 1, d), lambda b, i, *_: (0, 0, 0))
    return pl.BlockSpec((None, 1, d), lambda b, i, *_: (b, 0, 0))


def _attn_in_kernel(x_ref, g_ref, sc_ref, sh_ref, w_ref, gains_ref, cos_ref, sin_ref, bd_ref,
                    qa_ref, ka_ref, va_ref, qb_ref, kb_ref, vb_ref, *, rope, tm):
    h = _norm_mod(x_ref[...], g_ref[...], sc_ref[...], sh_ref[...]).astype(BF16)
    y = jnp.dot(h, w_ref[...], preferred_element_type=F32)
    lane = lax.broadcasted_iota(jnp.int32, (tm, LANES), 1)
    even = (lane & 1) == 0
    bd = bd_ref[...]

    def head_norm(yc, gain):
        ss = jnp.dot((yc * yc).astype(BF16), bd, preferred_element_type=F32)
        t = yc * lax.rsqrt(ss * (1.0 / HEAD_DIM) + EPS) * gain
        if rope:
            partner = jnp.where(even, pltpu.roll(t, LANES - 1, 1), pltpu.roll(t, 1, 1))
            t = t * cos_ref[...] + partner * sin_ref[...]
        return t

    groups = ((0, qa_ref, ka_ref, va_ref, 0), (Q_WIDTH + 2 * KV_WIDTH, qb_ref, kb_ref, vb_ref, 2))
    for col0, q_ref, k_ref, v_ref, grow in groups:
        gq = gains_ref[grow:grow + 1, :]
        gk = gains_ref[grow + 1:grow + 2, :]
        for c in range(Q_WIDTH // LANES):
            t = head_norm(y[:, col0 + c * LANES:col0 + (c + 1) * LANES], gq)
            q_ref[c * LANES:(c + 1) * LANES, :] = t.T.astype(BF16)
        kcol = col0 + Q_WIDTH
        k_ref[...] = head_norm(y[:, kcol:kcol + KV_WIDTH], gk).astype(BF16)
        vt = y[:, kcol + KV_WIDTH:kcol + 2 * KV_WIDTH].T
        for j in range(tm // CHUNK):
            v_ref[j] = vt[:, j * CHUNK:(j + 1) * CHUNK].astype(BF16)


def _attn_in(x, g, sc, sh, w_bf16, gains, cos_t, sin_t, bd, *, rope):
    b, t, d = x.shape
    tm = _pick(t, (512, 256))
    nch = t // CHUNK
    kern = functools.partial(_attn_in_kernel, rope=rope, tm=tm)
    q_spec = pl.BlockSpec((None, Q_WIDTH, tm), lambda bb, i: (bb, 0, i))
    k_spec = pl.BlockSpec((None, tm, KV_WIDTH), lambda bb, i: (bb, i, 0))
    v_spec = pl.BlockSpec((None, tm // CHUNK, KV_WIDTH, CHUNK), lambda bb, i: (bb, i, 0, 0))
    q_shape = jax.ShapeDtypeStruct((b, Q_WIDTH, t), BF16)
    k_shape = jax.ShapeDtypeStruct((b, t, KV_WIDTH), BF16)
    v_shape = jax.ShapeDtypeStruct((b, nch, KV_WIDTH, CHUNK), BF16)
    return pl.pallas_call(
        kern,
        grid=(b, t // tm),
        in_specs=[
            pl.BlockSpec((None, tm, d), lambda bb, i: (bb, i, 0)),
            pl.BlockSpec((1, d), lambda bb, i: (0, 0)),
            _vec_spec(sc), _vec_spec(sh),
            pl.BlockSpec((d, IN_PROJ_WIDTH), lambda bb, i: (0, 0)),
            pl.BlockSpec((4, LANES), lambda bb, i: (0, 0)),
            pl.BlockSpec((tm, LANES), lambda bb, i: (i, 0)),
            pl.BlockSpec((tm, LANES), lambda bb, i: (i, 0)),
            pl.BlockSpec((LANES, LANES), lambda bb, i: (0, 0)),
        ],
        out_specs=[q_spec, k_spec, v_spec, q_spec, k_spec, v_spec],
        out_shape=[q_shape, k_shape, v_shape, q_shape, k_shape, v_shape],
        compiler_params=_params("parallel", "parallel"),
        name="attn_in",
    )(x, g, sc, sh, w_bf16, gains, cos_t, sin_t, bd)


def _attn_kernel(q_ref, k_ref, v_ref, sink_ref, o_ref, qpad_ref, s_ref, cmax_ref, m_ref, l_ref,
                 acc_ref, *, mode, use_sink, n_lat, tq):
    i = pl.program_id(1)
    w = GROUP * tq
    zeros = jnp.zeros((HEAD_DIM, w), BF16)
    for kv in range(N_KV):
        heads = range(kv * GROUP, (kv + 1) * GROUP)
        q4 = jnp.concatenate([q_ref[h * HEAD_DIM:(h + 1) * HEAD_DIM, :] for h in heads], axis=1)
        qpad_ref[kv] = jnp.concatenate([q4, zeros] if kv == 0 else [zeros, q4], axis=0)
        if use_sink:
            m_ref[kv] = jnp.concatenate(
                [jnp.full((1, tq), sink_ref[h] * LOG2E, F32) for h in heads], axis=1)
            l_ref[kv] = jnp.ones((1, w), F32)
        else:
            m_ref[kv] = jnp.full((1, w), NEG_BIG, F32)
            l_ref[kv] = jnp.zeros((1, w), F32)
        acc_ref[kv] = jnp.zeros((HEAD_DIM, w), F32)

    def scores(c, slot, mask):
        start = c * CHUNK if isinstance(c, int) else pl.multiple_of(c * CHUNK, CHUNK)
        kc = k_ref[pl.ds(start, CHUNK), :]
        for kv in range(N_KV):
            s = jnp.dot(kc, qpad_ref[kv], preferred_element_type=F32)
            if mask is not None:
                s = jnp.where(mask, s, -jnp.inf)
            s_ref[slot, kv] = s
            cmax_ref[slot, kv] = jnp.max(s, axis=0, keepdims=True)

    def absorb(c, slot):
        for kv in range(N_KV):
            m = m_ref[kv]
            m_new = jnp.maximum(m, cmax_ref[slot, kv])
            alpha = jnp.exp2(m - m_new)
            p = jnp.exp2(s_ref[slot, kv] - m_new)
            l_ref[kv] = alpha * l_ref[kv] + jnp.sum(p, axis=0, keepdims=True)
            m_ref[kv] = m_new
            vc = v_ref[c, kv * HEAD_DIM:(kv + 1) * HEAD_DIM, :]
            acc_ref[kv] = alpha * acc_ref[kv] + jnp.dot(vc, p.astype(BF16),
                                                        preferred_element_type=F32)

    scores(0, 0, None)
    if mode == "global":
        def body(j, carry):
            c = 2 * j
            scores(c + 1, 1, None)
            absorb(c, 0)
            scores(c + 2, 0, None)
            absorb(c + 1, 1)
            return carry
        lax.fori_loop(0, n_lat // 2, body, 0)
        c_end = 2 * (n_lat // 2)
        if n_lat % 2:
            scores(c_end + 1, 1, None)
            absorb(c_end, 0)
            absorb(c_end + 1, 1)
        else:
            absorb(c_end, 0)
    elif mode == "window":
        row = lax.broadcasted_iota(jnp.int32, (CHUNK, w), 0)
        col = lax.broadcasted_iota(jnp.int32, (CHUNK, w), 1) & (tq - 1)
        rel0 = col - row
        prev_c, prev_slot = 0, 0
        for d in (-1, 0, 1):
            cl = i + d
            ok = jnp.logical_and(cl >= 0, cl < n_lat)
            c = 1 + jnp.clip(cl, 0, n_lat - 1)
            mask = jnp.logical_and(jnp.abs(rel0 - d * CHUNK) <= WINDOW, ok)
            scores(c, 1 - prev_slot, mask)
            absorb(prev_c, prev_slot)
            prev_c, prev_slot = c, 1 - prev_slot
        absorb(prev_c, prev_slot)
    else:
        absorb(0, 0)

    for kv in range(N_KV):
        out = acc_ref[kv] * (1.0 / l_ref[kv])
        for g in range(GROUP):
            h = kv * GROUP + g
            o_ref[h * HEAD_DIM:(h + 1) * HEAD_DIM, :] = out[:, g * tq:(g + 1) * tq].astype(BF16)


def _attention(q_t, k_all, v_all, sink, *, mode, use_sink):
    b, _, tq_total = q_t.shape
    nk = k_all.shape[1]
    nc = v_all.shape[1]
    tq = CHUNK
    kern = functools.partial(_attn_kernel, mode=mode, use_sink=use_sink, n_lat=nc - 1, tq=tq)
    return pl.pallas_call(
        kern,
        grid=(b, tq_total // tq),
        in_specs=[
            pl.BlockSpec((None, Q_WIDTH, tq), lambda bb, i: (bb, 0, i)),
            pl.BlockSpec((None, nk, KV_WIDTH), lambda bb, i: (bb, 0, 0)),
            pl.BlockSpec((None, nc, KV_WIDTH, CHUNK), lambda bb, i: (bb, 0, 0, 0)),
            pl.BlockSpec(memory_space=pltpu.SMEM),
        ],
        out_specs=pl.BlockSpec((None, Q_WIDTH, tq), lambda bb, i: (bb, 0, i)),
        out_shape=jax.ShapeDtypeStruct((b, Q_WIDTH, tq_total), BF16),
        scratch_shapes=[pltpu.VMEM((N_KV, KV_WIDTH, GROUP * tq), BF16),
                        pltpu.VMEM((2, N_KV, CHUNK, GROUP * tq), F32),
                        pltpu.VMEM((2, N_KV, 1, GROUP * tq), F32),
                        pltpu.VMEM((N_KV, 1, GROUP * tq), F32),
                        pltpu.VMEM((N_KV, 1, GROUP * tq), F32),
                        pltpu.VMEM((N_KV, HEAD_DIM, GROUP * tq), F32)],
        compiler_params=_params("parallel", "parallel"),
        name="attn_" + mode,
    )(q_t, k_all, v_all, sink)


_TN_DIMS = (((0,), (0,)), ((), ()))


def _attn_out_kernel(oa_ref, ob_ref, w_ref, x_ref, g1_ref, n2_ref, sc2_ref, sh2_ref, x1_ref, h2_ref):
    mix = lax.dot_general(oa_ref[...], w_ref[0:Q_WIDTH, :], _TN_DIMS, preferred_element_type=F32)
    mix = mix + lax.dot_general(ob_ref[...], w_ref[Q_WIDTH:2 * Q_WIDTH, :], _TN_DIMS,
                                preferred_element_type=F32)
    x1 = x_ref[...] + g1_ref[...] * mix
    x1_ref[...] = x1
    h2_ref[...] = _norm_mod(x1, n2_ref[...], sc2_ref[...], sh2_ref[...]).astype(BF16)


def _attn_out(oa, ob, w_bf16, x, g1, n2, sc2, sh2):
    b, t, d = x.shape
    tm = _pick(t, (512, 256))
    o_spec = pl.BlockSpec((None, Q_WIDTH, tm), lambda bb, i: (bb, 0, i))
    x_spec = pl.BlockSpec((None, tm, d), lambda bb, i: (bb, i, 0))
    return pl.pallas_call(
        _attn_out_kernel,
        grid=(b, t // tm),
        in_specs=[o_spec, o_spec,
                  pl.BlockSpec((2 * Q_WIDTH, d), lambda bb, i: (0, 0)),
                  x_spec, _vec_spec(g1),
                  pl.BlockSpec((1, d), lambda bb, i: (0, 0)),
                  _vec_spec(sc2), _vec_spec(sh2)],
        out_specs=[x_spec, x_spec],
        out_shape=[jax.ShapeDtypeStruct((b, t, d), F32), jax.ShapeDtypeStruct((b, t, d), BF16)],
        compiler_params=_params("parallel", "parallel"),
        name="attn_out",
    )(oa, ob, w_bf16, x, g1, n2, sc2, sh2)


def _ffn_kernel(*refs, gated, tm):
    if gated:
        h_ref, wg_ref, wu_ref, wd_ref, gate_ref, x_ref, g2_ref, o_ref, acc_ref = refs
    else:
        h_ref, wg_ref, wu_ref, wd_ref, x_ref, g2_ref, o_ref, acc_ref = refs
    e = pl.program_id(2)
    f = pl.program_id(3)
    first = jnp.logical_and(e == 0, f == 0)
    last = jnp.logical_and(e == pl.num_programs(2) - 1, f == pl.num_programs(3) - 1)

    @pl.when(first)
    def _():
        acc_ref[...] = jnp.zeros_like(acc_ref)

    h = h_ref[...]
    a = jnp.dot(h, wg_ref[...], preferred_element_type=F32)
    u = jnp.dot(h, wu_ref[...], preferred_element_type=F32)
    act = (a * _sigmoid(a) * u).astype(BF16)
    y = jnp.dot(act, wd_ref[...], preferred_element_type=F32)
    if gated:
        lane = lax.broadcasted_iota(jnp.int32, (tm, LANES), 1)
        ge = jnp.sum(jnp.where(lane == e, gate_ref[...], 0.0), axis=-1, keepdims=True)
        y = y * ge
    acc_ref[...] += y

    @pl.when(last)
    def _():
        o_ref[...] = x_ref[...] + g2_ref[...] * acc_ref[...]


def _ffn(h2, wg, wu, wd, gates, x1, g2):
    b, t, d = x1.shape
    n_e, _, f = wg.shape
    tm = _pick(t, (512, 256))
    tf = _pick(f, (1792, 1408, 1024, 512))
    gated = gates is not None
    x_spec = pl.BlockSpec((None, tm, d), lambda bb, i, e, j: (bb, i, 0))
    in_specs = [x_spec,
                pl.BlockSpec((None, d, tf), lambda bb, i, e, j: (e, 0, j)),
                pl.BlockSpec((None, d, tf), lambda bb, i, e, j: (e, 0, j)),
                pl.BlockSpec((None, tf, d), lambda bb, i, e, j: (e, j, 0))]
    args = [h2, wg, wu, wd]
    if gated:
        in_specs.append(pl.BlockSpec((None, tm, LANES), lambda bb, i, e, j: (bb, i, 0)))
        args.append(gates)
    in_specs += [x_spec, _vec_spec(g2)]
    args += [x1, g2]
    return pl.pallas_call(
        functools.partial(_ffn_kernel, gated=gated, tm=tm),
        grid=(b, t // tm, n_e, f // tf),
        in_specs=in_specs,
        out_specs=x_spec,
        out_shape=jax.ShapeDtypeStruct((b, t, d), F32),
        scratch_shapes=[pltpu.VMEM((tm, d), F32)],
        compiler_params=_params("parallel", "parallel", "arbitrary", "arbitrary"),
        name="ffn_gated" if gated else "ffn_dense",
    )(*args)


def _conv_in_kernel(x_ref, g_ref, sc_ref, sh_ref, w_ref, bg_ref, z_ref, *, d):
    h = _norm_mod(x_ref[...], g_ref[...], sc_ref[...], sh_ref[...]).astype(BF16)
    y = jnp.dot(h, w_ref[...], preferred_element_type=F32)
    bg_ref[...] = y[:, 0:d].astype(BF16)
    z_ref[...] = (y[:, d:2 * d] * y[:, 2 * d:3 * d]).astype(BF16)


def _conv_in(x, g, sc, sh, w_bf16):
    b, t, d = x.shape
    tm = _pick(t, (512, 256))
    x_spec = pl.BlockSpec((None, tm, d), lambda bb, i: (bb, i, 0))
    return pl.pallas_call(
        functools.partial(_conv_in_kernel, d=d),
        grid=(b, t // tm),
        in_specs=[x_spec, pl.BlockSpec((1, d), lambda bb, i: (0, 0)), _vec_spec(sc), _vec_spec(sh),
                  pl.BlockSpec((d, 3 * d), lambda bb, i: (0, 0))],
        out_specs=[x_spec, x_spec],
        out_shape=[jax.ShapeDtypeStruct((b, t, d), BF16), jax.ShapeDtypeStruct((b, t, d), BF16)],
        compiler_params=_params("parallel", "parallel"),
        name="conv_in",
    )(x, g, sc, sh, w_bf16)


HALO = 16


def _conv_out_kernel(z_ref, zp_ref, zn_ref, bg_ref, cw_ref, w_ref, x_ref, g1_ref, n2_ref, sc2_ref,
                     sh2_ref, rw_ref, x1_ref, h2_ref, gate_ref, *, tm, n_experts):
    i = pl.program_id(1)
    z = z_ref[...].astype(F32)
    prev = jnp.where(i > 0, zp_ref[HALO - 1:HALO, :].astype(F32), 0.0)
    nxt = jnp.where(i < pl.num_programs(1) - 1, zn_ref[0:1, :].astype(F32), 0.0)
    row = lax.broadcasted_iota(jnp.int32, z.shape, 0)
    z_dn = jnp.where(row == 0, prev, pltpu.roll(z, 1, 0))
    z_up = jnp.where(row == tm - 1, nxt, pltpu.roll(z, tm - 1, 0))
    conv = z_dn * cw_ref[0:1, :] + z * cw_ref[1:2, :] + z_up * cw_ref[2:3, :]
    v = (bg_ref[...].astype(F32) * conv).astype(BF16)
    mix = jnp.dot(v, w_ref[...], preferred_element_type=F32)
    x1 = x_ref[...] + g1_ref[...] * mix
    x1_ref[...] = x1
    h2 = _norm_mod(x1, n2_ref[...], sc2_ref[...], sh2_ref[...])
    h2_ref[...] = h2.astype(h2_ref.dtype)

    logits = jnp.dot(h2, rw_ref[...], preferred_element_type=F32, precision=lax.Precision.HIGHEST)
    lane = lax.broadcasted_iota(jnp.int32, logits.shape, 1)
    lanef = lane.astype(F32)
    logits = jnp.where(lane < n_experts, logits, -jnp.inf)
    m1 = jnp.max(logits, axis=-1, keepdims=True)
    i1 = jnp.min(jnp.where(logits == m1, lanef, float(LANES)), axis=-1, keepdims=True)
    rest = jnp.where(lanef == i1, -jnp.inf, logits)
    m2 = jnp.max(rest, axis=-1, keepdims=True)
    i2 = jnp.min(jnp.where(rest == m2, lanef, float(LANES)), axis=-1, keepdims=True)
    e2 = jnp.exp(m2 - m1)
    w1 = 1.0 / (1.0 + e2)
    w2 = e2 * w1
    rec = jnp.where(lanef == i1, w1, 0.0) + jnp.where(lanef == i2, w2, 0.0)
    for k, val in enumerate((i1, i2, w1, w2)):
        rec = jnp.where(lane == n_experts + k, val, rec)
    gate_ref[...] = rec


def _conv_out(z, bg, conv_w, w_bf16, x, g1, n2, sc2, sh2, router_pad, n_experts, h2_dtype):
    b, t, d = x.shape
    tm = _pick(t, (512, 256))
    hb = tm // HALO
    n_halo = t // HALO
    x_spec = pl.BlockSpec((None, tm, d), lambda bb, i: (bb, i, 0))
    prev_spec = pl.BlockSpec((None, HALO, d), lambda bb, i: (bb, jnp.maximum(i * hb - 1, 0), 0))
    next_spec = pl.BlockSpec((None, HALO, d), lambda bb, i: (bb, jnp.minimum((i + 1) * hb, n_halo - 1), 0))
    full = lambda r, c: pl.BlockSpec((r, c), lambda bb, i: (0, 0))
    return pl.pallas_call(
        functools.partial(_conv_out_kernel, tm=tm, n_experts=n_experts),
        grid=(b, t // tm),
        in_specs=[x_spec, prev_spec, next_spec, x_spec, full(3, d), full(d, d), x_spec, _vec_spec(g1),
                  full(1, d), _vec_spec(sc2), _vec_spec(sh2), full(d, LANES)],
        out_specs=[x_spec, x_spec, pl.BlockSpec((None, tm, LANES), lambda bb, i: (bb, i, 0))],
        out_shape=[jax.ShapeDtypeStruct((b, t, d), F32), jax.ShapeDtypeStruct((b, t, d), h2_dtype),
                   jax.ShapeDtypeStruct((b, t, LANES), F32)],
        compiler_params=_params("parallel", "parallel"),
        name="conv_out",
    )(z, z, z, bg, conv_w, w_bf16, x, g1, n2, sc2, sh2, router_pad)


EXPERT_TILE = 1024


def _rank_kernel(route_ref, rank_ref, count_ref, base_ref, *, tm, n_experts):
    @pl.when(pl.program_id(0) == 0)
    def _():
        base_ref[...] = jnp.zeros_like(base_ref)

    route = route_ref[...]
    lane = lax.broadcasted_iota(jnp.int32, (tm, LANES), 1)
    lanef = lane.astype(F32)
    i1 = jnp.sum(jnp.where(lane == n_experts, route, 0.0), axis=-1, keepdims=True)
    i2 = jnp.sum(jnp.where(lane == n_experts + 1, route, 0.0), axis=-1, keepdims=True)
    onehot = jnp.where(lanef == i1, 1.0, 0.0) + jnp.where(lanef == i2, 1.0, 0.0)
    r = lax.broadcasted_iota(jnp.int32, (tm, tm), 0)
    c = lax.broadcasted_iota(jnp.int32, (tm, tm), 1)
    lower = jnp.where(c < r, 1.0, 0.0).astype(BF16)
    before = jnp.dot(lower, onehot.astype(BF16), preferred_element_type=F32) + base_ref[...]
    r1 = jnp.sum(jnp.where(lanef == i1, before, 0.0), axis=-1, keepdims=True)
    r2 = jnp.sum(jnp.where(lanef == i2, before, 0.0), axis=-1, keepdims=True)
    rank_ref[...] = jnp.where(lane == 0, r1, jnp.where(lane == 1, r2, 0.0)).astype(jnp.int32)
    base_ref[...] += jnp.sum(onehot, axis=0, keepdims=True)
    count_ref[...] = base_ref[...]


def _rank(route, n_experts):
    n = route.shape[0]
    tm = _pick(n, (512, 256))
    return pl.pallas_call(
        functools.partial(_rank_kernel, tm=tm, n_experts=n_experts),
        grid=(n // tm,),
        in_specs=[pl.BlockSpec((tm, LANES), lambda i: (i, 0))],
        out_specs=[pl.BlockSpec((tm, LANES), lambda i: (i, 0)), pl.BlockSpec((1, LANES), lambda i: (0, 0))],
        out_shape=[jax.ShapeDtypeStruct((n, LANES), jnp.int32), jax.ShapeDtypeStruct((1, LANES), F32)],
        scratch_shapes=[pltpu.VMEM((1, LANES), F32)],
        compiler_params=_params("arbitrary"),
        name="moe_rank",
    )(route)


def _dispatch_kernel(dest_ref, h_ref, xs_in_ref, xs_ref, sem, *, tm):
    del xs_in_ref

    def row_copy(r, k):
        d = dest_ref[0, k * tm + r]
        return pltpu.make_async_copy(h_ref.at[pl.ds(r, 1), :], xs_ref.at[pl.ds(d, 1), :], sem)

    def issue(r, carry):
        row_copy(r, 0).start()
        row_copy(r, 1).start()
        return carry

    lax.fori_loop(0, tm, issue, 0, unroll=8)

    def drain(r, carry):
        row_copy(r, 0).wait()
        row_copy(r, 1).wait()
        return carry

    lax.fori_loop(0, tm, drain, 0, unroll=8)


def _dispatch(dest, h2, n_rows):
    n, d = h2.shape
    tm = dest.shape[-1] // 2
    zeros = jnp.zeros((n_rows, d), h2.dtype)
    return pl.pallas_call(
        functools.partial(_dispatch_kernel, tm=tm),
        grid=(n // tm,),
        in_specs=[pl.BlockSpec((None, 1, 2 * tm), lambda i: (i, 0, 0), memory_space=pltpu.SMEM),
                  pl.BlockSpec((tm, d), lambda i: (i, 0)),
                  pl.BlockSpec(memory_space=pl.ANY)],
        out_specs=pl.BlockSpec(memory_space=pl.ANY),
        out_shape=jax.ShapeDtypeStruct((n_rows, d), h2.dtype),
        scratch_shapes=[pltpu.SemaphoreType.DMA(())],
        input_output_aliases={2: 0},
        compiler_params=_params("arbitrary"),
        name="moe_dispatch",
    )(dest, h2, zeros)


def _expert_kernel(te_ref, nu_ref, xs_ref, wg_ref, wu_ref, wd_ref, o_ref, hb_ref, acc_ref):
    del te_ref
    j = pl.program_id(0)
    f = pl.program_id(1)

    @pl.when(j < nu_ref[0])
    def _():
        @pl.when(f == 0)
        def _():
            hb_ref[...] = xs_ref[...].astype(BF16)
            acc_ref[...] = jnp.zeros_like(acc_ref)

        h = hb_ref[...]
        a = jnp.dot(h, wg_ref[...], preferred_element_type=F32)
        u = jnp.dot(h, wu_ref[...], preferred_element_type=F32)
        act = (a * _sigmoid(a) * u).astype(BF16)
        acc_ref[...] += jnp.dot(act, wd_ref[...], preferred_element_type=F32)

        @pl.when(f == pl.num_programs(1) - 1)
        def _():
            o_ref[...] = acc_ref[...]

    @pl.when(jnp.logical_and(j >= nu_ref[0], f == pl.num_programs(1) - 1))
    def _():
        o_ref[...] = jnp.zeros_like(o_ref)


def _experts(tile_expert, n_used, xs, wg, wu, wd):
    n_rows, d = xs.shape
    f = wg.shape[-1]
    tf = _pick(f, (896, 512, 256))
    nf = f // tf
    n_tiles = n_rows // EXPERT_TILE

    def row_map(j, ff, te, nu):
        return (jnp.minimum(j, nu[0] - 1), 0)

    def w_up_map(j, ff, te, nu):
        live = j < nu[0]
        return (te[jnp.minimum(j, nu[0] - 1)], 0, jnp.where(live, ff, nf - 1))

    def w_down_map(j, ff, te, nu):
        live = j < nu[0]
        return (te[jnp.minimum(j, nu[0] - 1)], jnp.where(live, ff, nf - 1), 0)

    grid_spec = pltpu.PrefetchScalarGridSpec(
        num_scalar_prefetch=2,
        grid=(n_tiles, nf),
        in_specs=[pl.BlockSpec((EXPERT_TILE, d), row_map),
                  pl.BlockSpec((None, d, tf), w_up_map),
                  pl.BlockSpec((None, d, tf), w_up_map),
                  pl.BlockSpec((None, tf, d), w_down_map)],
        out_specs=pl.BlockSpec((EXPERT_TILE, d), lambda j, ff, te, nu: (j, 0)),
        scratch_shapes=[pltpu.VMEM((EXPERT_TILE, d), BF16), pltpu.VMEM((EXPERT_TILE, d), F32)],
    )
    return pl.pallas_call(
        _expert_kernel,
        grid_spec=grid_spec,
        out_shape=jax.ShapeDtypeStruct((n_rows, d), F32),
        compiler_params=_params("arbitrary", "arbitrary"),
        name="moe_experts",
    )(tile_expert, n_used, xs, wg, wu, wd)


def _combine_kernel(dest_ref, os_ref, route_ref, x_ref, g2_ref, o_ref, buf_ref, sem, *, tm, n_experts):
    def row_copy(r, k):
        d = dest_ref[0, k * tm + r]
        return pltpu.make_async_copy(os_ref.at[pl.ds(d, 1), :], buf_ref.at[k, pl.ds(r, 1), :], sem)

    def issue(r, carry):
        row_copy(r, 0).start()
        row_copy(r, 1).start()
        return carry

    lax.fori_loop(0, tm, issue, 0, unroll=8)

    def drain(r, carry):
        row_copy(r, 0).wait()
        row_copy(r, 1).wait()
        return carry

    lax.fori_loop(0, tm, drain, 0, unroll=8)

    route = route_ref[...]
    lane = lax.broadcasted_iota(jnp.int32, (tm, LANES), 1)
    w1 = jnp.sum(jnp.where(lane == n_experts + 2, route, 0.0), axis=-1, keepdims=True)
    w2 = jnp.sum(jnp.where(lane == n_experts + 3, route, 0.0), axis=-1, keepdims=True)
    o_ref[...] = x_ref[...] + g2_ref[...] * (w1 * buf_ref[0] + w2 * buf_ref[1])


def _combine(dest, os, route, x1, g2, n_experts):
    b, t, d = x1.shape
    tm = dest.shape[-1] // 2
    tiles_per_seq = t // tm
    x_spec = pl.BlockSpec((None, tm, d), lambda bb, i: (bb, i, 0))
    return pl.pallas_call(
        functools.partial(_combine_kernel, tm=tm, n_experts=n_experts),
        grid=(b, tiles_per_seq),
        in_specs=[pl.BlockSpec((None, 1, 2 * tm), lambda bb, i: (bb * tiles_per_seq + i, 0, 0),
                               memory_space=pltpu.SMEM),
                  pl.BlockSpec(memory_space=pl.ANY),
                  pl.BlockSpec((None, tm, LANES), lambda bb, i: (bb, i, 0)),
                  x_spec, _vec_spec(g2)],
        out_specs=x_spec,
        out_shape=jax.ShapeDtypeStruct((b, t, d), F32),
        scratch_shapes=[pltpu.VMEM((2, tm, d), F32), pltpu.SemaphoreType.DMA(())],
        compiler_params=_params("arbitrary", "arbitrary"),
        name="moe_combine",
    )(dest, os, route, x1, g2)


def _moe_routed(h2, route, x1, g2, wg, wu, wd, n_experts):
    b, t, d = x1.shape
    n = b * t
    tm = _pick(t, (512, 256))
    rec = route.reshape(n, LANES)
    rank, counts = _rank(rec, n_experts)

    cnt = counts[0, :n_experts].astype(jnp.int32)
    padded = ((cnt + EXPERT_TILE - 1) // EXPERT_TILE) * EXPERT_TILE
    ends = jnp.cumsum(padded)
    offs = ends - padded
    n_tiles = (2 * n) // EXPERT_TILE + n_experts
    n_used = (ends[-1] // EXPERT_TILE).astype(jnp.int32).reshape(1)
    tile_start = jnp.arange(n_tiles, dtype=jnp.int32) * EXPERT_TILE
    tile_expert = jnp.minimum(jnp.sum(tile_start[:, None] >= ends[None, :], axis=1), n_experts - 1)
    e1 = rec[:, n_experts].astype(jnp.int32)
    e2 = rec[:, n_experts + 1].astype(jnp.int32)
    d1 = jnp.take(offs, e1) + rank[:, 0]
    d2 = jnp.take(offs, e2) + rank[:, 1]
    dest = jnp.concatenate([d1.reshape(n // tm, tm), d2.reshape(n // tm, tm)], axis=1)[:, None, :]

    xs = _dispatch(dest, h2.reshape(n, d), n_tiles * EXPERT_TILE)
    os = _experts(tile_expert.astype(jnp.int32), n_used, xs, wg, wu, wd)
    return _combine(dest, os, route, x1, g2, n_experts)


def _rope_tables(n_tokens):
    rows = n_tokens // GRID_W
    row, col = jnp.meshgrid(jnp.arange(rows, dtype=F32), jnp.arange(GRID_W, dtype=F32), indexing="ij")
    half = HEAD_DIM // 2
    inv_freq = ROPE_THETA ** (-jnp.arange(0, half, 2, dtype=F32) / half)
    ang = jnp.concatenate([row.reshape(-1, 1) * inv_freq, col.reshape(-1, 1) * inv_freq], axis=-1)
    cos = jnp.repeat(jnp.cos(ang), 2, axis=-1)
    sign = jnp.tile(jnp.array([-1.0, 1.0], F32), HEAD_DIM // 2)
    sin = jnp.repeat(jnp.sin(ang), 2, axis=-1) * sign
    reps = LANES // HEAD_DIM
    return jnp.tile(cos, (1, reps)), jnp.tile(sin, (1, reps))


def kernel(x, c, ctx, c_ctx, ada_w, ada_b, norm1_g, norm2_g, attn_w_in, attn_w_out, qnorm_a, knorm_a, qnorm_b, knorm_b, sink_b, ffn_w_gate, ffn_w_up, ffn_w_down, conv_w_in, conv_w, conv_w_out, router_w, moe_w_gate, moe_w_up, moe_w_down):
    bsz, n_tok, d = x.shape
    n_ctx = ctx.shape[1]
    depth = ada_w.shape[0]
    n_experts = router_w.shape[-1]
    assert n_ctx == CHUNK and n_tok % CHUNK == 0 and bsz + 1 <= 8
    assert attn_w_in.shape[-1] == IN_PROJ_WIDTH and d % LANES == 0 and n_experts + 4 <= LANES

    cc = jnp.zeros((8, d), F32).at[:bsz].set(c).at[bsz].set(c_ctx)
    mod = _modulation(cc, ada_w, ada_b)

    def mod_vec(layer, j, is_ctx):
        m = mod[layer, :, j * d:(j + 1) * d]
        return m[bsz:bsz + 1].reshape(1, 1, d) if is_ctx else m[:bsz].reshape(bsz, 1, d)

    cos_l, sin_l = _rope_tables(n_tok)
    cos_c = jnp.ones((n_ctx, LANES), F32)
    sin_c = jnp.zeros((n_ctx, LANES), F32)
    head_id = jnp.arange(LANES) // HEAD_DIM
    bd = (head_id[:, None] == head_id[None, :]).astype(BF16)
    no_sink = jnp.zeros((N_HEADS,), F32)
    tile2 = lambda v: jnp.tile(v, LANES // HEAD_DIM)

    xc = ctx
    for layer in range(depth):
        i = layer // 2
        ctx_needed = any(j % 2 == 0 for j in range(layer + 1, depth))
        n1 = norm1_g[layer].reshape(1, d)
        n2 = norm2_g[layer].reshape(1, d)
        mv = lambda j, is_ctx: mod_vec(layer, j, is_ctx)

        if layer % 2 == 0:
            w_in = attn_w_in[i].astype(BF16)
            w_out = attn_w_out[i].astype(BF16)
            qscale = ATTN_SCALE * LOG2E
            gains = jnp.stack([tile2(qnorm_a[i]) * qscale, tile2(knorm_a[i]),
                               tile2(qnorm_b[i]) * qscale, tile2(knorm_b[i])]).astype(F32)
            sink = sink_b[i].astype(F32)
            qa_l, ka_l, va_l, qb_l, kb_l, vb_l = _attn_in(
                x, n1, mv(1, False), mv(0, False), w_in, gains, cos_l, sin_l, bd, rope=True)
            qa_c, ka_c, va_c, qb_c, kb_c, vb_c = _attn_in(
                xc, n1, mv(1, True), mv(0, True), w_in, gains, cos_c, sin_c, bd, rope=False)
            ka_all = jnp.concatenate([ka_c, ka_l], axis=1)
            va_all = jnp.concatenate([va_c, va_l], axis=1)
            kb_all = jnp.concatenate([kb_c, kb_l], axis=1)
            vb_all = jnp.concatenate([vb_c, vb_l], axis=1)
            oa = _attention(qa_l, ka_all, va_all, no_sink, mode="global", use_sink=False)
            ob = _attention(qb_l, kb_all, vb_all, sink, mode="window", use_sink=True)
            x, h2 = _attn_out(oa, ob, w_out, x, mv(2, False), n2, mv(4, False), mv(3, False))
            if ctx_needed:
                oa_c = _attention(qa_c, ka_c, va_c, no_sink, mode="ctx", use_sink=False)
                ob_c = _attention(qb_c, kb_c, vb_c, sink, mode="ctx", use_sink=True)
                xc, h2c = _attn_out(oa_c, ob_c, w_out, xc, mv(2, True), n2, mv(4, True), mv(3, True))
            wg = ffn_w_gate[i].astype(BF16)[None]
            wu = ffn_w_up[i].astype(BF16)[None]
            wd = ffn_w_down[i].astype(BF16)[None]
            x = _ffn(h2, wg, wu, wd, None, x, mv(5, False))
            if ctx_needed:
                xc = _ffn(h2c, wg, wu, wd, None, xc, mv(5, True))
        else:
            w_in = conv_w_in[i].astype(BF16)
            w_out = conv_w_out[i].astype(BF16)
            cw = conv_w[i].astype(F32)
            router_pad = jnp.zeros((d, LANES), F32).at[:, :n_experts].set(router_w[i])
            wg = moe_w_gate[i].astype(BF16)
            wu = moe_w_up[i].astype(BF16)
            wd = moe_w_down[i].astype(BF16)
            bg, z = _conv_in(x, n1, mv(1, False), mv(0, False), w_in)
            x, h2, route = _conv_out(z, bg, cw, w_out, x, mv(2, False), n2, mv(4, False), mv(3, False),
                                     router_pad, n_experts, F32)
            x = _moe_routed(h2, route, x, mv(5, False), wg, wu, wd, n_experts)
            if ctx_needed:
                bg, z = _conv_in(xc, n1, mv(1, True), mv(0, True), w_in)
                xc, h2c, gates_c = _conv_out(z, bg, cw, w_out, xc, mv(2, True), n2, mv(4, True),
                                             mv(3, True), router_pad, n_experts, BF16)
                xc = _ffn(h2c, wg, wu, wd, gates_c, xc, mv(5, True))
    return x
```

```python
import functools

import jax
import jax.numpy as jnp
from jax import lax
from jax.experimental import pallas as pl
from jax.experimental.pallas import tpu as pltpu

F32 = jnp.float32
BF16 = jnp.bfloat16

HEAD_DIM = 64
N_KV = 2
GROUP = 4
N_HEADS = N_KV * GROUP
Q_WIDTH = N_HEADS * HEAD_DIM
KV_WIDTH = N_KV * HEAD_DIM
IN_PROJ_WIDTH = 2 * (Q_WIDTH + 2 * KV_WIDTH)
GRID_W = 64
WINDOW = 128
ROPE_THETA = 10000.0
ATTN_SCALE = HEAD_DIM ** -0.5
EPS = 1e-6
N_ADA = 6
LOG2E = 1.4426950408889634
LANES = 128
CHUNK = 256
K_AUG = 2 * KV_WIDTH
V_ROWS = HEAD_DIM + 16
GLOBAL_STEP = 4
FAST_MAX_LOGIT = 40.0
NEG_BIG = -1e30
VMEM_LIMIT_BYTES = 56 * 1024 * 1024


def _params(*sem):
    return pltpu.CompilerParams(dimension_semantics=sem, vmem_limit_bytes=VMEM_LIMIT_BYTES)


def _pick(n, candidates):
    for t in candidates:
        if n % t == 0:
            return t
    return n


def _sigmoid(a):
    return 1.0 / (1.0 + jnp.exp(-a))


def _norm_mod(x, g, sc, sh):
    ms = jnp.mean(x * x, axis=-1, keepdims=True)
    return x * lax.rsqrt(ms + EPS) * g * (1.0 + sc) + sh


def _mod_kernel(c_ref, w_ref, b_ref, o_ref):
    c = c_ref[...]
    s = c * _sigmoid(c)
    o_ref[...] = jnp.dot(s, w_ref[...], preferred_element_type=F32,
                         precision=lax.Precision.HIGHEST) + b_ref[...]


def _modulation(cc, ada_w, ada_b):
    depth, d, n = ada_w.shape
    tn = _pick(n, (1536, 1024, 512))
    rows = cc.shape[0]
    return pl.pallas_call(
        _mod_kernel,
        grid=(depth, n // tn),
        in_specs=[
            pl.BlockSpec((rows, d), lambda l, j: (0, 0)),
            pl.BlockSpec((None, d, tn), lambda l, j: (l, 0, j)),
            pl.BlockSpec((None, 1, tn), lambda l, j: (l, 0, j)),
        ],
        out_specs=pl.BlockSpec((None, rows, tn), lambda l, j: (l, 0, j)),
        out_shape=jax.ShapeDtypeStruct((depth, rows, n), F32),
        compiler_params=_params("arbitrary", "arbitrary"),
        name="modulation",
    )(cc, ada_w, ada_b.reshape(depth, 1, n))


def _vec_spec(arr):
    d = arr.shape[-1]
    if arr.shape[0] == 1:
        return pl.BlockSpec((None, 1, d), lambda b, i, *_: (0, 0, 0))
    return pl.BlockSpec((None, 1, d), lambda b, i, *_: (b, 0, 0))


def _attn_in_kernel(x_ref, g_ref, sc_ref, sh_ref, w_ref, gains_ref, cos_ref, sin_ref, bd_ref,
                    qa_ref, ka_ref, va_ref, qb_ref, kb_ref, vb_ref, *, rope, tm):
    h = _norm_mod(x_ref[...], g_ref[...], sc_ref[...], sh_ref[...]).astype(BF16)
    y = jnp.dot(h, w_ref[...], preferred_element_type=F32)
    lane = lax.broadcasted_iota(jnp.int32, (tm, LANES), 1)
    even = (lane & 1) == 0
    bd = bd_ref[...]

    def head_norm(yc, gain):
        ss = jnp.dot((yc * yc).astype(BF16), bd, preferred_element_type=F32)
        t = yc * lax.rsqrt(ss * (1.0 / HEAD_DIM) + EPS) * gain
        if rope:
            partner = jnp.where(even, pltpu.roll(t, LANES - 1, 1), pltpu.roll(t, 1, 1))
            t = t * cos_ref[...] + partner * sin_ref[...]
        return t

    groups = ((0, qa_ref, ka_ref, va_ref, 0), (Q_WIDTH + 2 * KV_WIDTH, qb_ref, kb_ref, vb_ref, 2))
    for col0, q_ref, k_ref, v_ref, grow in groups:
        gq = gains_ref[grow:grow + 1, :]
        gk = gains_ref[grow + 1:grow + 2, :]
        for c in range(Q_WIDTH // LANES):
            t = head_norm(y[:, col0 + c * LANES:col0 + (c + 1) * LANES], gq)
            q_ref[c * LANES:(c + 1) * LANES, :] = t.T.astype(BF16)
        kcol = col0 + Q_WIDTH
        k_ref[:, 0:KV_WIDTH] = head_norm(y[:, kcol:kcol + KV_WIDTH], gk).astype(BF16)
        k_ref[:, KV_WIDTH:K_AUG] = jnp.where(lane == 0, 1.0, 0.0).astype(BF16)
        vt = y[:, kcol + KV_WIDTH:kcol + 2 * KV_WIDTH].T.astype(BF16)
        ones_rows = jnp.where(lax.broadcasted_iota(jnp.int32, (V_ROWS - HEAD_DIM, CHUNK), 0) == 0,
                              1.0, 0.0).astype(BF16)
        for j in range(tm // CHUNK):
            for kv in range(N_KV):
                r0 = kv * V_ROWS
                v_ref[j, r0:r0 + HEAD_DIM, :] = vt[kv * HEAD_DIM:(kv + 1) * HEAD_DIM,
                                                   j * CHUNK:(j + 1) * CHUNK]
                v_ref[j, r0 + HEAD_DIM:r0 + V_ROWS, :] = ones_rows


def _attn_in(x, g, sc, sh, w_bf16, gains, cos_t, sin_t, bd, *, rope):
    b, t, d = x.shape
    tm = _pick(t, (512, 256))
    nch = t // CHUNK
    kern = functools.partial(_attn_in_kernel, rope=rope, tm=tm)
    q_spec = pl.BlockSpec((None, Q_WIDTH, tm), lambda bb, i: (bb, 0, i))
    k_spec = pl.BlockSpec((None, tm, K_AUG), lambda bb, i: (bb, i, 0))
    v_spec = pl.BlockSpec((None, tm // CHUNK, N_KV * V_ROWS, CHUNK), lambda bb, i: (bb, i, 0, 0))
    q_shape = jax.ShapeDtypeStruct((b, Q_WIDTH, t), BF16)
    k_shape = jax.ShapeDtypeStruct((b, t, K_AUG), BF16)
    v_shape = jax.ShapeDtypeStruct((b, nch, N_KV * V_ROWS, CHUNK), BF16)
    return pl.pallas_call(
        kern,
        grid=(b, t // tm),
        in_specs=[
            pl.BlockSpec((None, tm, d), lambda bb, i: (bb, i, 0)),
            pl.BlockSpec((1, d), lambda bb, i: (0, 0)),
            _vec_spec(sc), _vec_spec(sh),
            pl.BlockSpec((d, IN_PROJ_WIDTH), lambda bb, i: (0, 0)),
            pl.BlockSpec((4, LANES), lambda bb, i: (0, 0)),
            pl.BlockSpec((tm, LANES), lambda bb, i: (i, 0)),
            pl.BlockSpec((tm, LANES), lambda bb, i: (i, 0)),
            pl.BlockSpec((LANES, LANES), lambda bb, i: (0, 0)),
        ],
        out_specs=[q_spec, k_spec, v_spec, q_spec, k_spec, v_spec],
        out_shape=[q_shape, k_shape, v_shape, q_shape, k_shape, v_shape],
        compiler_params=_params("parallel", "parallel"),
        name="attn_in",
    )(x, g, sc, sh, w_bf16, gains, cos_t, sin_t, bd)


def _attn_kernel(q_ref, k_ref, v_ref, sink_ref, o_ref, qpad_ref, s_ref, cmax_ref, m_ref, l_ref,
                 acc_ref, *, mode, use_sink, n_lat, tq):
    i = pl.program_id(1)
    w = GROUP * tq
    zeros = jnp.zeros((HEAD_DIM, w), BF16)
    for kv in range(N_KV):
        heads = range(kv * GROUP, (kv + 1) * GROUP)
        q4 = jnp.concatenate([q_ref[h * HEAD_DIM:(h + 1) * HEAD_DIM, :] for h in heads], axis=1)
        qpad_ref[kv] = jnp.concatenate([q4, zeros] if kv == 0 else [zeros, q4], axis=0)
        if use_sink:
            m_ref[kv] = jnp.concatenate(
                [jnp.full((1, tq), sink_ref[h] * LOG2E, F32) for h in heads], axis=1)
            l_ref[kv] = jnp.ones((1, w), F32)
        else:
            m_ref[kv] = jnp.full((1, w), NEG_BIG, F32)
            l_ref[kv] = jnp.zeros((1, w), F32)
        acc_ref[kv] = jnp.zeros((HEAD_DIM, w), F32)

    def scores(c, slot, mask):
        start = c * CHUNK if isinstance(c, int) else pl.multiple_of(c * CHUNK, CHUNK)
        kc = k_ref[pl.ds(start, CHUNK), 0:KV_WIDTH]
        for kv in range(N_KV):
            s = jnp.dot(kc, qpad_ref[kv], preferred_element_type=F32)
            if mask is not None:
                s = jnp.where(mask, s, -jnp.inf)
            s_ref[slot, kv] = s
            cmax_ref[slot, kv] = jnp.max(s, axis=0, keepdims=True)

    def absorb(c, slot):
        for kv in range(N_KV):
            m = m_ref[kv]
            m_new = jnp.maximum(m, cmax_ref[slot, kv])
            alpha = jnp.exp2(m - m_new)
            p = jnp.exp2(s_ref[slot, kv] - m_new)
            l_ref[kv] = alpha * l_ref[kv] + jnp.sum(p, axis=0, keepdims=True)
            m_ref[kv] = m_new
            vc = v_ref[c, kv * V_ROWS:kv * V_ROWS + HEAD_DIM, :]
            acc_ref[kv] = alpha * acc_ref[kv] + jnp.dot(vc, p.astype(BF16),
                                                        preferred_element_type=F32)

    scores(0, 0, None)
    if mode == "global":
        def body(j, carry):
            c = 2 * j
            scores(c + 1, 1, None)
            absorb(c, 0)
            scores(c + 2, 0, None)
            absorb(c + 1, 1)
            return carry
        lax.fori_loop(0, n_lat // 2, body, 0)
        c_end = 2 * (n_lat // 2)
        if n_lat % 2:
            scores(c_end + 1, 1, None)
            absorb(c_end, 0)
            absorb(c_end + 1, 1)
        else:
            absorb(c_end, 0)
    elif mode == "window":
        row = lax.broadcasted_iota(jnp.int32, (CHUNK, w), 0)
        col = lax.broadcasted_iota(jnp.int32, (CHUNK, w), 1) & (tq - 1)
        rel0 = col - row
        prev_c, prev_slot = 0, 0
        for d in (-1, 0, 1):
            cl = i + d
            ok = jnp.logical_and(cl >= 0, cl < n_lat)
            c = 1 + jnp.clip(cl, 0, n_lat - 1)
            mask = jnp.logical_and(jnp.abs(rel0 - d * CHUNK) <= WINDOW, ok)
            scores(c, 1 - prev_slot, mask)
            absorb(prev_c, prev_slot)
            prev_c, prev_slot = c, 1 - prev_slot
        absorb(prev_c, prev_slot)
    else:
        absorb(0, 0)

    for kv in range(N_KV):
        out = acc_ref[kv] * (1.0 / l_ref[kv])
        for g in range(GROUP):
            h = kv * GROUP + g
            o_ref[h * HEAD_DIM:(h + 1) * HEAD_DIM, :] = out[:, g * tq:(g + 1) * tq].astype(BF16)


def _attention(q_t, k_all, v_all, sink, bound, *, mode, use_sink):
    del bound
    b, _, tq_total = q_t.shape
    nk = k_all.shape[1]
    nc = v_all.shape[1]
    tq = CHUNK
    kern = functools.partial(_attn_kernel, mode=mode, use_sink=use_sink, n_lat=nc - 1, tq=tq)
    return pl.pallas_call(
        kern,
        grid=(b, tq_total // tq),
        in_specs=[
            pl.BlockSpec((None, Q_WIDTH, tq), lambda bb, i: (bb, 0, i)),
            pl.BlockSpec((None, nk, K_AUG), lambda bb, i: (bb, 0, 0)),
            pl.BlockSpec((None, nc, N_KV * V_ROWS, CHUNK), lambda bb, i: (bb, 0, 0, 0)),
            pl.BlockSpec(memory_space=pltpu.SMEM),
        ],
        out_specs=pl.BlockSpec((None, Q_WIDTH, tq), lambda bb, i: (bb, 0, i)),
        out_shape=jax.ShapeDtypeStruct((b, Q_WIDTH, tq_total), BF16),
        scratch_shapes=[pltpu.VMEM((N_KV, KV_WIDTH, GROUP * tq), BF16),
                        pltpu.VMEM((2, N_KV, CHUNK, GROUP * tq), F32),
                        pltpu.VMEM((2, N_KV, 1, GROUP * tq), F32),
                        pltpu.VMEM((N_KV, 1, GROUP * tq), F32),
                        pltpu.VMEM((N_KV, 1, GROUP * tq), F32),
                        pltpu.VMEM((N_KV, HEAD_DIM, GROUP * tq), F32)],
        compiler_params=_params("parallel", "parallel"),
        name="attn_" + mode,
    )(q_t, k_all, v_all, sink)


def _attn_fast_kernel(q_ref, k_ref, v_ref, sink_ref, bound_ref, o_ref, qa_ref, acc_ref, *,
                      mode, use_sink, n_lat, tq):
    i = pl.program_id(1)
    w = GROUP * tq
    bound = bound_ref[0]
    zeros = jnp.zeros((HEAD_DIM, w), BF16)
    row = lax.broadcasted_iota(jnp.int32, (KV_WIDTH, w), 0)
    shift = jnp.where(row == 0, -bound, 0.0).astype(BF16)
    for kv in range(N_KV):
        heads = range(kv * GROUP, (kv + 1) * GROUP)
        q4 = jnp.concatenate([q_ref[h * HEAD_DIM:(h + 1) * HEAD_DIM, :] for h in heads], axis=1)
        qa_ref[kv] = jnp.concatenate(([q4, zeros] if kv == 0 else [zeros, q4]) + [shift], axis=0)
        acc_ref[kv] = jnp.zeros((V_ROWS, w), F32)

    def update(c, mask, n=1):
        start = c * CHUNK if isinstance(c, int) else pl.multiple_of(c * CHUNK, CHUNK)
        kc = k_ref[pl.ds(start, n * CHUNK), :]
        for kv in range(N_KV):
            s = jnp.dot(kc, qa_ref[kv], preferred_element_type=F32)
            if mask is not None:
                s = jnp.where(mask, s, -jnp.inf)
            p = jnp.exp2(s).astype(BF16)
            vc = jnp.concatenate([v_ref[c + j, kv * V_ROWS:(kv + 1) * V_ROWS, :] for j in range(n)],
                                 axis=1)
            acc_ref[kv] += jnp.dot(vc, p, preferred_element_type=F32)

    update(0, None)
    if mode == "global":
        def body(j, carry):
            update(1 + GLOBAL_STEP * j, None, GLOBAL_STEP)
            return carry
        lax.fori_loop(0, n_lat // GLOBAL_STEP, body, 0)
        for c in range(1 + GLOBAL_STEP * (n_lat // GLOBAL_STEP), 1 + n_lat):
            update(c, None)
    elif mode == "window":
        rowk =lax.broadcasted_iota(jnp.int32, (CHUNK, w), 0)
        col = lax.broadcasted_iota(jnp.int32, (CHUNK, w), 1) & (tq - 1)
        rel0 = col - rowk
        for d in (-1, 0, 1):
            cl = i + d
            ok = jnp.logical_and(cl >= 0, cl < n_lat)
            c = 1 + jnp.clip(cl, 0, n_lat - 1)
            update(c, jnp.logical_and(jnp.abs(rel0 - d * CHUNK) <= WINDOW, ok))

    for kv in range(N_KV):
        acc = acc_ref[kv]
        l = acc[HEAD_DIM:HEAD_DIM + 1, :]
        if use_sink:
            l = l + jnp.concatenate(
                [jnp.full((1, tq), jnp.exp2(sink_ref[kv * GROUP + g] * LOG2E - bound), F32)
                 for g in range(GROUP)], axis=1)
        out = acc[0:HEAD_DIM, :] * (1.0 / l)
        for g in range(GROUP):
            h = kv * GROUP + g
            o_ref[h * HEAD_DIM:(h + 1) * HEAD_DIM, :] = out[:, g * tq:(g + 1) * tq].astype(BF16)


def _attention_fast(q_t, k_all, v_all, sink, bound, *, mode, use_sink):
    b, _, tq_total = q_t.shape
    nk = k_all.shape[1]
    nc = v_all.shape[1]
    tq = CHUNK
    kern = functools.partial(_attn_fast_kernel, mode=mode, use_sink=use_sink, n_lat=nc - 1, tq=tq)
    return pl.pallas_call(
        kern,
        grid=(b, tq_total // tq),
        in_specs=[
            pl.BlockSpec((None, Q_WIDTH, tq), lambda bb, i: (bb, 0, i)),
            pl.BlockSpec((None, nk, K_AUG), lambda bb, i: (bb, 0, 0)),
            pl.BlockSpec((None, nc, N_KV * V_ROWS, CHUNK), lambda bb, i: (bb, 0, 0, 0)),
            pl.BlockSpec(memory_space=pltpu.SMEM),
            pl.BlockSpec(memory_space=pltpu.SMEM),
        ],
        out_specs=pl.BlockSpec((None, Q_WIDTH, tq), lambda bb, i: (bb, 0, i)),
        out_shape=jax.ShapeDtypeStruct((b, Q_WIDTH, tq_total), BF16),
        scratch_shapes=[pltpu.VMEM((N_KV, K_AUG, GROUP * tq), BF16),
                        pltpu.VMEM((N_KV, V_ROWS, GROUP * tq), F32)],
        compiler_params=_params("parallel", "parallel"),
        name="attn_fast_" + mode,
    )(q_t, k_all, v_all, sink, bound)


def _attend(fast_ok, *args, **kw):
    return lax.cond(fast_ok, functools.partial(_attention_fast, **kw),
                    functools.partial(_attention, **kw), *args)


_TN_DIMS = (((0,), (0,)), ((), ()))


def _attn_out_kernel(oa_ref, ob_ref, w_ref, x_ref, g1_ref, n2_ref, sc2_ref, sh2_ref, x1_ref, h2_ref):
    mix = lax.dot_general(oa_ref[...], w_ref[0:Q_WIDTH, :], _TN_DIMS, preferred_element_type=F32)
    mix = mix + lax.dot_general(ob_ref[...], w_ref[Q_WIDTH:2 * Q_WIDTH, :], _TN_DIMS,
                                preferred_element_type=F32)
    x1 = x_ref[...] + g1_ref[...] * mix
    x1_ref[...] = x1
    h2_ref[...] = _norm_mod(x1, n2_ref[...], sc2_ref[...], sh2_ref[...]).astype(BF16)


def _attn_out(oa, ob, w_bf16, x, g1, n2, sc2, sh2):
    b, t, d = x.shape
    tm = _pick(t, (512, 256))
    o_spec = pl.BlockSpec((None, Q_WIDTH, tm), lambda bb, i: (bb, 0, i))
    x_spec = pl.BlockSpec((None, tm, d), lambda bb, i: (bb, i, 0))
    return pl.pallas_call(
        _attn_out_kernel,
        grid=(b, t // tm),
        in_specs=[o_spec, o_spec,
                  pl.BlockSpec((2 * Q_WIDTH, d), lambda bb, i: (0, 0)),
                  x_spec, _vec_spec(g1),
                  pl.BlockSpec((1, d), lambda bb, i: (0, 0)),
                  _vec_spec(sc2), _vec_spec(sh2)],
        out_specs=[x_spec, x_spec],
        out_shape=[jax.ShapeDtypeStruct((b, t, d), F32), jax.ShapeDtypeStruct((b, t, d), BF16)],
        compiler_params=_params("parallel", "parallel"),
        name="attn_out",
    )(oa, ob, w_bf16, x, g1, n2, sc2, sh2)


def _ffn_kernel(*refs, gated, tm):
    if gated:
        h_ref, wg_ref, wu_ref, wd_ref, gate_ref, x_ref, g2_ref, o_ref, acc_ref = refs
    else:
        h_ref, wg_ref, wu_ref, wd_ref, x_ref, g2_ref, o_ref, acc_ref = refs
    e = pl.program_id(2)
    f = pl.program_id(3)
    first = jnp.logical_and(e == 0, f == 0)
    last = jnp.logical_and(e == pl.num_programs(2) - 1, f == pl.num_programs(3) - 1)

    @pl.when(first)
    def _():
        acc_ref[...] = jnp.zeros_like(acc_ref)

    h = h_ref[...]
    a = jnp.dot(h, wg_ref[...], preferred_element_type=F32)
    u = jnp.dot(h, wu_ref[...], preferred_element_type=F32)
    act = (a * _sigmoid(a) * u).astype(BF16)
    y = jnp.dot(act, wd_ref[...], preferred_element_type=F32)
    if gated:
        lane = lax.broadcasted_iota(jnp.int32, (tm, LANES), 1)
        ge = jnp.sum(jnp.where(lane == e, gate_ref[...], 0.0), axis=-1, keepdims=True)
        y = y * ge
    acc_ref[...] += y

    @pl.when(last)
    def _():
        o_ref[...] = x_ref[...] + g2_ref[...] * acc_ref[...]


def _ffn(h2, wg, wu, wd, gates, x1, g2):
    b, t, d = x1.shape
    n_e, _, f = wg.shape
    tm = _pick(t, (512, 256))
    tf = _pick(f, (1792, 1408, 1024, 512))
    gated = gates is not None
    x_spec = pl.BlockSpec((None, tm, d), lambda bb, i, e, j: (bb, i, 0))
    in_specs = [x_spec,
                pl.BlockSpec((None, d, tf), lambda bb, i, e, j: (e, 0, j)),
                pl.BlockSpec((None, d, tf), lambda bb, i, e, j: (e, 0, j)),
                pl.BlockSpec((None, tf, d), lambda bb, i, e, j: (e, j, 0))]
    args = [h2, wg, wu, wd]
    if gated:
        in_specs.append(pl.BlockSpec((None, tm, LANES), lambda bb, i, e, j: (bb, i, 0)))
        args.append(gates)
    in_specs += [x_spec, _vec_spec(g2)]
    args += [x1, g2]
    return pl.pallas_call(
        functools.partial(_ffn_kernel, gated=gated, tm=tm),
        grid=(b, t // tm, n_e, f // tf),
        in_specs=in_specs,
        out_specs=x_spec,
        out_shape=jax.ShapeDtypeStruct((b, t, d), F32),
        scratch_shapes=[pltpu.VMEM((tm, d), F32)],
        compiler_params=_params("parallel", "parallel", "arbitrary", "arbitrary"),
        name="ffn_gated" if gated else "ffn_dense",
    )(*args)


def _conv_in_kernel(x_ref, g_ref, sc_ref, sh_ref, w_ref, bg_ref, z_ref, *, d):
    h = _norm_mod(x_ref[...], g_ref[...], sc_ref[...], sh_ref[...]).astype(BF16)
    y = jnp.dot(h, w_ref[...], preferred_element_type=F32)
    bg_ref[...] = y[:, 0:d].astype(BF16)
    z_ref[...] = (y[:, d:2 * d] * y[:, 2 * d:3 * d]).astype(BF16)


def _conv_in(x, g, sc, sh, w_bf16):
    b, t, d = x.shape
    tm = _pick(t, (512, 256))
    x_spec = pl.BlockSpec((None, tm, d), lambda bb, i: (bb, i, 0))
    return pl.pallas_call(
        functools.partial(_conv_in_kernel, d=d),
        grid=(b, t // tm),
        in_specs=[x_spec, pl.BlockSpec((1, d), lambda bb, i: (0, 0)), _vec_spec(sc), _vec_spec(sh),
                  pl.BlockSpec((d, 3 * d), lambda bb, i: (0, 0))],
        out_specs=[x_spec, x_spec],
        out_shape=[jax.ShapeDtypeStruct((b, t, d), BF16), jax.ShapeDtypeStruct((b, t, d), BF16)],
        compiler_params=_params("parallel", "parallel"),
        name="conv_in",
    )(x, g, sc, sh, w_bf16)


HALO = 16


def _conv_out_kernel(z_ref, zp_ref, zn_ref, bg_ref, cw_ref, w_ref, x_ref, g1_ref, n2_ref, sc2_ref,
                     sh2_ref, rw_ref, x1_ref, h2_ref, gate_ref, *, tm, n_experts):
    i = pl.program_id(1)
    z = z_ref[...].astype(F32)
    prev = jnp.where(i > 0, zp_ref[HALO - 1:HALO, :].astype(F32), 0.0)
    nxt = jnp.where(i < pl.num_programs(1) - 1, zn_ref[0:1, :].astype(F32), 0.0)
    row = lax.broadcasted_iota(jnp.int32, z.shape, 0)
    z_dn = jnp.where(row == 0, prev, pltpu.roll(z, 1, 0))
    z_up = jnp.where(row == tm - 1, nxt, pltpu.roll(z, tm - 1, 0))
    conv = z_dn * cw_ref[0:1, :] + z * cw_ref[1:2, :] + z_up * cw_ref[2:3, :]
    v = (bg_ref[...].astype(F32) * conv).astype(BF16)
    mix = jnp.dot(v, w_ref[...], preferred_element_type=F32)
    x1 = x_ref[...] + g1_ref[...] * mix
    x1_ref[...] = x1
    h2 = _norm_mod(x1, n2_ref[...], sc2_ref[...], sh2_ref[...])
    h2_ref[...] = h2.astype(h2_ref.dtype)

    logits = jnp.dot(h2, rw_ref[...], preferred_element_type=F32, precision=lax.Precision.HIGHEST)
    lane = lax.broadcasted_iota(jnp.int32, logits.shape, 1)
    lanef = lane.astype(F32)
    logits = jnp.where(lane < n_experts, logits, -jnp.inf)
    m1 = jnp.max(logits, axis=-1, keepdims=True)
    i1 = jnp.min(jnp.where(logits == m1, lanef, float(LANES)), axis=-1, keepdims=True)
    rest = jnp.where(lanef == i1, -jnp.inf, logits)
    m2 = jnp.max(rest, axis=-1, keepdims=True)
    i2 = jnp.min(jnp.where(rest == m2, lanef, float(LANES)), axis=-1, keepdims=True)
    e2 = jnp.exp(m2 - m1)
    w1 = 1.0 / (1.0 + e2)
    w2 = e2 * w1
    rec = jnp.where(lanef == i1, w1, 0.0) + jnp.where(lanef == i2, w2, 0.0)
    for k, val in enumerate((i1, i2, w1, w2)):
        rec = jnp.where(lane == n_experts + k, val, rec)
    gate_ref[...] = rec


def _conv_out(z, bg, conv_w, w_bf16, x, g1, n2, sc2, sh2, router_pad, n_experts, h2_dtype):
    b, t, d = x.shape
    tm = _pick(t, (512, 256))
    hb = tm // HALO
    n_halo = t // HALO
    x_spec = pl.BlockSpec((None, tm, d), lambda bb, i: (bb, i, 0))
    prev_spec = pl.BlockSpec((None, HALO, d), lambda bb, i: (bb, jnp.maximum(i * hb - 1, 0), 0))
    next_spec = pl.BlockSpec((None, HALO, d), lambda bb, i: (bb, jnp.minimum((i + 1) * hb, n_halo - 1), 0))
    full = lambda r, c: pl.BlockSpec((r, c), lambda bb, i: (0, 0))
    return pl.pallas_call(
        functools.partial(_conv_out_kernel, tm=tm, n_experts=n_experts),
        grid=(b, t // tm),
        in_specs=[x_spec, prev_spec, next_spec, x_spec, full(3, d), full(d, d), x_spec, _vec_spec(g1),
                  full(1, d), _vec_spec(sc2), _vec_spec(sh2), full(d, LANES)],
        out_specs=[x_spec, x_spec, pl.BlockSpec((None, tm, LANES), lambda bb, i: (bb, i, 0))],
        out_shape=[jax.ShapeDtypeStruct((b, t, d), F32), jax.ShapeDtypeStruct((b, t, d), h2_dtype),
                   jax.ShapeDtypeStruct((b, t, LANES), F32)],
        compiler_params=_params("parallel", "parallel"),
        name="conv_out",
    )(z, z, z, bg, conv_w, w_bf16, x, g1, n2, sc2, sh2, router_pad)


EXPERT_TILE = 1024


def _rank_kernel(route_ref, rank_ref, count_ref, base_ref, *, tm, n_experts):
    @pl.when(pl.program_id(0) == 0)
    def _():
        base_ref[...] = jnp.zeros_like(base_ref)

    route = route_ref[...]
    lane = lax.broadcasted_iota(jnp.int32, (tm, LANES), 1)
    lanef = lane.astype(F32)
    i1 = jnp.sum(jnp.where(lane == n_experts, route, 0.0), axis=-1, keepdims=True)
    i2 = jnp.sum(jnp.where(lane == n_experts + 1, route, 0.0), axis=-1, keepdims=True)
    onehot = jnp.where(lanef == i1, 1.0, 0.0) + jnp.where(lanef == i2, 1.0, 0.0)
    r = lax.broadcasted_iota(jnp.int32, (tm, tm), 0)
    c = lax.broadcasted_iota(jnp.int32, (tm, tm), 1)
    lower = jnp.where(c < r, 1.0, 0.0).astype(BF16)
    before = jnp.dot(lower, onehot.astype(BF16), preferred_element_type=F32) + base_ref[...]
    r1 = jnp.sum(jnp.where(lanef == i1, before, 0.0), axis=-1, keepdims=True)
    r2 = jnp.sum(jnp.where(lanef == i2, before, 0.0), axis=-1, keepdims=True)
    rank_ref[...] = jnp.where(lane == 0, r1, jnp.where(lane == 1, r2, 0.0)).astype(jnp.int32)
    base_ref[...] += jnp.sum(onehot, axis=0, keepdims=True)
    count_ref[...] = base_ref[...]


def _rank(route, n_experts):
    n = route.shape[0]
    tm = _pick(n, (512, 256))
    return pl.pallas_call(
        functools.partial(_rank_kernel, tm=tm, n_experts=n_experts),
        grid=(n // tm,),
        in_specs=[pl.BlockSpec((tm, LANES), lambda i: (i, 0))],
        out_specs=[pl.BlockSpec((tm, LANES), lambda i: (i, 0)), pl.BlockSpec((1, LANES), lambda i: (0, 0))],
        out_shape=[jax.ShapeDtypeStruct((n, LANES), jnp.int32), jax.ShapeDtypeStruct((1, LANES), F32)],
        scratch_shapes=[pltpu.VMEM((1, LANES), F32)],
        compiler_params=_params("arbitrary"),
        name="moe_rank",
    )(route)


def _dispatch_kernel(dest_ref, h_ref, xs_in_ref, xs_ref, sem, *, tm):
    del xs_in_ref

    def row_copy(r, k):
        d = dest_ref[0, k * tm + r]
        return pltpu.make_async_copy(h_ref.at[pl.ds(r, 1), :], xs_ref.at[pl.ds(d, 1), :], sem)

    def issue(r, carry):
        row_copy(r, 0).start()
        row_copy(r, 1).start()
        return carry

    lax.fori_loop(0, tm, issue, 0, unroll=8)

    def drain(r, carry):
        row_copy(r, 0).wait()
        row_copy(r, 1).wait()
        return carry

    lax.fori_loop(0, tm, drain, 0, unroll=8)


def _dispatch(dest, h2, n_rows):
    n, d = h2.shape
    tm = dest.shape[-1] // 2
    zeros = jnp.zeros((n_rows, d), h2.dtype)
    return pl.pallas_call(
        functools.partial(_dispatch_kernel, tm=tm),
        grid=(n // tm,),
        in_specs=[pl.BlockSpec((None, 1, 2 * tm), lambda i: (i, 0, 0), memory_space=pltpu.SMEM),
                  pl.BlockSpec((tm, d), lambda i: (i, 0)),
                  pl.BlockSpec(memory_space=pl.ANY)],
        out_specs=pl.BlockSpec(memory_space=pl.ANY),
        out_shape=jax.ShapeDtypeStruct((n_rows, d), h2.dtype),
        scratch_shapes=[pltpu.SemaphoreType.DMA(())],
        input_output_aliases={2: 0},
        compiler_params=_params("arbitrary"),
        name="moe_dispatch",
    )(dest, h2, zeros)


def _expert_kernel(te_ref, nu_ref, xs_ref, wg_ref, wu_ref, wd_ref, o_ref, hb_ref, acc_ref):
    del te_ref
    j = pl.program_id(0)
    f = pl.program_id(1)

    @pl.when(j < nu_ref[0])
    def _():
        @pl.when(f == 0)
        def _():
            hb_ref[...] = xs_ref[...].astype(BF16)
            acc_ref[...] = jnp.zeros_like(acc_ref)

        h = hb_ref[...]
        a = jnp.dot(h, wg_ref[...], preferred_element_type=F32)
        u = jnp.dot(h, wu_ref[...], preferred_element_type=F32)
        act = (a * _sigmoid(a) * u).astype(BF16)
        acc_ref[...] += jnp.dot(act, wd_ref[...], preferred_element_type=F32)

        @pl.when(f == pl.num_programs(1) - 1)
        def _():
            o_ref[...] = acc_ref[...]

    @pl.when(jnp.logical_and(j >= nu_ref[0], f == pl.num_programs(1) - 1))
    def _():
        o_ref[...] = jnp.zeros_like(o_ref)


def _experts(tile_expert, n_used, xs, wg, wu, wd):
    n_rows, d = xs.shape
    f = wg.shape[-1]
    tf = _pick(f, (896, 512, 256))
    nf = f // tf
    n_tiles = n_rows // EXPERT_TILE

    def row_map(j, ff, te, nu):
        return (jnp.minimum(j, nu[0] - 1), 0)

    def w_up_map(j, ff, te, nu):
        live = j < nu[0]
        return (te[jnp.minimum(j, nu[0] - 1)], 0, jnp.where(live, ff, nf - 1))

    def w_down_map(j, ff, te, nu):
        live = j < nu[0]
        return (te[jnp.minimum(j, nu[0] - 1)], jnp.where(live, ff, nf - 1), 0)

    grid_spec = pltpu.PrefetchScalarGridSpec(
        num_scalar_prefetch=2,
        grid=(n_tiles, nf),
        in_specs=[pl.BlockSpec((EXPERT_TILE, d), row_map),
                  pl.BlockSpec((None, d, tf), w_up_map),
                  pl.BlockSpec((None, d, tf), w_up_map),
                  pl.BlockSpec((None, tf, d), w_down_map)],
        out_specs=pl.BlockSpec((EXPERT_TILE, d), lambda j, ff, te, nu: (j, 0)),
        scratch_shapes=[pltpu.VMEM((EXPERT_TILE, d), BF16), pltpu.VMEM((EXPERT_TILE, d), F32)],
    )
    return pl.pallas_call(
        _expert_kernel,
        grid_spec=grid_spec,
        out_shape=jax.ShapeDtypeStruct((n_rows, d), F32),
        compiler_params=_params("arbitrary", "arbitrary"),
        name="moe_experts",
    )(tile_expert, n_used, xs, wg, wu, wd)


def _combine_kernel(dest_ref, os_ref, route_ref, x_ref, g2_ref, o_ref, buf_ref, sem, *, tm, n_experts):
    def row_copy(r, k):
        d = dest_ref[0, k * tm + r]
        return pltpu.make_async_copy(os_ref.at[pl.ds(d, 1), :], buf_ref.at[k, pl.ds(r, 1), :], sem)

    def issue(r, carry):
        row_copy(r, 0).start()
        row_copy(r, 1).start()
        return carry

    lax.fori_loop(0, tm, issue, 0, unroll=8)

    def drain(r, carry):
        row_copy(r, 0).wait()
        row_copy(r, 1).wait()
        return carry

    lax.fori_loop(0, tm, drain, 0, unroll=8)

    route = route_ref[...]
    lane = lax.broadcasted_iota(jnp.int32, (tm, LANES), 1)
    w1 = jnp.sum(jnp.where(lane == n_experts + 2, route, 0.0), axis=-1, keepdims=True)
    w2 = jnp.sum(jnp.where(lane == n_experts + 3, route, 0.0), axis=-1, keepdims=True)
    o_ref[...] = x_ref[...] + g2_ref[...] * (w1 * buf_ref[0] + w2 * buf_ref[1])


def _combine(dest, os, route, x1, g2, n_experts):
    b, t, d = x1.shape
    tm = dest.shape[-1] // 2
    tiles_per_seq = t // tm
    x_spec = pl.BlockSpec((None, tm, d), lambda bb, i: (bb, i, 0))
    return pl.pallas_call(
        functools.partial(_combine_kernel, tm=tm, n_experts=n_experts),
        grid=(b, tiles_per_seq),
        in_specs=[pl.BlockSpec((None, 1, 2 * tm), lambda bb, i: (bb * tiles_per_seq + i, 0, 0),
                               memory_space=pltpu.SMEM),
                  pl.BlockSpec(memory_space=pl.ANY),
                  pl.BlockSpec((None, tm, LANES), lambda bb, i: (bb, i, 0)),
                  x_spec, _vec_spec(g2)],
        out_specs=x_spec,
        out_shape=jax.ShapeDtypeStruct((b, t, d), F32),
        scratch_shapes=[pltpu.VMEM((2, tm, d), F32), pltpu.SemaphoreType.DMA(())],
        compiler_params=_params("arbitrary", "arbitrary"),
        name="moe_combine",
    )(dest, os, route, x1, g2)


def _moe_routed(h2, route, x1, g2, wg, wu, wd, n_experts):
    b, t, d = x1.shape
    n = b * t
    tm = _pick(t, (512, 256))
    rec = route.reshape(n, LANES)
    rank, counts = _rank(rec, n_experts)

    cnt = counts[0, :n_experts].astype(jnp.int32)
    padded = ((cnt + EXPERT_TILE - 1) // EXPERT_TILE) * EXPERT_TILE
    ends = jnp.cumsum(padded)
    offs = ends - padded
    n_tiles = (2 * n) // EXPERT_TILE + n_experts
    n_used = (ends[-1] // EXPERT_TILE).astype(jnp.int32).reshape(1)
    tile_start = jnp.arange(n_tiles, dtype=jnp.int32) * EXPERT_TILE
    tile_expert = jnp.minimum(jnp.sum(tile_start[:, None] >= ends[None, :], axis=1), n_experts - 1)
    e1 = rec[:, n_experts].astype(jnp.int32)
    e2 = rec[:, n_experts + 1].astype(jnp.int32)
    d1 = jnp.take(offs, e1) + rank[:, 0]
    d2 = jnp.take(offs, e2) + rank[:, 1]
    dest = jnp.concatenate([d1.reshape(n // tm, tm), d2.reshape(n // tm, tm)], axis=1)[:, None, :]

    xs = _dispatch(dest, h2.reshape(n, d), n_tiles * EXPERT_TILE)
    os = _experts(tile_expert.astype(jnp.int32), n_used, xs, wg, wu, wd)
    return _combine(dest, os, route, x1, g2, n_experts)


def _rope_tables(n_tokens):
    rows = n_tokens // GRID_W
    row, col = jnp.meshgrid(jnp.arange(rows, dtype=F32), jnp.arange(GRID_W, dtype=F32), indexing="ij")
    half = HEAD_DIM // 2
    inv_freq = ROPE_THETA ** (-jnp.arange(0, half, 2, dtype=F32) / half)
    ang = jnp.concatenate([row.reshape(-1, 1) * inv_freq, col.reshape(-1, 1) * inv_freq], axis=-1)
    cos = jnp.repeat(jnp.cos(ang), 2, axis=-1)
    sign = jnp.tile(jnp.array([-1.0, 1.0], F32), HEAD_DIM // 2)
    sin = jnp.repeat(jnp.sin(ang), 2, axis=-1) * sign
    reps = LANES // HEAD_DIM
    return jnp.tile(cos, (1, reps)), jnp.tile(sin, (1, reps))


def kernel(x, c, ctx, c_ctx, ada_w, ada_b, norm1_g, norm2_g, attn_w_in, attn_w_out, qnorm_a, knorm_a, qnorm_b, knorm_b, sink_b, ffn_w_gate, ffn_w_up, ffn_w_down, conv_w_in, conv_w, conv_w_out, router_w, moe_w_gate, moe_w_up, moe_w_down):
    bsz, n_tok, d = x.shape
    n_ctx = ctx.shape[1]
    depth = ada_w.shape[0]
    n_experts = router_w.shape[-1]
    assert n_ctx == CHUNK and n_tok % CHUNK == 0 and bsz + 1 <= 8
    assert attn_w_in.shape[-1] == IN_PROJ_WIDTH and d % LANES == 0 and n_experts + 4 <= LANES

    cc = jnp.zeros((8, d), F32).at[:bsz].set(c).at[bsz].set(c_ctx)
    mod = _modulation(cc, ada_w, ada_b)

    def mod_vec(layer, j, is_ctx):
        m = mod[layer, :, j * d:(j + 1) * d]
        return m[bsz:bsz + 1].reshape(1, 1, d) if is_ctx else m[:bsz].reshape(bsz, 1, d)

    cos_l, sin_l = _rope_tables(n_tok)
    cos_c = jnp.ones((n_ctx, LANES), F32)
    sin_c = jnp.zeros((n_ctx, LANES), F32)
    head_id = jnp.arange(LANES) // HEAD_DIM
    bd = (head_id[:, None] == head_id[None, :]).astype(BF16)
    no_sink = jnp.zeros((N_HEADS,), F32)
    tile2 = lambda v: jnp.tile(v, LANES // HEAD_DIM)

    xc = ctx
    for layer in range(depth):
        i = layer // 2
        ctx_needed = any(j % 2 == 0 for j in range(layer + 1, depth))
        n1 = norm1_g[layer].reshape(1, d)
        n2 = norm2_g[layer].reshape(1, d)
        mv = lambda j, is_ctx: mod_vec(layer, j, is_ctx)

        if layer % 2 == 0:
            w_in = attn_w_in[i].astype(BF16)
            w_out = attn_w_out[i].astype(BF16)
            qscale = ATTN_SCALE * LOG2E
            gains = jnp.stack([tile2(qnorm_a[i]) * qscale, tile2(knorm_a[i]),
                               tile2(qnorm_b[i]) * qscale, tile2(knorm_b[i])]).astype(F32)
            sink = sink_b[i].astype(F32)
            def logit_bound(gq, gk):
                raw = HEAD_DIM * jnp.max(jnp.abs(gq)) * jnp.max(jnp.abs(gk)) * qscale * 1.02
                return raw.astype(BF16).astype(F32)
            bound_a = logit_bound(qnorm_a[i], knorm_a[i])
            bound_b = jnp.maximum(logit_bound(qnorm_b[i], knorm_b[i]),
                                  (jnp.max(sink) * LOG2E * 1.02).astype(BF16).astype(F32))
            fast_a = bound_a <= FAST_MAX_LOGIT
            fast_b = jnp.logical_and(bound_b <= FAST_MAX_LOGIT,
                                     jnp.max(jnp.abs(sink)) * LOG2E <= FAST_MAX_LOGIT)
            bound_a = bound_a.reshape(1)
            bound_b = bound_b.reshape(1)
            qa_l, ka_l, va_l, qb_l, kb_l, vb_l = _attn_in(
                x, n1, mv(1, False), mv(0, False), w_in, gains, cos_l, sin_l, bd, rope=True)
            qa_c, ka_c, va_c, qb_c, kb_c, vb_c = _attn_in(
                xc, n1, mv(1, True), mv(0, True), w_in, gains, cos_c, sin_c, bd, rope=False)
            ka_all = jnp.concatenate([ka_c, ka_l], axis=1)
            va_all = jnp.concatenate([va_c, va_l], axis=1)
            kb_all = jnp.concatenate([kb_c, kb_l], axis=1)
            vb_all = jnp.concatenate([vb_c, vb_l], axis=1)
            oa = _attend(fast_a, qa_l, ka_all, va_all, no_sink, bound_a, mode="global", use_sink=False)
            ob = _attend(fast_b, qb_l, kb_all, vb_all, sink, bound_b, mode="window", use_sink=True)
            x, h2 = _attn_out(oa, ob, w_out, x, mv(2, False), n2, mv(4, False), mv(3, False))
            if ctx_needed:
                oa_c = _attend(fast_a, qa_c, ka_c, va_c, no_sink, bound_a, mode="ctx", use_sink=False)
                ob_c = _attend(fast_b, qb_c, kb_c, vb_c, sink, bound_b, mode="ctx", use_sink=True)
                xc, h2c = _attn_out(oa_c, ob_c, w_out, xc, mv(2, True), n2, mv(4, True), mv(3, True))
            wg = ffn_w_gate[i].astype(BF16)[None]
            wu = ffn_w_up[i].astype(BF16)[None]
            wd = ffn_w_down[i].astype(BF16)[None]
            x = _ffn(h2, wg, wu, wd, None, x, mv(5, False))
            if ctx_needed:
                xc = _ffn(h2c, wg, wu, wd, None, xc, mv(5, True))
        else:
            w_in = conv_w_in[i].astype(BF16)
            w_out = conv_w_out[i].astype(BF16)
            cw = conv_w[i].astype(F32)
            router_pad = jnp.zeros((d, LANES), F32).at[:, :n_experts].set(router_w[i])
            wg = moe_w_gate[i].astype(BF16)
            wu = moe_w_up[i].astype(BF16)
            wd = moe_w_down[i].astype(BF16)
            bg, z = _conv_in(x, n1, mv(1, False), mv(0, False), w_in)
            x, h2, route = _conv_out(z, bg, cw, w_out, x, mv(2, False), n2, mv(4, False), mv(3, False),
                                     router_pad, n_experts, F32)
            x = _moe_routed(h2, route, x, mv(5, False), wg, wu, wd, n_experts)
            if ctx_needed:
                bg, z = _conv_in(xc, n1, mv(1, True), mv(0, True), w_in)
                xc, h2c, gates_c = _conv_out(z, bg, cw, w_out, xc, mv(2, True), n2, mv(4, True),
                                             mv(3, True), router_pad, n_experts, BF16)
                xc = _ffn(h2c, wg, wu, wd, gates_c, xc, mv(5, True))
    return x
```

```python
import functools

import jax
import jax.numpy as jnp
from jax import lax
from jax.experimental import pallas as pl
from jax.experimental.pallas import tpu as pltpu

F32 = jnp.float32
BF16 = jnp.bfloat16

HEAD_DIM = 64
N_KV = 2
GROUP = 4
N_HEADS = N_KV * GROUP
Q_WIDTH = N_HEADS * HEAD_DIM
KV_WIDTH = N_KV * HEAD_DIM
IN_PROJ_WIDTH = 2 * (Q_WIDTH + 2 * KV_WIDTH)
GRID_W = 64
WINDOW = 128
ROPE_THETA = 10000.0
ATTN_SCALE = HEAD_DIM ** -0.5
EPS = 1e-6
N_ADA = 6
LOG2E = 1.4426950408889634
LANES = 128
CHUNK = 256
K_AUG = 2 * KV_WIDTH
V_ROWS = HEAD_DIM + 16
GLOBAL_STEP = 3
FAST_MAX_LOGIT = 40.0
NEG_BIG = -1e30
VMEM_LIMIT_BYTES = 56 * 1024 * 1024


def _params(*sem):
    return pltpu.CompilerParams(dimension_semantics=sem, vmem_limit_bytes=VMEM_LIMIT_BYTES)


def _pick(n, candidates):
    for t in candidates:
        if n % t == 0:
            return t
    return n


def _sigmoid(a):
    return 1.0 / (1.0 + jnp.exp(-a))


def _norm_mod(x, g, sc, sh):
    ms = jnp.mean(x * x, axis=-1, keepdims=True)
    return x * lax.rsqrt(ms + EPS) * g * (1.0 + sc) + sh


def _mod_kernel(c_ref, w_ref, b_ref, o_ref):
    c = c_ref[...]
    s = c * _sigmoid(c)
    o_ref[...] = jnp.dot(s, w_ref[...], preferred_element_type=F32,
                         precision=lax.Precision.HIGHEST) + b_ref[...]


def _modulation(cc, ada_w, ada_b):
    depth, d, n = ada_w.shape
    tn = _pick(n, (1536, 1024, 512))
    rows = cc.shape[0]
    return pl.pallas_call(
        _mod_kernel,
        grid=(depth, n // tn),
        in_specs=[
            pl.BlockSpec((rows, d), lambda l, j: (0, 0)),
            pl.BlockSpec((None, d, tn), lambda l, j: (l, 0, j)),
            pl.BlockSpec((None, 1, tn), lambda l, j: (l, 0, j)),
        ],
        out_specs=pl.BlockSpec((None, rows, tn), lambda l, j: (l, 0, j)),
        out_shape=jax.ShapeDtypeStruct((depth, rows, n), F32),
        compiler_params=_params("arbitrary", "arbitrary"),
        name="modulation",
    )(cc, ada_w, ada_b.reshape(depth, 1, n))


def _vec_spec(arr):
    d = arr.shape[-1]
    if arr.shape[0] == 1:
        return pl.BlockSpec((None, 1, d), lambda b, i, *_: (0, 0, 0))
    return pl.BlockSpec((None, 1, d), lambda b, i, *_: (b, 0, 0))


def _attn_in_kernel(x_ref, g_ref, sc_ref, sh_ref, w_ref, gains_ref, cos_ref, sin_ref, bd_ref,
                    qa_ref, ka_ref, va_ref, qb_ref, kb_ref, vb_ref, *, rope, tm):
    h = _norm_mod(x_ref[...], g_ref[...], sc_ref[...], sh_ref[...]).astype(BF16)
    y = jnp.dot(h, w_ref[...], preferred_element_type=F32)
    lane = lax.broadcasted_iota(jnp.int32, (tm, LANES), 1)
    even = (lane & 1) == 0
    bd = bd_ref[...]

    def head_norm(yc, gain):
        ss = jnp.dot((yc * yc).astype(BF16), bd, preferred_element_type=F32)
        t = yc * lax.rsqrt(ss * (1.0 / HEAD_DIM) + EPS) * gain
        if rope:
            partner = jnp.where(even, pltpu.roll(t, LANES - 1, 1), pltpu.roll(t, 1, 1))
            t = t * cos_ref[...] + partner * sin_ref[...]
        return t

    groups = ((0, qa_ref, ka_ref, va_ref, 0), (Q_WIDTH + 2 * KV_WIDTH, qb_ref, kb_ref, vb_ref, 2))
    for col0, q_ref, k_ref, v_ref, grow in groups:
        gq = gains_ref[grow:grow + 1, :]
        gk = gains_ref[grow + 1:grow + 2, :]
        for c in range(Q_WIDTH // LANES):
            t = head_norm(y[:, col0 + c * LANES:col0 + (c + 1) * LANES], gq)
            q_ref[c * LANES:(c + 1) * LANES, :] = t.T.astype(BF16)
        kcol = col0 + Q_WIDTH
        k_ref[:, 0:KV_WIDTH] = head_norm(y[:, kcol:kcol + KV_WIDTH], gk).astype(BF16)
        k_ref[:, KV_WIDTH:K_AUG] = jnp.where(lane == 0, 1.0, 0.0).astype(BF16)
        vt = y[:, kcol + KV_WIDTH:kcol + 2 * KV_WIDTH].T.astype(BF16)
        ones_rows = jnp.where(lax.broadcasted_iota(jnp.int32, (V_ROWS - HEAD_DIM, CHUNK), 0) == 0,
                              1.0, 0.0).astype(BF16)
        for j in range(tm // CHUNK):
            for kv in range(N_KV):
                r0 = kv * V_ROWS
                v_ref[j, r0:r0 + HEAD_DIM, :] = vt[kv * HEAD_DIM:(kv + 1) * HEAD_DIM,
                                                   j * CHUNK:(j + 1) * CHUNK]
                v_ref[j, r0 + HEAD_DIM:r0 + V_ROWS, :] = ones_rows


def _attn_in(x, g, sc, sh, w_bf16, gains, cos_t, sin_t, bd, *, rope):
    b, t, d = x.shape
    tm = _pick(t, (512, 256))
    nch = t // CHUNK
    kern = functools.partial(_attn_in_kernel, rope=rope, tm=tm)
    q_spec = pl.BlockSpec((None, Q_WIDTH, tm), lambda bb, i: (bb, 0, i))
    k_spec = pl.BlockSpec((None, tm, K_AUG), lambda bb, i: (bb, i, 0))
    v_spec = pl.BlockSpec((None, tm // CHUNK, N_KV * V_ROWS, CHUNK), lambda bb, i: (bb, i, 0, 0))
    q_shape = jax.ShapeDtypeStruct((b, Q_WIDTH, t), BF16)
    k_shape = jax.ShapeDtypeStruct((b, t, K_AUG), BF16)
    v_shape = jax.ShapeDtypeStruct((b, nch, N_KV * V_ROWS, CHUNK), BF16)
    return pl.pallas_call(
        kern,
        grid=(b, t // tm),
        in_specs=[
            pl.BlockSpec((None, tm, d), lambda bb, i: (bb, i, 0)),
            pl.BlockSpec((1, d), lambda bb, i: (0, 0)),
            _vec_spec(sc), _vec_spec(sh),
            pl.BlockSpec((d, IN_PROJ_WIDTH), lambda bb, i: (0, 0)),
            pl.BlockSpec((4, LANES), lambda bb, i: (0, 0)),
            pl.BlockSpec((tm, LANES), lambda bb, i: (i, 0)),
            pl.BlockSpec((tm, LANES), lambda bb, i: (i, 0)),
            pl.BlockSpec((LANES, LANES), lambda bb, i: (0, 0)),
        ],
        out_specs=[q_spec, k_spec, v_spec, q_spec, k_spec, v_spec],
        out_shape=[q_shape, k_shape, v_shape, q_shape, k_shape, v_shape],
        compiler_params=_params("parallel", "parallel"),
        name="attn_in",
    )(x, g, sc, sh, w_bf16, gains, cos_t, sin_t, bd)


def _attn_kernel(q_ref, k_ref, v_ref, sink_ref, o_ref, qpad_ref, s_ref, cmax_ref, m_ref, l_ref,
                 acc_ref, *, mode, use_sink, n_lat, tq):
    i = pl.program_id(1)
    w = GROUP * tq
    zeros = jnp.zeros((HEAD_DIM, w), BF16)
    for kv in range(N_KV):
        heads = range(kv * GROUP, (kv + 1) * GROUP)
        q4 = jnp.concatenate([q_ref[h * HEAD_DIM:(h + 1) * HEAD_DIM, :] for h in heads], axis=1)
        qpad_ref[kv] = jnp.concatenate([q4, zeros] if kv == 0 else [zeros, q4], axis=0)
        if use_sink:
            m_ref[kv] = jnp.concatenate(
                [jnp.full((1, tq), sink_ref[h] * LOG2E, F32) for h in heads], axis=1)
            l_ref[kv] = jnp.ones((1, w), F32)
        else:
            m_ref[kv] = jnp.full((1, w), NEG_BIG, F32)
            l_ref[kv] = jnp.zeros((1, w), F32)
        acc_ref[kv] = jnp.zeros((HEAD_DIM, w), F32)

    def scores(c, slot, mask):
        start = c * CHUNK if isinstance(c, int) else pl.multiple_of(c * CHUNK, CHUNK)
        kc = k_ref[pl.ds(start, CHUNK), 0:KV_WIDTH]
        for kv in range(N_KV):
            s = jnp.dot(kc, qpad_ref[kv], preferred_element_type=F32)
            if mask is not None:
                s = jnp.where(mask, s, -jnp.inf)
            s_ref[slot, kv] = s
            cmax_ref[slot, kv] = jnp.max(s, axis=0, keepdims=True)

    def absorb(c, slot):
        for kv in range(N_KV):
            m = m_ref[kv]
            m_new = jnp.maximum(m, cmax_ref[slot, kv])
            alpha = jnp.exp2(m - m_new)
            p = jnp.exp2(s_ref[slot, kv] - m_new)
            l_ref[kv] = alpha * l_ref[kv] + jnp.sum(p, axis=0, keepdims=True)
            m_ref[kv] = m_new
            vc = v_ref[c, kv * V_ROWS:kv * V_ROWS + HEAD_DIM, :]
            acc_ref[kv] = alpha * acc_ref[kv] + jnp.dot(vc, p.astype(BF16),
                                                        preferred_element_type=F32)

    scores(0, 0, None)
    if mode == "global":
        def body(j, carry):
            c = 2 * j
            scores(c + 1, 1, None)
            absorb(c, 0)
            scores(c + 2, 0, None)
            absorb(c + 1, 1)
            return carry
        lax.fori_loop(0, n_lat // 2, body, 0)
        c_end = 2 * (n_lat // 2)
        if n_lat % 2:
            scores(c_end + 1, 1, None)
            absorb(c_end, 0)
            absorb(c_end + 1, 1)
        else:
            absorb(c_end, 0)
    elif mode == "window":
        row = lax.broadcasted_iota(jnp.int32, (CHUNK, w), 0)
        col = lax.broadcasted_iota(jnp.int32, (CHUNK, w), 1) & (tq - 1)
        rel0 = col - row
        prev_c, prev_slot = 0, 0
        for d in (-1, 0, 1):
            cl = i + d
            ok = jnp.logical_and(cl >= 0, cl < n_lat)
            c = 1 + jnp.clip(cl, 0, n_lat - 1)
            mask = jnp.logical_and(jnp.abs(rel0 - d * CHUNK) <= WINDOW, ok)
            scores(c, 1 - prev_slot, mask)
            absorb(prev_c, prev_slot)
            prev_c, prev_slot = c, 1 - prev_slot
        absorb(prev_c, prev_slot)
    else:
        absorb(0, 0)

    for kv in range(N_KV):
        out = acc_ref[kv] * (1.0 / l_ref[kv])
        for g in range(GROUP):
            h = kv * GROUP + g
            o_ref[h * HEAD_DIM:(h + 1) * HEAD_DIM, :] = out[:, g * tq:(g + 1) * tq].astype(BF16)


def _attention(q_t, k_all, v_all, sink, bound, *, mode, use_sink):
    del bound
    b, _, tq_total = q_t.shape
    nk = k_all.shape[1]
    nc = v_all.shape[1]
    tq = CHUNK
    kern = functools.partial(_attn_kernel, mode=mode, use_sink=use_sink, n_lat=nc - 1, tq=tq)
    return pl.pallas_call(
        kern,
        grid=(b, tq_total // tq),
        in_specs=[
            pl.BlockSpec((None, Q_WIDTH, tq), lambda bb, i: (bb, 0, i)),
            pl.BlockSpec((None, nk, K_AUG), lambda bb, i: (bb, 0, 0)),
            pl.BlockSpec((None, nc, N_KV * V_ROWS, CHUNK), lambda bb, i: (bb, 0, 0, 0)),
            pl.BlockSpec(memory_space=pltpu.SMEM),
        ],
        out_specs=pl.BlockSpec((None, Q_WIDTH, tq), lambda bb, i: (bb, 0, i)),
        out_shape=jax.ShapeDtypeStruct((b, Q_WIDTH, tq_total), BF16),
        scratch_shapes=[pltpu.VMEM((N_KV, KV_WIDTH, GROUP * tq), BF16),
                        pltpu.VMEM((2, N_KV, CHUNK, GROUP * tq), F32),
                        pltpu.VMEM((2, N_KV, 1, GROUP * tq), F32),
                        pltpu.VMEM((N_KV, 1, GROUP * tq), F32),
                        pltpu.VMEM((N_KV, 1, GROUP * tq), F32),
                        pltpu.VMEM((N_KV, HEAD_DIM, GROUP * tq), F32)],
        compiler_params=_params("parallel", "parallel"),
        name="attn_" + mode,
    )(q_t, k_all, v_all, sink)


def _attn_fast_kernel(q_ref, k_ref, v_ref, sink_ref, bound_ref, o_ref, qa_ref, acc_ref, *,
                      mode, use_sink, n_lat, tq):
    i = pl.program_id(1)
    w = GROUP * tq
    bound = bound_ref[0]
    zeros = jnp.zeros((HEAD_DIM, w), BF16)
    row = lax.broadcasted_iota(jnp.int32, (KV_WIDTH, w), 0)
    shift = jnp.where(row == 0, -bound, 0.0).astype(BF16)
    for kv in range(N_KV):
        heads = range(kv * GROUP, (kv + 1) * GROUP)
        q4 = jnp.concatenate([q_ref[h * HEAD_DIM:(h + 1) * HEAD_DIM, :] for h in heads], axis=1)
        qa_ref[kv] = jnp.concatenate(([q4, zeros] if kv == 0 else [zeros, q4]) + [shift], axis=0)
        acc_ref[kv] = jnp.zeros((V_ROWS, w), F32)

    def update(c, mask, n=1):
        start = c * CHUNK if isinstance(c, int) else pl.multiple_of(c * CHUNK, CHUNK)
        kc = k_ref[pl.ds(start, n * CHUNK), :]
        for kv in range(N_KV):
            s = jnp.dot(kc, qa_ref[kv], preferred_element_type=F32)
            if mask is not None:
                s = jnp.where(mask, s, -jnp.inf)
            p = jnp.exp2(s).astype(BF16)
            vc = jnp.concatenate([v_ref[c + j, kv * V_ROWS:(kv + 1) * V_ROWS, :] for j in range(n)],
                                 axis=1)
            acc_ref[kv] += jnp.dot(vc, p, preferred_element_type=F32)

    if mode == "global":
        n_steps = (1 + n_lat) // GLOBAL_STEP

        def body(j, carry):
            update(GLOBAL_STEP * j, None, GLOBAL_STEP)
            return carry
        lax.fori_loop(0, n_steps, body, 0)
        for c in range(GLOBAL_STEP * n_steps, 1 + n_lat):
            update(c, None)
    else:
        update(0, None)
    if mode == "window":
        rowk =lax.broadcasted_iota(jnp.int32, (CHUNK, w), 0)
        col = lax.broadcasted_iota(jnp.int32, (CHUNK, w), 1) & (tq - 1)
        rel0 = col - rowk
        for d in (-1, 0, 1):
            cl = i + d
            ok = jnp.logical_and(cl >= 0, cl < n_lat)
            c = 1 + jnp.clip(cl, 0, n_lat - 1)
            update(c, jnp.logical_and(jnp.abs(rel0 - d * CHUNK) <= WINDOW, ok))

    for kv in range(N_KV):
        acc = acc_ref[kv]
        l = acc[HEAD_DIM:HEAD_DIM + 1, :]
        if use_sink:
            l = l + jnp.concatenate(
                [jnp.full((1, tq), jnp.exp2(sink_ref[kv * GROUP + g] * LOG2E - bound), F32)
                 for g in range(GROUP)], axis=1)
        out = acc[0:HEAD_DIM, :] * (1.0 / l)
        for g in range(GROUP):
            h = kv * GROUP + g
            o_ref[h * HEAD_DIM:(h + 1) * HEAD_DIM, :] = out[:, g * tq:(g + 1) * tq].astype(BF16)


def _attention_fast(q_t, k_all, v_all, sink, bound, *, mode, use_sink):
    b, _, tq_total = q_t.shape
    nk = k_all.shape[1]
    nc = v_all.shape[1]
    tq = CHUNK
    kern = functools.partial(_attn_fast_kernel, mode=mode, use_sink=use_sink, n_lat=nc - 1, tq=tq)
    return pl.pallas_call(
        kern,
        grid=(b, tq_total // tq),
        in_specs=[
            pl.BlockSpec((None, Q_WIDTH, tq), lambda bb, i: (bb, 0, i)),
            pl.BlockSpec((None, nk, K_AUG), lambda bb, i: (bb, 0, 0)),
            pl.BlockSpec((None, nc, N_KV * V_ROWS, CHUNK), lambda bb, i: (bb, 0, 0, 0)),
            pl.BlockSpec(memory_space=pltpu.SMEM),
            pl.BlockSpec(memory_space=pltpu.SMEM),
        ],
        out_specs=pl.BlockSpec((None, Q_WIDTH, tq), lambda bb, i: (bb, 0, i)),
        out_shape=jax.ShapeDtypeStruct((b, Q_WIDTH, tq_total), BF16),
        scratch_shapes=[pltpu.VMEM((N_KV, K_AUG, GROUP * tq), BF16),
                        pltpu.VMEM((N_KV, V_ROWS, GROUP * tq), F32)],
        compiler_params=_params("parallel", "parallel"),
        name="attn_fast_" + mode,
    )(q_t, k_all, v_all, sink, bound)


def _attend(fast_ok, *args, **kw):
    return lax.cond(fast_ok, functools.partial(_attention_fast, **kw),
                    functools.partial(_attention, **kw), *args)


_TN_DIMS = (((0,), (0,)), ((), ()))


def _attn_out_kernel(oa_ref, ob_ref, w_ref, x_ref, g1_ref, n2_ref, sc2_ref, sh2_ref, x1_ref, h2_ref):
    mix = lax.dot_general(oa_ref[...], w_ref[0:Q_WIDTH, :], _TN_DIMS, preferred_element_type=F32)
    mix = mix + lax.dot_general(ob_ref[...], w_ref[Q_WIDTH:2 * Q_WIDTH, :], _TN_DIMS,
                                preferred_element_type=F32)
    x1 = x_ref[...] + g1_ref[...] * mix
    x1_ref[...] = x1
    h2_ref[...] = _norm_mod(x1, n2_ref[...], sc2_ref[...], sh2_ref[...]).astype(BF16)


def _attn_out(oa, ob, w_bf16, x, g1, n2, sc2, sh2):
    b, t, d = x.shape
    tm = _pick(t, (512, 256))
    o_spec = pl.BlockSpec((None, Q_WIDTH, tm), lambda bb, i: (bb, 0, i))
    x_spec = pl.BlockSpec((None, tm, d), lambda bb, i: (bb, i, 0))
    return pl.pallas_call(
        _attn_out_kernel,
        grid=(b, t // tm),
        in_specs=[o_spec, o_spec,
                  pl.BlockSpec((2 * Q_WIDTH, d), lambda bb, i: (0, 0)),
                  x_spec, _vec_spec(g1),
                  pl.BlockSpec((1, d), lambda bb, i: (0, 0)),
                  _vec_spec(sc2), _vec_spec(sh2)],
        out_specs=[x_spec, x_spec],
        out_shape=[jax.ShapeDtypeStruct((b, t, d), F32), jax.ShapeDtypeStruct((b, t, d), BF16)],
        compiler_params=_params("parallel", "parallel"),
        name="attn_out",
    )(oa, ob, w_bf16, x, g1, n2, sc2, sh2)


def _ffn_kernel(*refs, gated, tm):
    if gated:
        h_ref, wg_ref, wu_ref, wd_ref, gate_ref, x_ref, g2_ref, o_ref, acc_ref = refs
    else:
        h_ref, wg_ref, wu_ref, wd_ref, x_ref, g2_ref, o_ref, acc_ref = refs
    e = pl.program_id(2)
    f = pl.program_id(3)
    first = jnp.logical_and(e == 0, f == 0)
    last = jnp.logical_and(e == pl.num_programs(2) - 1, f == pl.num_programs(3) - 1)

    @pl.when(first)
    def _():
        acc_ref[...] = jnp.zeros_like(acc_ref)

    h = h_ref[...]
    a = jnp.dot(h, wg_ref[...], preferred_element_type=F32)
    u = jnp.dot(h, wu_ref[...], preferred_element_type=F32)
    act = (a * _sigmoid(a) * u).astype(BF16)
    y = jnp.dot(act, wd_ref[...], preferred_element_type=F32)
    if gated:
        lane = lax.broadcasted_iota(jnp.int32, (tm, LANES), 1)
        ge = jnp.sum(jnp.where(lane == e, gate_ref[...], 0.0), axis=-1, keepdims=True)
        y = y * ge
    acc_ref[...] += y

    @pl.when(last)
    def _():
        o_ref[...] = x_ref[...] + g2_ref[...] * acc_ref[...]


def _ffn(h2, wg, wu, wd, gates, x1, g2):
    b, t, d = x1.shape
    n_e, _, f = wg.shape
    tm = _pick(t, (512, 256))
    gated = gates is not None
    resident = n_e == 1 and 3 * d * f * 2 <= VMEM_LIMIT_BYTES // 3
    tf = f if resident else _pick(f, (1792, 1408, 1024, 512))
    mode = dict(pipeline_mode=pl.Buffered(1)) if resident else {}
    x_spec = pl.BlockSpec((None, tm, d), lambda bb, i, e, j: (bb, i, 0))
    in_specs = [x_spec,
                pl.BlockSpec((None, d, tf), lambda bb, i, e, j: (e, 0, j), **mode),
                pl.BlockSpec((None, d, tf), lambda bb, i, e, j: (e, 0, j), **mode),
                pl.BlockSpec((None, tf, d), lambda bb, i, e, j: (e, j, 0), **mode)]
    args = [h2, wg, wu, wd]
    if gated:
        in_specs.append(pl.BlockSpec((None, tm, LANES), lambda bb, i, e, j: (bb, i, 0)))
        args.append(gates)
    in_specs += [x_spec, _vec_spec(g2)]
    args += [x1, g2]
    return pl.pallas_call(
        functools.partial(_ffn_kernel, gated=gated, tm=tm),
        grid=(b, t // tm, n_e, f // tf),
        in_specs=in_specs,
        out_specs=x_spec,
        out_shape=jax.ShapeDtypeStruct((b, t, d), F32),
        scratch_shapes=[pltpu.VMEM((tm, d), F32)],
        compiler_params=_params("parallel", "parallel", "arbitrary", "arbitrary"),
        name="ffn_gated" if gated else "ffn_dense",
    )(*args)


def _conv_in_kernel(x_ref, g_ref, sc_ref, sh_ref, w_ref, bg_ref, z_ref, *, d):
    h = _norm_mod(x_ref[...], g_ref[...], sc_ref[...], sh_ref[...]).astype(BF16)
    y = jnp.dot(h, w_ref[...], preferred_element_type=F32)
    bg_ref[...] = y[:, 0:d].astype(BF16)
    z_ref[...] = (y[:, d:2 * d] * y[:, 2 * d:3 * d]).astype(BF16)


def _conv_in(x, g, sc, sh, w_bf16):
    b, t, d = x.shape
    tm = _pick(t, (512, 256))
    x_spec = pl.BlockSpec((None, tm, d), lambda bb, i: (bb, i, 0))
    return pl.pallas_call(
        functools.partial(_conv_in_kernel, d=d),
        grid=(b, t // tm),
        in_specs=[x_spec, pl.BlockSpec((1, d), lambda bb, i: (0, 0)), _vec_spec(sc), _vec_spec(sh),
                  pl.BlockSpec((d, 3 * d), lambda bb, i: (0, 0))],
        out_specs=[x_spec, x_spec],
        out_shape=[jax.ShapeDtypeStruct((b, t, d), BF16), jax.ShapeDtypeStruct((b, t, d), BF16)],
        compiler_params=_params("parallel", "parallel"),
        name="conv_in",
    )(x, g, sc, sh, w_bf16)


HALO = 16


def _conv_out_kernel(z_ref, zp_ref, zn_ref, bg_ref, cw_ref, w_ref, x_ref, g1_ref, n2_ref, sc2_ref,
                     sh2_ref, rw_ref, x1_ref, h2_ref, gate_ref, *, tm, n_experts):
    i = pl.program_id(1)
    z = z_ref[...].astype(F32)
    prev = jnp.where(i > 0, zp_ref[HALO - 1:HALO, :].astype(F32), 0.0)
    nxt = jnp.where(i < pl.num_programs(1) - 1, zn_ref[0:1, :].astype(F32), 0.0)
    row = lax.broadcasted_iota(jnp.int32, z.shape, 0)
    z_dn = jnp.where(row == 0, prev, pltpu.roll(z, 1, 0))
    z_up = jnp.where(row == tm - 1, nxt, pltpu.roll(z, tm - 1, 0))
    conv = z_dn * cw_ref[0:1, :] + z * cw_ref[1:2, :] + z_up * cw_ref[2:3, :]
    v = (bg_ref[...].astype(F32) * conv).astype(BF16)
    mix = jnp.dot(v, w_ref[...], preferred_element_type=F32)
    x1 = x_ref[...] + g1_ref[...] * mix
    x1_ref[...] = x1
    h2 = _norm_mod(x1, n2_ref[...], sc2_ref[...], sh2_ref[...])
    h2_ref[...] = h2.astype(h2_ref.dtype)

    h_hi = h2.astype(BF16)
    h_lo = (h2 - h_hi.astype(F32)).astype(BF16)
    logits = (jnp.dot(h_hi, rw_ref[0], preferred_element_type=F32)
              + jnp.dot(h_lo, rw_ref[0], preferred_element_type=F32)
              + jnp.dot(h_hi, rw_ref[1], preferred_element_type=F32))
    lane = lax.broadcasted_iota(jnp.int32, logits.shape, 1)
    lanef = lane.astype(F32)
    logits = jnp.where(lane < n_experts, logits, -jnp.inf)
    m1 = jnp.max(logits, axis=-1, keepdims=True)
    i1 = jnp.min(jnp.where(logits == m1, lanef, float(LANES)), axis=-1, keepdims=True)
    rest = jnp.where(lanef == i1, -jnp.inf, logits)
    m2 = jnp.max(rest, axis=-1, keepdims=True)
    i2 = jnp.min(jnp.where(rest == m2, lanef, float(LANES)), axis=-1, keepdims=True)
    e2 = jnp.exp(m2 - m1)
    w1 = 1.0 / (1.0 + e2)
    w2 = e2 * w1
    rec = jnp.where(lanef == i1, w1, 0.0) + jnp.where(lanef == i2, w2, 0.0)
    for k, val in enumerate((i1, i2, w1, w2)):
        rec = jnp.where(lane == n_experts + k, val, rec)
    gate_ref[...] = rec


def _conv_out(z, bg, conv_w, w_bf16, x, g1, n2, sc2, sh2, router_pad, n_experts, h2_dtype):
    b, t, d = x.shape
    tm = _pick(t, (512, 256))
    hb = tm // HALO
    n_halo = t // HALO
    x_spec = pl.BlockSpec((None, tm, d), lambda bb, i: (bb, i, 0))
    prev_spec = pl.BlockSpec((None, HALO, d), lambda bb, i: (bb, jnp.maximum(i * hb - 1, 0), 0))
    next_spec = pl.BlockSpec((None, HALO, d), lambda bb, i: (bb, jnp.minimum((i + 1) * hb, n_halo - 1), 0))
    full = lambda r, c: pl.BlockSpec((r, c), lambda bb, i: (0, 0))
    return pl.pallas_call(
        functools.partial(_conv_out_kernel, tm=tm, n_experts=n_experts),
        grid=(b, t // tm),
        in_specs=[x_spec, prev_spec, next_spec, x_spec, full(3, d), full(d, d), x_spec, _vec_spec(g1),
                  full(1, d), _vec_spec(sc2), _vec_spec(sh2),
                  pl.BlockSpec((2, d, LANES), lambda bb, i: (0, 0, 0))],
        out_specs=[x_spec, x_spec, pl.BlockSpec((None, tm, LANES), lambda bb, i: (bb, i, 0))],
        out_shape=[jax.ShapeDtypeStruct((b, t, d), F32), jax.ShapeDtypeStruct((b, t, d), h2_dtype),
                   jax.ShapeDtypeStruct((b, t, LANES), F32)],
        compiler_params=_params("parallel", "parallel"),
        name="conv_out",
    )(z, z, z, bg, conv_w, w_bf16, x, g1, n2, sc2, sh2, router_pad)


EXPERT_TILE = 1024


def _rank_kernel(route_ref, rank_ref, count_ref, base_ref, *, tm, n_experts):
    @pl.when(pl.program_id(0) == 0)
    def _():
        base_ref[...] = jnp.zeros_like(base_ref)

    route = route_ref[...]
    lane = lax.broadcasted_iota(jnp.int32, (tm, LANES), 1)
    lanef = lane.astype(F32)
    i1 = jnp.sum(jnp.where(lane == n_experts, route, 0.0), axis=-1, keepdims=True)
    i2 = jnp.sum(jnp.where(lane == n_experts + 1, route, 0.0), axis=-1, keepdims=True)
    onehot = jnp.where(lanef == i1, 1.0, 0.0) + jnp.where(lanef == i2, 1.0, 0.0)
    r = lax.broadcasted_iota(jnp.int32, (tm, tm), 0)
    c = lax.broadcasted_iota(jnp.int32, (tm, tm), 1)
    lower = jnp.where(c < r, 1.0, 0.0).astype(BF16)
    before = jnp.dot(lower, onehot.astype(BF16), preferred_element_type=F32) + base_ref[...]
    r1 = jnp.sum(jnp.where(lanef == i1, before, 0.0), axis=-1, keepdims=True)
    r2 = jnp.sum(jnp.where(lanef == i2, before, 0.0), axis=-1, keepdims=True)
    packed = jnp.where(lane == 0, r1, jnp.where(lane == 1, r2, jnp.where(lane == 2, i1,
                       jnp.where(lane == 3, i2, 0.0))))
    rank_ref[...] = packed.T[0:8, :]
    base_ref[...] += jnp.sum(onehot, axis=0, keepdims=True)
    count_ref[...] = base_ref[...]


def _rank(route, n_experts):
    n = route.shape[0]
    tm = _pick(n, (512, 256))
    return pl.pallas_call(
        functools.partial(_rank_kernel, tm=tm, n_experts=n_experts),
        grid=(n // tm,),
        in_specs=[pl.BlockSpec((tm, LANES), lambda i: (i, 0))],
        out_specs=[pl.BlockSpec((8, tm), lambda i: (0, i)), pl.BlockSpec((1, LANES), lambda i: (0, 0))],
        out_shape=[jax.ShapeDtypeStruct((8, n), F32), jax.ShapeDtypeStruct((1, LANES), F32)],
        scratch_shapes=[pltpu.VMEM((1, LANES), F32)],
        compiler_params=_params("arbitrary"),
        name="moe_rank",
    )(route)


def _dispatch_kernel(dest_ref, h_ref, xs_in_ref, xs_ref, sem, *, tm):
    del xs_in_ref

    def row_copy(r, k):
        d = dest_ref[0, k * tm + r]
        return pltpu.make_async_copy(h_ref.at[pl.ds(r, 1), :], xs_ref.at[pl.ds(d, 1), :], sem)

    def issue(r, carry):
        row_copy(r, 0).start()
        row_copy(r, 1).start()
        return carry

    lax.fori_loop(0, tm, issue, 0, unroll=8)

    def drain(r, carry):
        row_copy(r, 0).wait()
        row_copy(r, 1).wait()
        return carry

    lax.fori_loop(0, tm, drain, 0, unroll=8)


def _dispatch(dest, h2, n_rows):
    n, d = h2.shape
    tm = dest.shape[-1] // 2
    zeros = jnp.zeros((n_rows, d), h2.dtype)
    return pl.pallas_call(
        functools.partial(_dispatch_kernel, tm=tm),
        grid=(n // tm,),
        in_specs=[pl.BlockSpec((None, 1, 2 * tm), lambda i: (i, 0, 0), memory_space=pltpu.SMEM),
                  pl.BlockSpec((tm, d), lambda i: (i, 0)),
                  pl.BlockSpec(memory_space=pl.ANY)],
        out_specs=pl.BlockSpec(memory_space=pl.ANY),
        out_shape=jax.ShapeDtypeStruct((n_rows, d), h2.dtype),
        scratch_shapes=[pltpu.SemaphoreType.DMA(())],
        input_output_aliases={2: 0},
        compiler_params=_params("arbitrary"),
        name="moe_dispatch",
    )(dest, h2, zeros)


def _expert_kernel(te_ref, nu_ref, xs_ref, wg_ref, wu_ref, wd_ref, o_ref, hb_ref, acc_ref):
    del te_ref
    j = pl.program_id(0)
    f = pl.program_id(1)

    @pl.when(j < nu_ref[0])
    def _():
        @pl.when(f == 0)
        def _():
            hb_ref[...] = xs_ref[...].astype(BF16)
            acc_ref[...] = jnp.zeros_like(acc_ref)

        h = hb_ref[...]
        a = jnp.dot(h, wg_ref[...], preferred_element_type=F32)
        u = jnp.dot(h, wu_ref[...], preferred_element_type=F32)
        act = (a * _sigmoid(a) * u).astype(BF16)
        acc_ref[...] += jnp.dot(act, wd_ref[...], preferred_element_type=F32)

        @pl.when(f == pl.num_programs(1) - 1)
        def _():
            o_ref[...] = acc_ref[...]

    @pl.when(jnp.logical_and(j >= nu_ref[0], f == pl.num_programs(1) - 1))
    def _():
        o_ref[...] = jnp.zeros_like(o_ref)


def _experts(tile_expert, n_used, xs, wg, wu, wd):
    n_rows, d = xs.shape
    f = wg.shape[-1]
    tf = _pick(f, (512, 256))
    nf = f // tf
    n_tiles = n_rows // EXPERT_TILE

    def row_map(j, ff, te, nu):
        return (jnp.minimum(j, nu[0] - 1), 0)

    def w_up_map(j, ff, te, nu):
        live = j < nu[0]
        return (te[jnp.minimum(j, nu[0] - 1)], 0, jnp.where(live, ff, nf - 1))

    def w_down_map(j, ff, te, nu):
        live = j < nu[0]
        return (te[jnp.minimum(j, nu[0] - 1)], jnp.where(live, ff, nf - 1), 0)

    grid_spec = pltpu.PrefetchScalarGridSpec(
        num_scalar_prefetch=2,
        grid=(n_tiles, nf),
        in_specs=[pl.BlockSpec((EXPERT_TILE, d), row_map),
                  pl.BlockSpec((None, d, tf), w_up_map),
                  pl.BlockSpec((None, d, tf), w_up_map),
                  pl.BlockSpec((None, tf, d), w_down_map)],
        out_specs=pl.BlockSpec((EXPERT_TILE, d), lambda j, ff, te, nu: (j, 0)),
        scratch_shapes=[pltpu.VMEM((EXPERT_TILE, d), BF16), pltpu.VMEM((EXPERT_TILE, d), F32)],
    )
    return pl.pallas_call(
        _expert_kernel,
        grid_spec=grid_spec,
        out_shape=jax.ShapeDtypeStruct((n_rows, d), F32),
        compiler_params=_params("arbitrary", "arbitrary"),
        name="moe_experts",
    )(tile_expert, n_used, xs, wg, wu, wd)


def _combine_kernel(dest_ref, os_ref, route_ref, x_ref, g2_ref, o_ref, buf_ref, sem, *, tm, n_experts):
    def row_copy(r, k):
        d = dest_ref[0, k * tm + r]
        return pltpu.make_async_copy(os_ref.at[pl.ds(d, 1), :], buf_ref.at[k, pl.ds(r, 1), :], sem)

    def issue(r, carry):
        row_copy(r, 0).start()
        row_copy(r, 1).start()
        return carry

    lax.fori_loop(0, tm, issue, 0, unroll=8)

    def drain(r, carry):
        row_copy(r, 0).wait()
        row_copy(r, 1).wait()
        return carry

    lax.fori_loop(0, tm, drain, 0, unroll=8)

    route = route_ref[...]
    lane = lax.broadcasted_iota(jnp.int32, (tm, LANES), 1)
    w1 = jnp.sum(jnp.where(lane == n_experts + 2, route, 0.0), axis=-1, keepdims=True)
    w2 = jnp.sum(jnp.where(lane == n_experts + 3, route, 0.0), axis=-1, keepdims=True)
    o_ref[...] = x_ref[...] + g2_ref[...] * (w1 * buf_ref[0] + w2 * buf_ref[1])


def _combine(dest, os, route, x1, g2, n_experts):
    b, t, d = x1.shape
    tm = dest.shape[-1] // 2
    tiles_per_seq = t // tm
    x_spec = pl.BlockSpec((None, tm, d), lambda bb, i: (bb, i, 0))
    return pl.pallas_call(
        functools.partial(_combine_kernel, tm=tm, n_experts=n_experts),
        grid=(b, tiles_per_seq),
        in_specs=[pl.BlockSpec((None, 1, 2 * tm), lambda bb, i: (bb * tiles_per_seq + i, 0, 0),
                               memory_space=pltpu.SMEM),
                  pl.BlockSpec(memory_space=pl.ANY),
                  pl.BlockSpec((None, tm, LANES), lambda bb, i: (bb, i, 0)),
                  x_spec, _vec_spec(g2)],
        out_specs=x_spec,
        out_shape=jax.ShapeDtypeStruct((b, t, d), F32),
        scratch_shapes=[pltpu.VMEM((2, tm, d), F32), pltpu.SemaphoreType.DMA(())],
        compiler_params=_params("arbitrary", "arbitrary"),
        name="moe_combine",
    )(dest, os, route, x1, g2)


def _moe_routed(h2, route, x1, g2, wg, wu, wd, n_experts):
    b, t, d = x1.shape
    n = b * t
    tm = _pick(t, (512, 256))
    rec = route.reshape(n, LANES)
    rank, counts = _rank(rec, n_experts)

    cnt = counts[0, :n_experts].astype(jnp.int32)
    padded = ((cnt + EXPERT_TILE - 1) // EXPERT_TILE) * EXPERT_TILE
    ends = jnp.cumsum(padded)
    offs = ends - padded
    n_tiles = (2 * n) // EXPERT_TILE + n_experts
    n_used = (ends[-1] // EXPERT_TILE).astype(jnp.int32).reshape(1)
    tile_start = jnp.arange(n_tiles, dtype=jnp.int32) * EXPERT_TILE
    tile_expert = jnp.minimum(jnp.sum(tile_start[:, None] >= ends[None, :], axis=1), n_experts - 1)
    slot = rank.astype(jnp.int32)
    d1 = jnp.take(offs, slot[2]) + slot[0]
    d2 = jnp.take(offs, slot[3]) + slot[1]
    dest = jnp.concatenate([d1.reshape(n // tm, tm), d2.reshape(n // tm, tm)], axis=1)[:, None, :]

    xs = _dispatch(dest, h2.reshape(n, d), n_tiles * EXPERT_TILE)
    os = _experts(tile_expert.astype(jnp.int32), n_used, xs, wg, wu, wd)
    return _combine(dest, os, route, x1, g2, n_experts)


def _rope_tables(n_tokens):
    rows = n_tokens // GRID_W
    row, col = jnp.meshgrid(jnp.arange(rows, dtype=F32), jnp.arange(GRID_W, dtype=F32), indexing="ij")
    half = HEAD_DIM // 2
    inv_freq = ROPE_THETA ** (-jnp.arange(0, half, 2, dtype=F32) / half)
    ang = jnp.concatenate([row.reshape(-1, 1) * inv_freq, col.reshape(-1, 1) * inv_freq], axis=-1)
    cos = jnp.repeat(jnp.cos(ang), 2, axis=-1)
    sign = jnp.tile(jnp.array([-1.0, 1.0], F32), HEAD_DIM // 2)
    sin = jnp.repeat(jnp.sin(ang), 2, axis=-1) * sign
    reps = LANES // HEAD_DIM
    return jnp.tile(cos, (1, reps)), jnp.tile(sin, (1, reps))


def kernel(x, c, ctx, c_ctx, ada_w, ada_b, norm1_g, norm2_g, attn_w_in, attn_w_out, qnorm_a, knorm_a, qnorm_b, knorm_b, sink_b, ffn_w_gate, ffn_w_up, ffn_w_down, conv_w_in, conv_w, conv_w_out, router_w, moe_w_gate, moe_w_up, moe_w_down):
    bsz, n_tok, d = x.shape
    n_ctx = ctx.shape[1]
    depth = ada_w.shape[0]
    n_experts = router_w.shape[-1]
    assert n_ctx == CHUNK and n_tok % CHUNK == 0 and bsz + 1 <= 8
    assert attn_w_in.shape[-1] == IN_PROJ_WIDTH and d % LANES == 0 and n_experts + 4 <= LANES

    cc = jnp.zeros((8, d), F32).at[:bsz].set(c).at[bsz].set(c_ctx)
    mod = _modulation(cc, ada_w, ada_b)

    def mod_vec(layer, j, is_ctx):
        m = mod[layer, :, j * d:(j + 1) * d]
        return m[bsz:bsz + 1].reshape(1, 1, d) if is_ctx else m[:bsz].reshape(bsz, 1, d)

    cos_l, sin_l = _rope_tables(n_tok)
    cos_c = jnp.ones((n_ctx, LANES), F32)
    sin_c = jnp.zeros((n_ctx, LANES), F32)
    head_id = jnp.arange(LANES) // HEAD_DIM
    bd = (head_id[:, None] == head_id[None, :]).astype(BF16)
    no_sink = jnp.zeros((N_HEADS,), F32)
    tile2 = lambda v: jnp.tile(v, LANES // HEAD_DIM)

    xc = ctx
    for layer in range(depth):
        i = layer // 2
        ctx_needed = any(j % 2 == 0 for j in range(layer + 1, depth))
        n1 = norm1_g[layer].reshape(1, d)
        n2 = norm2_g[layer].reshape(1, d)
        mv = lambda j, is_ctx: mod_vec(layer, j, is_ctx)

        if layer % 2 == 0:
            w_in = attn_w_in[i].astype(BF16)
            w_out = attn_w_out[i].astype(BF16)
            qscale = ATTN_SCALE * LOG2E
            gains = jnp.stack([tile2(qnorm_a[i]) * qscale, tile2(knorm_a[i]),
                               tile2(qnorm_b[i]) * qscale, tile2(knorm_b[i])]).astype(F32)
            sink = sink_b[i].astype(F32)
            def logit_bound(gq, gk):
                raw = HEAD_DIM * jnp.max(jnp.abs(gq)) * jnp.max(jnp.abs(gk)) * qscale * 1.02
                return raw.astype(BF16).astype(F32)
            bound_a = logit_bound(qnorm_a[i], knorm_a[i])
            bound_b = jnp.maximum(logit_bound(qnorm_b[i], knorm_b[i]),
                                  (jnp.max(sink) * LOG2E * 1.02).astype(BF16).astype(F32))
            fast_a = bound_a <= FAST_MAX_LOGIT
            fast_b = jnp.logical_and(bound_b <= FAST_MAX_LOGIT,
                                     jnp.max(jnp.abs(sink)) * LOG2E <= FAST_MAX_LOGIT)
            bound_a = bound_a.reshape(1)
            bound_b = bound_b.reshape(1)
            qa_l, ka_l, va_l, qb_l, kb_l, vb_l = _attn_in(
                x, n1, mv(1, False), mv(0, False), w_in, gains, cos_l, sin_l, bd, rope=True)
            qa_c, ka_c, va_c, qb_c, kb_c, vb_c = _attn_in(
                xc, n1, mv(1, True), mv(0, True), w_in, gains, cos_c, sin_c, bd, rope=False)
            ka_all = jnp.concatenate([ka_c, ka_l], axis=1)
            va_all = jnp.concatenate([va_c, va_l], axis=1)
            kb_all = jnp.concatenate([kb_c, kb_l], axis=1)
            vb_all = jnp.concatenate([vb_c, vb_l], axis=1)
            oa = _attend(fast_a, qa_l, ka_all, va_all, no_sink, bound_a, mode="global", use_sink=False)
            ob = _attend(fast_b, qb_l, kb_all, vb_all, sink, bound_b, mode="window", use_sink=True)
            x, h2 = _attn_out(oa, ob, w_out, x, mv(2, False), n2, mv(4, False), mv(3, False))
            if ctx_needed:
                oa_c = _attend(fast_a, qa_c, ka_c, va_c, no_sink, bound_a, mode="ctx", use_sink=False)
                ob_c = _attend(fast_b, qb_c, kb_c, vb_c, sink, bound_b, mode="ctx", use_sink=True)
                xc, h2c = _attn_out(oa_c, ob_c, w_out, xc, mv(2, True), n2, mv(4, True), mv(3, True))
            wg = ffn_w_gate[i].astype(BF16)[None]
            wu = ffn_w_up[i].astype(BF16)[None]
            wd = ffn_w_down[i].astype(BF16)[None]
            x = _ffn(h2, wg, wu, wd, None, x, mv(5, False))
            if ctx_needed:
                xc = _ffn(h2c, wg, wu, wd, None, xc, mv(5, True))
        else:
            w_in = conv_w_in[i].astype(BF16)
            w_out = conv_w_out[i].astype(BF16)
            cw = conv_w[i].astype(F32)
            router_f32 = jnp.zeros((d, LANES), F32).at[:, :n_experts].set(router_w[i])
            router_hi = router_f32.astype(BF16)
            router_pad = jnp.stack([router_hi, (router_f32 - router_hi.astype(F32)).astype(BF16)])
            wg = moe_w_gate[i].astype(BF16)
            wu = moe_w_up[i].astype(BF16)
            wd = moe_w_down[i].astype(BF16)
            bg, z = _conv_in(x, n1, mv(1, False), mv(0, False), w_in)
            x, h2, route = _conv_out(z, bg, cw, w_out, x, mv(2, False), n2, mv(4, False), mv(3, False),
                                     router_pad, n_experts, F32)
            x = _moe_routed(h2, route, x, mv(5, False), wg, wu, wd, n_experts)
            if ctx_needed:
                bg, z = _conv_in(xc, n1, mv(1, True), mv(0, True), w_in)
                xc, h2c, gates_c = _conv_out(z, bg, cw, w_out, xc, mv(2, True), n2, mv(4, True),
                                             mv(3, True), router_pad, n_experts, BF16)
                xc = _ffn(h2c, wg, wu, wd, gates_c, xc, mv(5, True))
    return x
```

```python
import functools

import jax
import jax.numpy as jnp
from jax import lax
from jax.experimental import pallas as pl
from jax.experimental.pallas import tpu as pltpu

F32 = jnp.float32
BF16 = jnp.bfloat16

HEAD_DIM = 64
N_KV = 2
GROUP = 4
N_HEADS = N_KV * GROUP
Q_WIDTH = N_HEADS * HEAD_DIM
KV_WIDTH = N_KV * HEAD_DIM
IN_PROJ_WIDTH = 2 * (Q_WIDTH + 2 * KV_WIDTH)
GRID_W = 64
WINDOW = 128
ROPE_THETA = 10000.0
ATTN_SCALE = HEAD_DIM ** -0.5
EPS = 1e-6
N_ADA = 6
LOG2E = 1.4426950408889634
LANES = 128
CHUNK = 256
K_AUG = 2 * KV_WIDTH
V_ROWS = HEAD_DIM + 16
GLOBAL_STEP = 4
FAST_MAX_LOGIT = 40.0
NEG_BIG = -1e30
VMEM_LIMIT_BYTES = 56 * 1024 * 1024


def _params(*sem):
    return pltpu.CompilerParams(dimension_semantics=sem, vmem_limit_bytes=VMEM_LIMIT_BYTES)


def _pick(n, candidates):
    for t in candidates:
        if n % t == 0:
            return t
    return n


def _sigmoid(a):
    return 1.0 / (1.0 + jnp.exp(-a))


def _norm_mod(x, g, sc, sh):
    ms = jnp.mean(x * x, axis=-1, keepdims=True)
    return x * lax.rsqrt(ms + EPS) * g * (1.0 + sc) + sh


def _mod_kernel(c_ref, w_ref, b_ref, o_ref):
    c = c_ref[...]
    s = c * _sigmoid(c)
    o_ref[...] = jnp.dot(s, w_ref[...], preferred_element_type=F32,
                         precision=lax.Precision.HIGHEST) + b_ref[...]


def _modulation(cc, ada_w, ada_b):
    depth, d, n = ada_w.shape
    tn = _pick(n, (1536, 1024, 512))
    rows = cc.shape[0]
    return pl.pallas_call(
        _mod_kernel,
        grid=(depth, n // tn),
        in_specs=[
            pl.BlockSpec((rows, d), lambda l, j: (0, 0)),
            pl.BlockSpec((None, d, tn), lambda l, j: (l, 0, j)),
            pl.BlockSpec((None, 1, tn), lambda l, j: (l, 0, j)),
        ],
        out_specs=pl.BlockSpec((None, rows, tn), lambda l, j: (l, 0, j)),
        out_shape=jax.ShapeDtypeStruct((depth, rows, n), F32),
        compiler_params=_params("arbitrary", "arbitrary"),
        name="modulation",
    )(cc, ada_w, ada_b.reshape(depth, 1, n))


def _vec_spec(arr):
    d = arr.shape[-1]
    if arr.shape[0] == 1:
        return pl.BlockSpec((None, 1, d), lambda b, i, *_: (0, 0, 0))
    return pl.BlockSpec((None, 1, d), lambda b, i, *_: (b, 0, 0))


def _attn_in_kernel(x_ref, g_ref, sc_ref, sh_ref, w_ref, gains_ref, cos_ref, sin_ref, bd_ref,
                    qa_ref, ka_ref, va_ref, qb_ref, kb_ref, vb_ref, *, rope, tm):
    h = _norm_mod(x_ref[...], g_ref[...], sc_ref[...], sh_ref[...]).astype(BF16)
    y = jnp.dot(h, w_ref[...], preferred_element_type=F32)
    lane = lax.broadcasted_iota(jnp.int32, (tm, LANES), 1)
    even = (lane & 1) == 0
    bd = bd_ref[...]

    def head_norm(yc, gain):
        ss = jnp.dot((yc * yc).astype(BF16), bd, preferred_element_type=F32)
        t = yc * lax.rsqrt(ss * (1.0 / HEAD_DIM) + EPS) * gain
        if rope:
            partner = jnp.where(even, pltpu.roll(t, LANES - 1, 1), pltpu.roll(t, 1, 1))
            t = t * cos_ref[...] + partner * sin_ref[...]
        return t

    groups = ((0, qa_ref, ka_ref, va_ref, 0), (Q_WIDTH + 2 * KV_WIDTH, qb_ref, kb_ref, vb_ref, 2))
    for col0, q_ref, k_ref, v_ref, grow in groups:
        gq = gains_ref[grow:grow + 1, :]
        gk = gains_ref[grow + 1:grow + 2, :]
        for c in range(Q_WIDTH // LANES):
            t = head_norm(y[:, col0 + c * LANES:col0 + (c + 1) * LANES], gq)
            q_ref[c * LANES:(c + 1) * LANES, :] = t.T.astype(BF16)
        kcol = col0 + Q_WIDTH
        k_ref[:, 0:KV_WIDTH] = head_norm(y[:, kcol:kcol + KV_WIDTH], gk).astype(BF16)
        k_ref[:, KV_WIDTH:K_AUG] = jnp.where(lane == 0, 1.0, 0.0).astype(BF16)
        vt = y[:, kcol + KV_WIDTH:kcol + 2 * KV_WIDTH].T.astype(BF16)
        ones_rows = jnp.where(lax.broadcasted_iota(jnp.int32, (V_ROWS - HEAD_DIM, CHUNK), 0) == 0,
                              1.0, 0.0).astype(BF16)
        for j in range(tm // CHUNK):
            for kv in range(N_KV):
                r0 = kv * V_ROWS
                v_ref[j, r0:r0 + HEAD_DIM, :] = vt[kv * HEAD_DIM:(kv + 1) * HEAD_DIM,
                                                   j * CHUNK:(j + 1) * CHUNK]
                v_ref[j, r0 + HEAD_DIM:r0 + V_ROWS, :] = ones_rows


def _attn_in(x, g, sc, sh, w_bf16, gains, cos_t, sin_t, bd, *, rope):
    b, t, d = x.shape
    tm = _pick(t, (512, 256))
    nch = t // CHUNK
    kern = functools.partial(_attn_in_kernel, rope=rope, tm=tm)
    q_spec = pl.BlockSpec((None, Q_WIDTH, tm), lambda bb, i: (bb, 0, i))
    k_spec = pl.BlockSpec((None, tm, K_AUG), lambda bb, i: (bb, i, 0))
    v_spec = pl.BlockSpec((None, tm // CHUNK, N_KV * V_ROWS, CHUNK), lambda bb, i: (bb, i, 0, 0))
    q_shape = jax.ShapeDtypeStruct((b, Q_WIDTH, t), BF16)
    k_shape = jax.ShapeDtypeStruct((b, t, K_AUG), BF16)
    v_shape = jax.ShapeDtypeStruct((b, nch, N_KV * V_ROWS, CHUNK), BF16)
    return pl.pallas_call(
        kern,
        grid=(b, t // tm),
        in_specs=[
            pl.BlockSpec((None, tm, d), lambda bb, i: (bb, i, 0)),
            pl.BlockSpec((1, d), lambda bb, i: (0, 0)),
            _vec_spec(sc), _vec_spec(sh),
            pl.BlockSpec((d, IN_PROJ_WIDTH), lambda bb, i: (0, 0)),
            pl.BlockSpec((4, LANES), lambda bb, i: (0, 0)),
            pl.BlockSpec((tm, LANES), lambda bb, i: (i, 0)),
            pl.BlockSpec((tm, LANES), lambda bb, i: (i, 0)),
            pl.BlockSpec((LANES, LANES), lambda bb, i: (0, 0)),
        ],
        out_specs=[q_spec, k_spec, v_spec, q_spec, k_spec, v_spec],
        out_shape=[q_shape, k_shape, v_shape, q_shape, k_shape, v_shape],
        compiler_params=_params("parallel", "parallel"),
        name="attn_in",
    )(x, g, sc, sh, w_bf16, gains, cos_t, sin_t, bd)


def _attn_kernel(q_ref, k_ref, v_ref, sink_ref, o_ref, qpad_ref, s_ref, cmax_ref, m_ref, l_ref,
                 acc_ref, *, mode, use_sink, n_lat, tq):
    i = pl.program_id(1)
    w = GROUP * tq
    zeros = jnp.zeros((HEAD_DIM, w), BF16)
    for kv in range(N_KV):
        heads = range(kv * GROUP, (kv + 1) * GROUP)
        q4 = jnp.concatenate([q_ref[h * HEAD_DIM:(h + 1) * HEAD_DIM, :] for h in heads], axis=1)
        qpad_ref[kv] = jnp.concatenate([q4, zeros] if kv == 0 else [zeros, q4], axis=0)
        if use_sink:
            m_ref[kv] = jnp.concatenate(
                [jnp.full((1, tq), sink_ref[h] * LOG2E, F32) for h in heads], axis=1)
            l_ref[kv] = jnp.ones((1, w), F32)
        else:
            m_ref[kv] = jnp.full((1, w), NEG_BIG, F32)
            l_ref[kv] = jnp.zeros((1, w), F32)
        acc_ref[kv] = jnp.zeros((HEAD_DIM, w), F32)

    def scores(c, slot, mask):
        start = c * CHUNK if isinstance(c, int) else pl.multiple_of(c * CHUNK, CHUNK)
        kc = k_ref[pl.ds(start, CHUNK), 0:KV_WIDTH]
        for kv in range(N_KV):
            s = jnp.dot(kc, qpad_ref[kv], preferred_element_type=F32)
            if mask is not None:
                s = jnp.where(mask, s, -jnp.inf)
            s_ref[slot, kv] = s
            cmax_ref[slot, kv] = jnp.max(s, axis=0, keepdims=True)

    def absorb(c, slot):
        for kv in range(N_KV):
            m = m_ref[kv]
            m_new = jnp.maximum(m, cmax_ref[slot, kv])
            alpha = jnp.exp2(m - m_new)
            p = jnp.exp2(s_ref[slot, kv] - m_new)
            l_ref[kv] = alpha * l_ref[kv] + jnp.sum(p, axis=0, keepdims=True)
            m_ref[kv] = m_new
            vc = v_ref[c, kv * V_ROWS:kv * V_ROWS + HEAD_DIM, :]
            acc_ref[kv] = alpha * acc_ref[kv] + jnp.dot(vc, p.astype(BF16),
                                                        preferred_element_type=F32)

    scores(0, 0, None)
    if mode == "global":
        def body(j, carry):
            c = 2 * j
            scores(c + 1, 1, None)
            absorb(c, 0)
            scores(c + 2, 0, None)
            absorb(c + 1, 1)
            return carry
        lax.fori_loop(0, n_lat // 2, body, 0)
        c_end = 2 * (n_lat // 2)
        if n_lat % 2:
            scores(c_end + 1, 1, None)
            absorb(c_end, 0)
            absorb(c_end + 1, 1)
        else:
            absorb(c_end, 0)
    elif mode == "window":
        row = lax.broadcasted_iota(jnp.int32, (CHUNK, w), 0)
        col = lax.broadcasted_iota(jnp.int32, (CHUNK, w), 1) & (tq - 1)
        rel0 = col - row
        prev_c, prev_slot = 0, 0
        for d in (-1, 0, 1):
            cl = i + d
            ok = jnp.logical_and(cl >= 0, cl < n_lat)
            c = 1 + jnp.clip(cl, 0, n_lat - 1)
            mask = jnp.logical_and(jnp.abs(rel0 - d * CHUNK) <= WINDOW, ok)
            scores(c, 1 - prev_slot, mask)
            absorb(prev_c, prev_slot)
            prev_c, prev_slot = c, 1 - prev_slot
        absorb(prev_c, prev_slot)
    else:
        absorb(0, 0)

    for kv in range(N_KV):
        out = acc_ref[kv] * (1.0 / l_ref[kv])
        for g in range(GROUP):
            h = kv * GROUP + g
            o_ref[h * HEAD_DIM:(h + 1) * HEAD_DIM, :] = out[:, g * tq:(g + 1) * tq].astype(BF16)


def _attention(q_t, k_all, v_all, sink, bound, *, mode, use_sink):
    del bound
    b, _, tq_total = q_t.shape
    nk = k_all.shape[1]
    nc = v_all.shape[1]
    tq = CHUNK
    kern = functools.partial(_attn_kernel, mode=mode, use_sink=use_sink, n_lat=nc - 1, tq=tq)
    return pl.pallas_call(
        kern,
        grid=(b, tq_total // tq),
        in_specs=[
            pl.BlockSpec((None, Q_WIDTH, tq), lambda bb, i: (bb, 0, i)),
            pl.BlockSpec((None, nk, K_AUG), lambda bb, i: (bb, 0, 0)),
            pl.BlockSpec((None, nc, N_KV * V_ROWS, CHUNK), lambda bb, i: (bb, 0, 0, 0)),
            pl.BlockSpec(memory_space=pltpu.SMEM),
        ],
        out_specs=pl.BlockSpec((None, Q_WIDTH, tq), lambda bb, i: (bb, 0, i)),
        out_shape=jax.ShapeDtypeStruct((b, Q_WIDTH, tq_total), BF16),
        scratch_shapes=[pltpu.VMEM((N_KV, KV_WIDTH, GROUP * tq), BF16),
                        pltpu.VMEM((2, N_KV, CHUNK, GROUP * tq), F32),
                        pltpu.VMEM((2, N_KV, 1, GROUP * tq), F32),
                        pltpu.VMEM((N_KV, 1, GROUP * tq), F32),
                        pltpu.VMEM((N_KV, 1, GROUP * tq), F32),
                        pltpu.VMEM((N_KV, HEAD_DIM, GROUP * tq), F32)],
        compiler_params=_params("parallel", "parallel"),
        name="attn_" + mode,
    )(q_t, k_all, v_all, sink)


def _attn_fast_kernel(q_ref, k_ref, v_ref, sink_ref, bound_ref, o_ref, qa_ref, acc_ref, *,
                      mode, use_sink, n_lat, tq):
    i = pl.program_id(1)
    w = GROUP * tq
    bound = bound_ref[0]
    zeros = jnp.zeros((HEAD_DIM, w), BF16)
    row = lax.broadcasted_iota(jnp.int32, (KV_WIDTH, w), 0)
    shift = jnp.where(row == 0, -bound, 0.0).astype(BF16)
    for kv in range(N_KV):
        heads = range(kv * GROUP, (kv + 1) * GROUP)
        q4 = jnp.concatenate([q_ref[h * HEAD_DIM:(h + 1) * HEAD_DIM, :] for h in heads], axis=1)
        qa_ref[kv] = jnp.concatenate(([q4, zeros] if kv == 0 else [zeros, q4]) + [shift], axis=0)
        acc_ref[kv] = jnp.zeros((V_ROWS, w), F32)

    def update(c, mask, n=1):
        start = c * CHUNK if isinstance(c, int) else pl.multiple_of(c * CHUNK, CHUNK)
        kc = k_ref[pl.ds(start, n * CHUNK), :]
        for kv in range(N_KV):
            s = jnp.dot(kc, qa_ref[kv], preferred_element_type=F32)
            if mask is not None:
                s = jnp.where(mask, s, -jnp.inf)
            p = jnp.exp2(s).astype(BF16)
            vc = jnp.concatenate([v_ref[c + j, kv * V_ROWS:(kv + 1) * V_ROWS, :] for j in range(n)],
                                 axis=1)
            acc_ref[kv] += jnp.dot(vc, p, preferred_element_type=F32)

    if mode == "global":
        n_steps = (1 + n_lat) // GLOBAL_STEP

        def body(j, carry):
            update(GLOBAL_STEP * j, None, GLOBAL_STEP)
            return carry
        lax.fori_loop(0, n_steps, body, 0)
        for c in range(GLOBAL_STEP * n_steps, 1 + n_lat):
            update(c, None)
    else:
        update(0, None)
    if mode == "window":
        rowk =lax.broadcasted_iota(jnp.int32, (CHUNK, w), 0)
        col = lax.broadcasted_iota(jnp.int32, (CHUNK, w), 1) & (tq - 1)
        rel0 = col - rowk
        for d in (-1, 0, 1):
            cl = i + d
            ok = jnp.logical_and(cl >= 0, cl < n_lat)
            c = 1 + jnp.clip(cl, 0, n_lat - 1)
            update(c, jnp.logical_and(jnp.abs(rel0 - d * CHUNK) <= WINDOW, ok))

    for kv in range(N_KV):
        acc = acc_ref[kv]
        l = acc[HEAD_DIM:HEAD_DIM + 1, :]
        if use_sink:
            l = l + jnp.concatenate(
                [jnp.full((1, tq), jnp.exp2(sink_ref[kv * GROUP + g] * LOG2E - bound), F32)
                 for g in range(GROUP)], axis=1)
        out = acc[0:HEAD_DIM, :] * (1.0 / l)
        for g in range(GROUP):
            h = kv * GROUP + g
            o_ref[h * HEAD_DIM:(h + 1) * HEAD_DIM, :] = out[:, g * tq:(g + 1) * tq].astype(BF16)


def _attention_fast(q_t, k_all, v_all, sink, bound, *, mode, use_sink):
    b, _, tq_total = q_t.shape
    nk = k_all.shape[1]
    nc = v_all.shape[1]
    tq = CHUNK
    kern = functools.partial(_attn_fast_kernel, mode=mode, use_sink=use_sink, n_lat=nc - 1, tq=tq)
    return pl.pallas_call(
        kern,
        grid=(b, tq_total // tq),
        in_specs=[
            pl.BlockSpec((None, Q_WIDTH, tq), lambda bb, i: (bb, 0, i)),
            pl.BlockSpec((None, nk, K_AUG), lambda bb, i: (bb, 0, 0)),
            pl.BlockSpec((None, nc, N_KV * V_ROWS, CHUNK), lambda bb, i: (bb, 0, 0, 0)),
            pl.BlockSpec(memory_space=pltpu.SMEM),
            pl.BlockSpec(memory_space=pltpu.SMEM),
        ],
        out_specs=pl.BlockSpec((None, Q_WIDTH, tq), lambda bb, i: (bb, 0, i)),
        out_shape=jax.ShapeDtypeStruct((b, Q_WIDTH, tq_total), BF16),
        scratch_shapes=[pltpu.VMEM((N_KV, K_AUG, GROUP * tq), BF16),
                        pltpu.VMEM((N_KV, V_ROWS, GROUP * tq), F32)],
        compiler_params=_params("parallel", "parallel"),
        name="attn_fast_" + mode,
    )(q_t, k_all, v_all, sink, bound)


def _attend(fast_ok, *args, **kw):
    return lax.cond(fast_ok, functools.partial(_attention_fast, **kw),
                    functools.partial(_attention, **kw), *args)


_TN_DIMS = (((0,), (0,)), ((), ()))


def _attn_out_kernel(oa_ref, ob_ref, w_ref, x_ref, g1_ref, n2_ref, sc2_ref, sh2_ref, x1_ref, h2_ref):
    mix = lax.dot_general(oa_ref[...], w_ref[0:Q_WIDTH, :], _TN_DIMS, preferred_element_type=F32)
    mix = mix + lax.dot_general(ob_ref[...], w_ref[Q_WIDTH:2 * Q_WIDTH, :], _TN_DIMS,
                                preferred_element_type=F32)
    x1 = x_ref[...] + g1_ref[...] * mix
    x1_ref[...] = x1
    h2_ref[...] = _norm_mod(x1, n2_ref[...], sc2_ref[...], sh2_ref[...]).astype(BF16)


def _attn_out(oa, ob, w_bf16, x, g1, n2, sc2, sh2):
    b, t, d = x.shape
    tm = _pick(t, (512, 256))
    o_spec = pl.BlockSpec((None, Q_WIDTH, tm), lambda bb, i: (bb, 0, i))
    x_spec = pl.BlockSpec((None, tm, d), lambda bb, i: (bb, i, 0))
    return pl.pallas_call(
        _attn_out_kernel,
        grid=(b, t // tm),
        in_specs=[o_spec, o_spec,
                  pl.BlockSpec((2 * Q_WIDTH, d), lambda bb, i: (0, 0)),
                  x_spec, _vec_spec(g1),
                  pl.BlockSpec((1, d), lambda bb, i: (0, 0)),
                  _vec_spec(sc2), _vec_spec(sh2)],
        out_specs=[x_spec, x_spec],
        out_shape=[jax.ShapeDtypeStruct((b, t, d), F32), jax.ShapeDtypeStruct((b, t, d), BF16)],
        compiler_params=_params("parallel", "parallel"),
        name="attn_out",
    )(oa, ob, w_bf16, x, g1, n2, sc2, sh2)


def _ffn_kernel(*refs, gated, tm):
    if gated:
        h_ref, wg_ref, wu_ref, wd_ref, gate_ref, x_ref, g2_ref, o_ref, acc_ref = refs
    else:
        h_ref, wg_ref, wu_ref, wd_ref, x_ref, g2_ref, o_ref, acc_ref = refs
    e = pl.program_id(2)
    f = pl.program_id(3)
    first = jnp.logical_and(e == 0, f == 0)
    last = jnp.logical_and(e == pl.num_programs(2) - 1, f == pl.num_programs(3) - 1)

    @pl.when(first)
    def _():
        acc_ref[...] = jnp.zeros_like(acc_ref)

    h = h_ref[...]
    a = jnp.dot(h, wg_ref[...].astype(BF16), preferred_element_type=F32)
    u = jnp.dot(h, wu_ref[...].astype(BF16), preferred_element_type=F32)
    act = (a * _sigmoid(a) * u).astype(BF16)
    y = jnp.dot(act, wd_ref[...].astype(BF16), preferred_element_type=F32)
    if gated:
        lane = lax.broadcasted_iota(jnp.int32, (tm, LANES), 1)
        ge = jnp.sum(jnp.where(lane == e, gate_ref[...], 0.0), axis=-1, keepdims=True)
        y = y * ge
    acc_ref[...] += y

    @pl.when(last)
    def _():
        o_ref[...] = x_ref[...] + g2_ref[...] * acc_ref[...]


def _ffn(h2, wg, wu, wd, gates, x1, g2, e0=0, n_e=1):
    b, t, d = x1.shape
    f = wg.shape[-1]
    gated = gates is not None
    tm = _pick(t, (1024, 512, 256) if gated else (512, 256))
    resident = n_e == 1 and 3 * d * f * 2 <= VMEM_LIMIT_BYTES // 3
    tf = f if resident else _pick(f, (512, 256))
    mode = dict(pipeline_mode=pl.Buffered(1)) if resident else {}
    x_spec = pl.BlockSpec((None, tm, d), lambda bb, i, e, j: (bb, i, 0))
    in_specs = [x_spec,
                pl.BlockSpec((None, d, tf), lambda bb, i, e, j: (e0 + e, 0, j), **mode),
                pl.BlockSpec((None, d, tf), lambda bb, i, e, j: (e0 + e, 0, j), **mode),
                pl.BlockSpec((None, tf, d), lambda bb, i, e, j: (e0 + e, j, 0), **mode)]
    args = [h2, wg, wu, wd]
    if gated:
        in_specs.append(pl.BlockSpec((None, tm, LANES), lambda bb, i, e, j: (bb, i, 0)))
        args.append(gates)
    in_specs += [x_spec, _vec_spec(g2)]
    args += [x1, g2]
    return pl.pallas_call(
        functools.partial(_ffn_kernel, gated=gated, tm=tm),
        grid=(b, t // tm, n_e, f // tf),
        in_specs=in_specs,
        out_specs=x_spec,
        out_shape=jax.ShapeDtypeStruct((b, t, d), F32),
        scratch_shapes=[pltpu.VMEM((tm, d), F32)],
        compiler_params=_params("parallel", "parallel", "arbitrary", "arbitrary"),
        name="ffn_gated" if gated else "ffn_dense",
    )(*args)


def _conv_in_kernel(x_ref, g_ref, sc_ref, sh_ref, w_ref, bg_ref, z_ref, *, d):
    h = _norm_mod(x_ref[...], g_ref[...], sc_ref[...], sh_ref[...]).astype(BF16)
    y = jnp.dot(h, w_ref[...], preferred_element_type=F32)
    bg_ref[...] = y[:, 0:d].astype(BF16)
    z_ref[...] = (y[:, d:2 * d] * y[:, 2 * d:3 * d]).astype(BF16)


def _conv_in(x, g, sc, sh, w_bf16):
    b, t, d = x.shape
    tm = _pick(t, (512, 256))
    x_spec = pl.BlockSpec((None, tm, d), lambda bb, i: (bb, i, 0))
    return pl.pallas_call(
        functools.partial(_conv_in_kernel, d=d),
        grid=(b, t // tm),
        in_specs=[x_spec, pl.BlockSpec((1, d), lambda bb, i: (0, 0)), _vec_spec(sc), _vec_spec(sh),
                  pl.BlockSpec((d, 3 * d), lambda bb, i: (0, 0))],
        out_specs=[x_spec, x_spec],
        out_shape=[jax.ShapeDtypeStruct((b, t, d), BF16), jax.ShapeDtypeStruct((b, t, d), BF16)],
        compiler_params=_params("parallel", "parallel"),
        name="conv_in",
    )(x, g, sc, sh, w_bf16)


HALO = 16


def _conv_out_kernel(z_ref, zp_ref, zn_ref, bg_ref, cw_ref, w_ref, x_ref, g1_ref, n2_ref, sc2_ref,
                     sh2_ref, rw_ref, x1_ref, h2_ref, gate_ref, *, tm, n_experts):
    i = pl.program_id(1)
    z = z_ref[...].astype(F32)
    prev = jnp.where(i > 0, zp_ref[HALO - 1:HALO, :].astype(F32), 0.0)
    nxt = jnp.where(i < pl.num_programs(1) - 1, zn_ref[0:1, :].astype(F32), 0.0)
    row = lax.broadcasted_iota(jnp.int32, z.shape, 0)
    z_dn = jnp.where(row == 0, prev, pltpu.roll(z, 1, 0))
    z_up = jnp.where(row == tm - 1, nxt, pltpu.roll(z, tm - 1, 0))
    conv = z_dn * cw_ref[0:1, :] + z * cw_ref[1:2, :] + z_up * cw_ref[2:3, :]
    v = (bg_ref[...].astype(F32) * conv).astype(BF16)
    mix = jnp.dot(v, w_ref[...], preferred_element_type=F32)
    x1 = x_ref[...] + g1_ref[...] * mix
    x1_ref[...] = x1
    h2 = _norm_mod(x1, n2_ref[...], sc2_ref[...], sh2_ref[...])
    h2_ref[...] = h2.astype(h2_ref.dtype)

    h_hi = h2.astype(BF16)
    h_lo = (h2 - h_hi.astype(F32)).astype(BF16)
    logits = (jnp.dot(h_hi, rw_ref[0], preferred_element_type=F32)
              + jnp.dot(h_lo, rw_ref[0], preferred_element_type=F32)
              + jnp.dot(h_hi, rw_ref[1], preferred_element_type=F32))
    lane = lax.broadcasted_iota(jnp.int32, logits.shape, 1)
    lanef = lane.astype(F32)
    logits = jnp.where(lane < n_experts, logits, -jnp.inf)
    m1 = jnp.max(logits, axis=-1, keepdims=True)
    i1 = jnp.min(jnp.where(logits == m1, lanef, float(LANES)), axis=-1, keepdims=True)
    rest = jnp.where(lanef == i1, -jnp.inf, logits)
    m2 = jnp.max(rest, axis=-1, keepdims=True)
    i2 = jnp.min(jnp.where(rest == m2, lanef, float(LANES)), axis=-1, keepdims=True)
    e2 = jnp.exp(m2 - m1)
    w1 = 1.0 / (1.0 + e2)
    w2 = e2 * w1
    rec = jnp.where(lanef == i1, w1, 0.0) + jnp.where(lanef == i2, w2, 0.0)
    for k, val in enumerate((i1, i2, w1, w2)):
        rec = jnp.where(lane == n_experts + k, val, rec)
    gate_ref[...] = rec


def _conv_out(z, bg, conv_w, w_bf16, x, g1, n2, sc2, sh2, router_pad, n_experts, h2_dtype):
    b, t, d = x.shape
    tm = _pick(t, (512, 256))
    hb = tm // HALO
    n_halo = t // HALO
    x_spec = pl.BlockSpec((None, tm, d), lambda bb, i: (bb, i, 0))
    prev_spec = pl.BlockSpec((None, HALO, d), lambda bb, i: (bb, jnp.maximum(i * hb - 1, 0), 0))
    next_spec = pl.BlockSpec((None, HALO, d), lambda bb, i: (bb, jnp.minimum((i + 1) * hb, n_halo - 1), 0))
    full = lambda r, c: pl.BlockSpec((r, c), lambda bb, i: (0, 0))
    return pl.pallas_call(
        functools.partial(_conv_out_kernel, tm=tm, n_experts=n_experts),
        grid=(b, t // tm),
        in_specs=[x_spec, prev_spec, next_spec, x_spec, full(3, d), full(d, d), x_spec, _vec_spec(g1),
                  full(1, d), _vec_spec(sc2), _vec_spec(sh2),
                  pl.BlockSpec((2, d, LANES), lambda bb, i: (0, 0, 0))],
        out_specs=[x_spec, x_spec, pl.BlockSpec((None, tm, LANES), lambda bb, i: (bb, i, 0))],
        out_shape=[jax.ShapeDtypeStruct((b, t, d), F32), jax.ShapeDtypeStruct((b, t, d), h2_dtype),
                   jax.ShapeDtypeStruct((b, t, LANES), F32)],
        compiler_params=_params("parallel", "parallel"),
        name="conv_out",
    )(z, z, z, bg, conv_w, w_bf16, x, g1, n2, sc2, sh2, router_pad)


EXPERT_TILE = 1024


def _rank_kernel(route_ref, rank_ref, count_ref, base_ref, *, tm, n_experts):
    @pl.when(pl.program_id(0) == 0)
    def _():
        base_ref[...] = jnp.zeros_like(base_ref)

    route = route_ref[...]
    lane = lax.broadcasted_iota(jnp.int32, (tm, LANES), 1)
    lanef = lane.astype(F32)
    i1 = jnp.sum(jnp.where(lane == n_experts, route, 0.0), axis=-1, keepdims=True)
    i2 = jnp.sum(jnp.where(lane == n_experts + 1, route, 0.0), axis=-1, keepdims=True)
    onehot = jnp.where(lanef == i1, 1.0, 0.0) + jnp.where(lanef == i2, 1.0, 0.0)
    r = lax.broadcasted_iota(jnp.int32, (tm, tm), 0)
    c = lax.broadcasted_iota(jnp.int32, (tm, tm), 1)
    lower = jnp.where(c < r, 1.0, 0.0).astype(BF16)
    before = jnp.dot(lower, onehot.astype(BF16), preferred_element_type=F32) + base_ref[...]
    r1 = jnp.sum(jnp.where(lanef == i1, before, 0.0), axis=-1, keepdims=True)
    r2 = jnp.sum(jnp.where(lanef == i2, before, 0.0), axis=-1, keepdims=True)
    packed = jnp.where(lane == 0, r1, jnp.where(lane == 1, r2, jnp.where(lane == 2, i1,
                       jnp.where(lane == 3, i2, 0.0))))
    rank_ref[...] = packed.T[0:8, :]
    base_ref[...] += jnp.sum(onehot, axis=0, keepdims=True)
    count_ref[...] = base_ref[...]


def _rank(route, n_experts):
    n = route.shape[0]
    tm = _pick(n, (512, 256))
    return pl.pallas_call(
        functools.partial(_rank_kernel, tm=tm, n_experts=n_experts),
        grid=(n // tm,),
        in_specs=[pl.BlockSpec((tm, LANES), lambda i: (i, 0))],
        out_specs=[pl.BlockSpec((8, tm), lambda i: (0, i)), pl.BlockSpec((1, LANES), lambda i: (0, 0))],
        out_shape=[jax.ShapeDtypeStruct((8, n), F32), jax.ShapeDtypeStruct((1, LANES), F32)],
        scratch_shapes=[pltpu.VMEM((1, LANES), F32)],
        compiler_params=_params("arbitrary"),
        name="moe_rank",
    )(route)


def _dispatch_kernel(dest_ref, h_ref, xs_in_ref, xs_ref, sem, *, tm):
    del xs_in_ref

    def row_copy(r, k):
        d = dest_ref[0, k * tm + r]
        return pltpu.make_async_copy(h_ref.at[pl.ds(r, 1), :], xs_ref.at[pl.ds(d, 1), :], sem)

    def issue(r, carry):
        row_copy(r, 0).start()
        row_copy(r, 1).start()
        return carry

    lax.fori_loop(0, tm, issue, 0, unroll=8)

    def drain(r, carry):
        row_copy(r, 0).wait()
        row_copy(r, 1).wait()
        return carry

    lax.fori_loop(0, tm, drain, 0, unroll=8)


def _dispatch(dest, h2, n_rows):
    n, d = h2.shape
    tm = dest.shape[-1] // 2
    zeros = jnp.zeros((n_rows, d), h2.dtype)
    return pl.pallas_call(
        functools.partial(_dispatch_kernel, tm=tm),
        grid=(n // tm,),
        in_specs=[pl.BlockSpec((None, 1, 2 * tm), lambda i: (i, 0, 0), memory_space=pltpu.SMEM),
                  pl.BlockSpec((tm, d), lambda i: (i, 0)),
                  pl.BlockSpec(memory_space=pl.ANY)],
        out_specs=pl.BlockSpec(memory_space=pl.ANY),
        out_shape=jax.ShapeDtypeStruct((n_rows, d), h2.dtype),
        scratch_shapes=[pltpu.SemaphoreType.DMA(())],
        input_output_aliases={2: 0},
        compiler_params=_params("arbitrary"),
        name="moe_dispatch",
    )(dest, h2, zeros)


def _expert_kernel(te_ref, nu_ref, xs_ref, wg_ref, wu_ref, wd_ref, o_ref, hb_ref, acc_ref):
    del te_ref
    j = pl.program_id(0)
    f = pl.program_id(1)

    @pl.when(j < nu_ref[0])
    def _():
        @pl.when(f == 0)
        def _():
            hb_ref[...] = xs_ref[...].astype(BF16)
            acc_ref[...] = jnp.zeros_like(acc_ref)

        h = hb_ref[...]
        a = jnp.dot(h, wg_ref[...].astype(BF16), preferred_element_type=F32)
        u = jnp.dot(h, wu_ref[...].astype(BF16), preferred_element_type=F32)
        act = (a * _sigmoid(a) * u).astype(BF16)
        acc_ref[...] += jnp.dot(act, wd_ref[...].astype(BF16), preferred_element_type=F32)

        @pl.when(f == pl.num_programs(1) - 1)
        def _():
            o_ref[...] = acc_ref[...]

    @pl.when(jnp.logical_and(j >= nu_ref[0], f == pl.num_programs(1) - 1))
    def _():
        o_ref[...] = jnp.zeros_like(o_ref)


def _experts(tile_expert, n_used, xs, wg, wu, wd):
    n_rows, d = xs.shape
    f = wg.shape[-1]
    tf = _pick(f, (512, 256))
    nf = f // tf
    n_tiles = n_rows // EXPERT_TILE

    def row_map(j, ff, te, nu):
        return (jnp.minimum(j, nu[0] - 1), 0)

    def w_up_map(j, ff, te, nu):
        live = j < nu[0]
        return (te[jnp.minimum(j, nu[0] - 1)], 0, jnp.where(live, ff, nf - 1))

    def w_down_map(j, ff, te, nu):
        live = j < nu[0]
        return (te[jnp.minimum(j, nu[0] - 1)], jnp.where(live, ff, nf - 1), 0)

    grid_spec = pltpu.PrefetchScalarGridSpec(
        num_scalar_prefetch=2,
        grid=(n_tiles, nf),
        in_specs=[pl.BlockSpec((EXPERT_TILE, d), row_map),
                  pl.BlockSpec((None, d, tf), w_up_map),
                  pl.BlockSpec((None, d, tf), w_up_map),
                  pl.BlockSpec((None, tf, d), w_down_map)],
        out_specs=pl.BlockSpec((EXPERT_TILE, d), lambda j, ff, te, nu: (j, 0)),
        scratch_shapes=[pltpu.VMEM((EXPERT_TILE, d), BF16), pltpu.VMEM((EXPERT_TILE, d), F32)],
    )
    return pl.pallas_call(
        _expert_kernel,
        grid_spec=grid_spec,
        out_shape=jax.ShapeDtypeStruct((n_rows, d), F32),
        compiler_params=_params("arbitrary", "arbitrary"),
        name="moe_experts",
    )(tile_expert, n_used, xs, wg, wu, wd)


def _combine_kernel(dest_ref, os_ref, route_ref, x_ref, g2_ref, o_ref, buf_ref, sem, *, tm, n_experts):
    def row_copy(r, k):
        d = dest_ref[0, k * tm + r]
        return pltpu.make_async_copy(os_ref.at[pl.ds(d, 1), :], buf_ref.at[k, pl.ds(r, 1), :], sem)

    def issue(r, carry):
        row_copy(r, 0).start()
        row_copy(r, 1).start()
        return carry

    lax.fori_loop(0, tm, issue, 0, unroll=8)

    def drain(r, carry):
        row_copy(r, 0).wait()
        row_copy(r, 1).wait()
        return carry

    lax.fori_loop(0, tm, drain, 0, unroll=8)

    route = route_ref[...]
    lane = lax.broadcasted_iota(jnp.int32, (tm, LANES), 1)
    w1 = jnp.sum(jnp.where(lane == n_experts + 2, route, 0.0), axis=-1, keepdims=True)
    w2 = jnp.sum(jnp.where(lane == n_experts + 3, route, 0.0), axis=-1, keepdims=True)
    o_ref[...] = x_ref[...] + g2_ref[...] * (w1 * buf_ref[0] + w2 * buf_ref[1])


def _combine(dest, os, route, x1, g2, n_experts):
    b, t, d = x1.shape
    tm = dest.shape[-1] // 2
    tiles_per_seq = t // tm
    x_spec = pl.BlockSpec((None, tm, d), lambda bb, i: (bb, i, 0))
    return pl.pallas_call(
        functools.partial(_combine_kernel, tm=tm, n_experts=n_experts),
        grid=(b, tiles_per_seq),
        in_specs=[pl.BlockSpec((None, 1, 2 * tm), lambda bb, i: (bb * tiles_per_seq + i, 0, 0),
                               memory_space=pltpu.SMEM),
                  pl.BlockSpec(memory_space=pl.ANY),
                  pl.BlockSpec((None, tm, LANES), lambda bb, i: (bb, i, 0)),
                  x_spec, _vec_spec(g2)],
        out_specs=x_spec,
        out_shape=jax.ShapeDtypeStruct((b, t, d), F32),
        scratch_shapes=[pltpu.VMEM((2, tm, d), F32), pltpu.SemaphoreType.DMA(())],
        compiler_params=_params("arbitrary", "arbitrary"),
        name="moe_combine",
    )(dest, os, route, x1, g2)


def _moe_routed(h2, route, x1, g2, wg, wu, wd, e0, n_experts):
    b, t, d = x1.shape
    n = b * t
    tm = _pick(t, (512, 256))
    rec = route.reshape(n, LANES)
    rank, counts = _rank(rec, n_experts)

    cnt = counts[0, :n_experts].astype(jnp.int32)
    padded = ((cnt + EXPERT_TILE - 1) // EXPERT_TILE) * EXPERT_TILE
    ends = jnp.cumsum(padded)
    offs = ends - padded
    n_tiles = (2 * n) // EXPERT_TILE + n_experts
    n_used = (ends[-1] // EXPERT_TILE).astype(jnp.int32).reshape(1)
    tile_start = jnp.arange(n_tiles, dtype=jnp.int32) * EXPERT_TILE
    tile_expert = jnp.minimum(jnp.sum(tile_start[:, None] >= ends[None, :], axis=1), n_experts - 1)
    slot = rank.astype(jnp.int32)
    d1 = jnp.take(offs, slot[2]) + slot[0]
    d2 = jnp.take(offs, slot[3]) + slot[1]
    dest = jnp.concatenate([d1.reshape(n // tm, tm), d2.reshape(n // tm, tm)], axis=1)[:, None, :]

    xs = _dispatch(dest, h2.reshape(n, d), n_tiles * EXPERT_TILE)
    os = _experts((e0 + tile_expert).astype(jnp.int32), n_used, xs, wg, wu, wd)
    return _combine(dest, os, route, x1, g2, n_experts)


def _rope_tables(n_tokens):
    rows = n_tokens // GRID_W
    row, col = jnp.meshgrid(jnp.arange(rows, dtype=F32), jnp.arange(GRID_W, dtype=F32), indexing="ij")
    half = HEAD_DIM // 2
    inv_freq = ROPE_THETA ** (-jnp.arange(0, half, 2, dtype=F32) / half)
    ang = jnp.concatenate([row.reshape(-1, 1) * inv_freq, col.reshape(-1, 1) * inv_freq], axis=-1)
    cos = jnp.repeat(jnp.cos(ang), 2, axis=-1)
    sign = jnp.tile(jnp.array([-1.0, 1.0], F32), HEAD_DIM // 2)
    sin = jnp.repeat(jnp.sin(ang), 2, axis=-1) * sign
    reps = LANES // HEAD_DIM
    return jnp.tile(cos, (1, reps)), jnp.tile(sin, (1, reps))


def kernel(x, c, ctx, c_ctx, ada_w, ada_b, norm1_g, norm2_g, attn_w_in, attn_w_out, qnorm_a, knorm_a, qnorm_b, knorm_b, sink_b, ffn_w_gate, ffn_w_up, ffn_w_down, conv_w_in, conv_w, conv_w_out, router_w, moe_w_gate, moe_w_up, moe_w_down):
    bsz, n_tok, d = x.shape
    n_ctx = ctx.shape[1]
    depth = ada_w.shape[0]
    n_experts = router_w.shape[-1]
    assert n_ctx == CHUNK and n_tok % CHUNK == 0 and bsz + 1 <= 8
    assert attn_w_in.shape[-1] == IN_PROJ_WIDTH and d % LANES == 0 and n_experts + 4 <= LANES

    cc = jnp.zeros((8, d), F32).at[:bsz].set(c).at[bsz].set(c_ctx)
    mod = _modulation(cc, ada_w, ada_b)

    def mod_vec(layer, j, is_ctx):
        m = mod[layer, :, j * d:(j + 1) * d]
        return m[bsz:bsz + 1].reshape(1, 1, d) if is_ctx else m[:bsz].reshape(bsz, 1, d)

    cos_l, sin_l = _rope_tables(n_tok)
    cos_c = jnp.ones((n_ctx, LANES), F32)
    sin_c = jnp.zeros((n_ctx, LANES), F32)
    head_id = jnp.arange(LANES) // HEAD_DIM
    bd = (head_id[:, None] == head_id[None, :]).astype(BF16)
    no_sink = jnp.zeros((N_HEADS,), F32)
    tile2 = lambda v: jnp.tile(v, LANES // HEAD_DIM)

    xc = ctx
    for layer in range(depth):
        i = layer // 2
        ctx_needed = any(j % 2 == 0 for j in range(layer + 1, depth))
        n1 = norm1_g[layer].reshape(1, d)
        n2 = norm2_g[layer].reshape(1, d)
        mv = lambda j, is_ctx: mod_vec(layer, j, is_ctx)

        if layer % 2 == 0:
            w_in = attn_w_in[i].astype(BF16)
            w_out = attn_w_out[i].astype(BF16)
            qscale = ATTN_SCALE * LOG2E
            gains = jnp.stack([tile2(qnorm_a[i]) * qscale, tile2(knorm_a[i]),
                               tile2(qnorm_b[i]) * qscale, tile2(knorm_b[i])]).astype(F32)
            sink = sink_b[i].astype(F32)
            def logit_bound(gq, gk):
                raw = HEAD_DIM * jnp.max(jnp.abs(gq)) * jnp.max(jnp.abs(gk)) * qscale * 1.02
                return raw.astype(BF16).astype(F32)
            bound_a = logit_bound(qnorm_a[i], knorm_a[i])
            bound_b = jnp.maximum(logit_bound(qnorm_b[i], knorm_b[i]),
                                  (jnp.max(sink) * LOG2E * 1.02).astype(BF16).astype(F32))
            fast_a = bound_a <= FAST_MAX_LOGIT
            fast_b = jnp.logical_and(bound_b <= FAST_MAX_LOGIT,
                                     jnp.max(jnp.abs(sink)) * LOG2E <= FAST_MAX_LOGIT)
            bound_a = bound_a.reshape(1)
            bound_b = bound_b.reshape(1)
            qa_l, ka_l, va_l, qb_l, kb_l, vb_l = _attn_in(
                x, n1, mv(1, False), mv(0, False), w_in, gains, cos_l, sin_l, bd, rope=True)
            qa_c, ka_c, va_c, qb_c, kb_c, vb_c = _attn_in(
                xc, n1, mv(1, True), mv(0, True), w_in, gains, cos_c, sin_c, bd, rope=False)
            ka_all = jnp.concatenate([ka_c, ka_l], axis=1)
            va_all = jnp.concatenate([va_c, va_l], axis=1)
            kb_all = jnp.concatenate([kb_c, kb_l], axis=1)
            vb_all = jnp.concatenate([vb_c, vb_l], axis=1)
            oa = _attend(fast_a, qa_l, ka_all, va_all, no_sink, bound_a, mode="global", use_sink=False)
            ob = _attend(fast_b, qb_l, kb_all, vb_all, sink, bound_b, mode="window", use_sink=True)
            x, h2 = _attn_out(oa, ob, w_out, x, mv(2, False), n2, mv(4, False), mv(3, False))
            if ctx_needed:
                oa_c = _attend(fast_a, qa_c, ka_c, va_c, no_sink, bound_a, mode="ctx", use_sink=False)
                ob_c = _attend(fast_b, qb_c, kb_c, vb_c, sink, bound_b, mode="ctx", use_sink=True)
                xc, h2c = _attn_out(oa_c, ob_c, w_out, xc, mv(2, True), n2, mv(4, True), mv(3, True))
            wg = ffn_w_gate.astype(BF16)
            wu = ffn_w_up.astype(BF16)
            wd = ffn_w_down.astype(BF16)
            x = _ffn(h2, wg, wu, wd, None, x, mv(5, False), e0=i)
            if ctx_needed:
                xc = _ffn(h2c, wg, wu, wd, None, xc, mv(5, True), e0=i)
        else:
            w_in = conv_w_in[i].astype(BF16)
            w_out = conv_w_out[i].astype(BF16)
            cw = conv_w[i].astype(F32)
            router_f32 = jnp.zeros((d, LANES), F32).at[:, :n_experts].set(router_w[i])
            router_hi = router_f32.astype(BF16)
            router_pad = jnp.stack([router_hi, (router_f32 - router_hi.astype(F32)).astype(BF16)])
            wg = moe_w_gate.reshape((-1,) + moe_w_gate.shape[2:])
            wu = moe_w_up.reshape((-1,) + moe_w_up.shape[2:])
            wd = moe_w_down.reshape((-1,) + moe_w_down.shape[2:])
            e0 = i * n_experts
            bg, z = _conv_in(x, n1, mv(1, False), mv(0, False), w_in)
            x, h2, route = _conv_out(z, bg, cw, w_out, x, mv(2, False), n2, mv(4, False), mv(3, False),
                                     router_pad, n_experts, F32)
            x = _moe_routed(h2, route, x, mv(5, False), wg, wu, wd, e0, n_experts)
            if ctx_needed:
                bg, z = _conv_in(xc, n1, mv(1, True), mv(0, True), w_in)
                xc, h2c, gates_c = _conv_out(z, bg, cw, w_out, xc, mv(2, True), n2, mv(4, True),
                                             mv(3, True), router_pad, n_experts, BF16)
                flat = lambda a: a.reshape((1, bsz * n_ctx) + a.shape[2:])
                xc = _ffn(flat(h2c), wg, wu, wd, flat(gates_c), flat(xc), mv(5, True),
                          e0=e0, n_e=n_experts).reshape(bsz, n_ctx, d)
    return x
```

```python
import functools

import jax
import jax.numpy as jnp
from jax import lax
from jax.experimental import pallas as pl
from jax.experimental.pallas import tpu as pltpu

F32 = jnp.float32
BF16 = jnp.bfloat16

HEAD_DIM = 64
N_KV = 2
GROUP = 4
N_HEADS = N_KV * GROUP
Q_WIDTH = N_HEADS * HEAD_DIM
KV_WIDTH = N_KV * HEAD_DIM
IN_PROJ_WIDTH = 2 * (Q_WIDTH + 2 * KV_WIDTH)
GRID_W = 64
WINDOW = 128
ROPE_THETA = 10000.0
ATTN_SCALE = HEAD_DIM ** -0.5
EPS = 1e-6
N_ADA = 6
LOG2E = 1.4426950408889634
LANES = 128
CHUNK = 256
K_AUG = 2 * KV_WIDTH
V_ROWS = HEAD_DIM + 16
GLOBAL_STEP = 4
FAST_MAX_LOGIT = 40.0
NEG_BIG = -1e30
VMEM_LIMIT_BYTES = 56 * 1024 * 1024


def _params(*sem):
    return pltpu.CompilerParams(dimension_semantics=sem, vmem_limit_bytes=VMEM_LIMIT_BYTES)


def _pick(n, candidates):
    for t in candidates:
        if n % t == 0:
            return t
    return n


def _sigmoid(a):
    return 1.0 / (1.0 + jnp.exp(-a))


def _norm_mod(x, g, sc, sh):
    ms = jnp.mean(x * x, axis=-1, keepdims=True)
    return x * lax.rsqrt(ms + EPS) * g * (1.0 + sc) + sh


def _mod_kernel(c_ref, w_ref, b_ref, o_ref):
    c = c_ref[...]
    s = c * _sigmoid(c)
    o_ref[...] = jnp.dot(s, w_ref[...], preferred_element_type=F32,
                         precision=lax.Precision.HIGHEST) + b_ref[...]


def _modulation(cc, ada_w, ada_b):
    depth, d, n = ada_w.shape
    tn = _pick(n, (1536, 1024, 512))
    rows = cc.shape[0]
    return pl.pallas_call(
        _mod_kernel,
        grid=(depth, n // tn),
        in_specs=[
            pl.BlockSpec((rows, d), lambda l, j: (0, 0)),
            pl.BlockSpec((None, d, tn), lambda l, j: (l, 0, j)),
            pl.BlockSpec((None, 1, tn), lambda l, j: (l, 0, j)),
        ],
        out_specs=pl.BlockSpec((None, rows, tn), lambda l, j: (l, 0, j)),
        out_shape=jax.ShapeDtypeStruct((depth, rows, n), F32),
        compiler_params=_params("arbitrary", "arbitrary"),
        name="modulation",
    )(cc, ada_w, ada_b.reshape(depth, 1, n))


def _vec_spec(arr):
    d = arr.shape[-1]
    if arr.shape[0] == 1:
        return pl.BlockSpec((None, 1, d), lambda b, i, *_: (0, 0, 0))
    return pl.BlockSpec((None, 1, d), lambda b, i, *_: (b, 0, 0))


def _attn_in_kernel(x_ref, g_ref, sc_ref, sh_ref, w_ref, gains_ref, cos_ref, sin_ref, bd_ref,
                    qa_ref, ka_ref, va_ref, qb_ref, kb_ref, vb_ref, *, rope, tm):
    lane = lax.broadcasted_iota(jnp.int32, (CHUNK, LANES), 1)
    even = (lane & 1) == 0
    bd = bd_ref[...]
    ones_rows = jnp.where(lax.broadcasted_iota(jnp.int32, (V_ROWS - HEAD_DIM, CHUNK), 0) == 0,
                          1.0, 0.0).astype(BF16)
    groups = ((0, qa_ref, ka_ref, va_ref, 0), (Q_WIDTH + 2 * KV_WIDTH, qb_ref, kb_ref, vb_ref, 2))

    for j in range(tm // CHUNK):
        rows = slice(j * CHUNK, (j + 1) * CHUNK)
        h = _norm_mod(x_ref[rows, :], g_ref[...], sc_ref[...], sh_ref[...]).astype(BF16)
        y = jnp.dot(h, w_ref[...], preferred_element_type=F32)

        def head_norm(yc, gain, rows=rows):
            ss = jnp.dot((yc * yc).astype(BF16), bd, preferred_element_type=F32)
            t = yc * lax.rsqrt(ss * (1.0 / HEAD_DIM) + EPS) * gain
            if rope:
                partner = jnp.where(even, pltpu.roll(t, LANES - 1, 1), pltpu.roll(t, 1, 1))
                t = t * cos_ref[rows, :] + partner * sin_ref[rows, :]
            return t

        for col0, q_ref, k_ref, v_ref, grow in groups:
            gq = gains_ref[grow:grow + 1, :]
            gk = gains_ref[grow + 1:grow + 2, :]
            for c in range(Q_WIDTH // LANES):
                t = head_norm(y[:, col0 + c * LANES:col0 + (c + 1) * LANES], gq)
                q_ref[c * LANES:(c + 1) * LANES, rows] = t.T.astype(BF16)
            kcol = col0 + Q_WIDTH
            k_ref[rows, 0:KV_WIDTH] = head_norm(y[:, kcol:kcol + KV_WIDTH], gk).astype(BF16)
            k_ref[rows, KV_WIDTH:K_AUG] = jnp.where(lane == 0, 1.0, 0.0).astype(BF16)
            vt = y[:, kcol + KV_WIDTH:kcol + 2 * KV_WIDTH].T.astype(BF16)
            for kv in range(N_KV):
                r0 = kv * V_ROWS
                v_ref[j, r0:r0 + HEAD_DIM, :] = vt[kv * HEAD_DIM:(kv + 1) * HEAD_DIM, :]
                v_ref[j, r0 + HEAD_DIM:r0 + V_ROWS, :] = ones_rows


def _attn_in(x, g, sc, sh, w_bf16, gains, cos_t, sin_t, bd, *, rope):
    b, t, d = x.shape
    tm = _pick(t, (512, 256))
    nch = t // CHUNK
    kern = functools.partial(_attn_in_kernel, rope=rope, tm=tm)
    q_spec = pl.BlockSpec((None, Q_WIDTH, tm), lambda bb, i: (bb, 0, i))
    k_spec = pl.BlockSpec((None, tm, K_AUG), lambda bb, i: (bb, i, 0))
    v_spec = pl.BlockSpec((None, tm // CHUNK, N_KV * V_ROWS, CHUNK), lambda bb, i: (bb, i, 0, 0))
    q_shape = jax.ShapeDtypeStruct((b, Q_WIDTH, t), BF16)
    k_shape = jax.ShapeDtypeStruct((b, t, K_AUG), BF16)
    v_shape = jax.ShapeDtypeStruct((b, nch, N_KV * V_ROWS, CHUNK), BF16)
    return pl.pallas_call(
        kern,
        grid=(b, t // tm),
        in_specs=[
            pl.BlockSpec((None, tm, d), lambda bb, i: (bb, i, 0)),
            pl.BlockSpec((1, d), lambda bb, i: (0, 0)),
            _vec_spec(sc), _vec_spec(sh),
            pl.BlockSpec((d, IN_PROJ_WIDTH), lambda bb, i: (0, 0)),
            pl.BlockSpec((4, LANES), lambda bb, i: (0, 0)),
            pl.BlockSpec((tm, LANES), lambda bb, i: (i, 0)),
            pl.BlockSpec((tm, LANES), lambda bb, i: (i, 0)),
            pl.BlockSpec((LANES, LANES), lambda bb, i: (0, 0)),
        ],
        out_specs=[q_spec, k_spec, v_spec, q_spec, k_spec, v_spec],
        out_shape=[q_shape, k_shape, v_shape, q_shape, k_shape, v_shape],
        compiler_params=_params("parallel", "parallel"),
        name="attn_in",
    )(x, g, sc, sh, w_bf16, gains, cos_t, sin_t, bd)


def _attn_kernel(q_ref, k_ref, v_ref, sink_ref, o_ref, qpad_ref, s_ref, cmax_ref, m_ref, l_ref,
                 acc_ref, *, mode, use_sink, n_lat, tq):
    i = pl.program_id(1)
    w = GROUP * tq
    zeros = jnp.zeros((HEAD_DIM, w), BF16)
    for kv in range(N_KV):
        heads = range(kv * GROUP, (kv + 1) * GROUP)
        q4 = jnp.concatenate([q_ref[h * HEAD_DIM:(h + 1) * HEAD_DIM, :] for h in heads], axis=1)
        qpad_ref[kv] = jnp.concatenate([q4, zeros] if kv == 0 else [zeros, q4], axis=0)
        if use_sink:
            m_ref[kv] = jnp.concatenate(
                [jnp.full((1, tq), sink_ref[h] * LOG2E, F32) for h in heads], axis=1)
            l_ref[kv] = jnp.ones((1, w), F32)
        else:
            m_ref[kv] = jnp.full((1, w), NEG_BIG, F32)
            l_ref[kv] = jnp.zeros((1, w), F32)
        acc_ref[kv] = jnp.zeros((HEAD_DIM, w), F32)

    def scores(c, slot, mask):
        start = c * CHUNK if isinstance(c, int) else pl.multiple_of(c * CHUNK, CHUNK)
        kc = k_ref[pl.ds(start, CHUNK), 0:KV_WIDTH]
        for kv in range(N_KV):
            s = jnp.dot(kc, qpad_ref[kv], preferred_element_type=F32)
            if mask is not None:
                s = jnp.where(mask, s, -jnp.inf)
            s_ref[slot, kv] = s
            cmax_ref[slot, kv] = jnp.max(s, axis=0, keepdims=True)

    def absorb(c, slot):
        for kv in range(N_KV):
            m = m_ref[kv]
            m_new = jnp.maximum(m, cmax_ref[slot, kv])
            alpha = jnp.exp2(m - m_new)
            p = jnp.exp2(s_ref[slot, kv] - m_new)
            l_ref[kv] = alpha * l_ref[kv] + jnp.sum(p, axis=0, keepdims=True)
            m_ref[kv] = m_new
            vc = v_ref[c, kv * V_ROWS:kv * V_ROWS + HEAD_DIM, :]
            acc_ref[kv] = alpha * acc_ref[kv] + jnp.dot(vc, p.astype(BF16),
                                                        preferred_element_type=F32)

    scores(0, 0, None)
    if mode == "global":
        def body(j, carry):
            c = 2 * j
            scores(c + 1, 1, None)
            absorb(c, 0)
            scores(c + 2, 0, None)
            absorb(c + 1, 1)
            return carry
        lax.fori_loop(0, n_lat // 2, body, 0)
        c_end = 2 * (n_lat // 2)
        if n_lat % 2:
            scores(c_end + 1, 1, None)
            absorb(c_end, 0)
            absorb(c_end + 1, 1)
        else:
            absorb(c_end, 0)
    elif mode == "window":
        row = lax.broadcasted_iota(jnp.int32, (CHUNK, w), 0)
        col = lax.broadcasted_iota(jnp.int32, (CHUNK, w), 1) & (tq - 1)
        rel0 = col - row
        prev_c, prev_slot = 0, 0
        for d in (-1, 0, 1):
            cl = i + d
            ok = jnp.logical_and(cl >= 0, cl < n_lat)
            c = 1 + jnp.clip(cl, 0, n_lat - 1)
            mask = jnp.logical_and(jnp.abs(rel0 - d * CHUNK) <= WINDOW, ok)
            scores(c, 1 - prev_slot, mask)
            absorb(prev_c, prev_slot)
            prev_c, prev_slot = c, 1 - prev_slot
        absorb(prev_c, prev_slot)
    else:
        absorb(0, 0)

    for kv in range(N_KV):
        out = acc_ref[kv] * (1.0 / l_ref[kv])
        for g in range(GROUP):
            h = kv * GROUP + g
            o_ref[h * HEAD_DIM:(h + 1) * HEAD_DIM, :] = out[:, g * tq:(g + 1) * tq].astype(BF16)


def _attention(q_t, k_all, v_all, sink, bound, *, mode, use_sink):
    del bound
    b, _, tq_total = q_t.shape
    nk = k_all.shape[1]
    nc = v_all.shape[1]
    tq = CHUNK
    kern = functools.partial(_attn_kernel, mode=mode, use_sink=use_sink, n_lat=nc - 1, tq=tq)
    return pl.pallas_call(
        kern,
        grid=(b, tq_total // tq),
        in_specs=[
            pl.BlockSpec((None, Q_WIDTH, tq), lambda bb, i: (bb, 0, i)),
            pl.BlockSpec((None, nk, K_AUG), lambda bb, i: (bb, 0, 0)),
            pl.BlockSpec((None, nc, N_KV * V_ROWS, CHUNK), lambda bb, i: (bb, 0, 0, 0)),
            pl.BlockSpec(memory_space=pltpu.SMEM),
        ],
        out_specs=pl.BlockSpec((None, Q_WIDTH, tq), lambda bb, i: (bb, 0, i)),
        out_shape=jax.ShapeDtypeStruct((b, Q_WIDTH, tq_total), BF16),
        scratch_shapes=[pltpu.VMEM((N_KV, KV_WIDTH, GROUP * tq), BF16),
                        pltpu.VMEM((2, N_KV, CHUNK, GROUP * tq), F32),
                        pltpu.VMEM((2, N_KV, 1, GROUP * tq), F32),
                        pltpu.VMEM((N_KV, 1, GROUP * tq), F32),
                        pltpu.VMEM((N_KV, 1, GROUP * tq), F32),
                        pltpu.VMEM((N_KV, HEAD_DIM, GROUP * tq), F32)],
        compiler_params=_params("parallel", "parallel"),
        name="attn_" + mode,
    )(q_t, k_all, v_all, sink)


def _attn_fast_kernel(q_ref, k_ref, v_ref, sink_ref, bound_ref, o_ref, qa_ref, acc_ref, p_ref, *,
                      mode, use_sink, n_lat, tq):
    i = pl.program_id(1)
    w = GROUP * tq
    bound = bound_ref[0]
    zeros = jnp.zeros((HEAD_DIM, w), BF16)
    row = lax.broadcasted_iota(jnp.int32, (KV_WIDTH, w), 0)
    shift = jnp.where(row == 0, -bound, 0.0).astype(BF16)
    for kv in range(N_KV):
        heads = range(kv * GROUP, (kv + 1) * GROUP)
        q4 = jnp.concatenate([q_ref[h * HEAD_DIM:(h + 1) * HEAD_DIM, :] for h in heads], axis=1)
        qa_ref[kv] = jnp.concatenate(([q4, zeros] if kv == 0 else [zeros, q4]) + [shift], axis=0)
        acc_ref[kv] = jnp.zeros((V_ROWS, w), F32)

    def update(c, mask, n=1):
        start = c * CHUNK if isinstance(c, int) else pl.multiple_of(c * CHUNK, CHUNK)
        kc = k_ref[pl.ds(start, n * CHUNK), :]
        for kv in range(N_KV):
            s = jnp.dot(kc, qa_ref[kv], preferred_element_type=F32)
            if mask is not None:
                s = jnp.where(mask, s, -jnp.inf)
            p = jnp.exp2(s).astype(BF16)
            vc = jnp.concatenate([v_ref[c + j, kv * V_ROWS:(kv + 1) * V_ROWS, :] for j in range(n)],
                                 axis=1)
            acc_ref[kv] += jnp.dot(vc, p, preferred_element_type=F32)

    def probs(c, n, slot):
        start = c * CHUNK if isinstance(c, int) else pl.multiple_of(c * CHUNK, CHUNK)
        kc = k_ref[pl.ds(start, n * CHUNK), :]
        for kv in range(N_KV):
            s = jnp.dot(kc, qa_ref[kv], preferred_element_type=F32)
            p_ref[slot, kv, 0:n * CHUNK, :] = jnp.exp2(s).astype(BF16)

    def values(c, n, slot):
        for kv in range(N_KV):
            vc = jnp.concatenate([v_ref[c + j, kv * V_ROWS:(kv + 1) * V_ROWS, :] for j in range(n)],
                                 axis=1)
            acc_ref[kv] += jnp.dot(vc, p_ref[slot, kv, 0:n * CHUNK, :], preferred_element_type=F32)

    if mode == "global":
        n_full = (1 + n_lat) // GLOBAL_STEP
        steps = [(GLOBAL_STEP * s, GLOBAL_STEP) for s in range(n_full)]
        steps += [(c, 1) for c in range(GLOBAL_STEP * n_full, 1 + n_lat)]
        n_pairs = max((n_full - 1) // 2, 0)

        probs(*steps[0], 0)

        def body(t, carry):
            c = 2 * GLOBAL_STEP * t
            probs(c + GLOBAL_STEP, GLOBAL_STEP, 1)
            values(c, GLOBAL_STEP, 0)
            probs(c + 2 * GLOBAL_STEP, GLOBAL_STEP, 0)
            values(c + GLOBAL_STEP, GLOBAL_STEP, 1)
            return carry
        lax.fori_loop(0, n_pairs, body, 0)
        pending, slot = steps[2 * n_pairs], 0
        for nxt in steps[2 * n_pairs + 1:]:
            probs(*nxt, 1 - slot)
            values(*pending, slot)
            pending, slot = nxt, 1 - slot
        values(*pending, slot)
    else:
        update(0, None)
    if mode == "window":
        rowk =lax.broadcasted_iota(jnp.int32, (CHUNK, w), 0)
        col = lax.broadcasted_iota(jnp.int32, (CHUNK, w), 1) & (tq - 1)
        rel0 = col - rowk
        for d in (-1, 0, 1):
            cl = i + d
            ok = jnp.logical_and(cl >= 0, cl < n_lat)
            c = 1 + jnp.clip(cl, 0, n_lat - 1)
            update(c, jnp.logical_and(jnp.abs(rel0 - d * CHUNK) <= WINDOW, ok))

    for kv in range(N_KV):
        acc = acc_ref[kv]
        l = acc[HEAD_DIM:HEAD_DIM + 1, :]
        if use_sink:
            l = l + jnp.concatenate(
                [jnp.full((1, tq), jnp.exp2(sink_ref[kv * GROUP + g] * LOG2E - bound), F32)
                 for g in range(GROUP)], axis=1)
        out = acc[0:HEAD_DIM, :] * (1.0 / l)
        for g in range(GROUP):
            h = kv * GROUP + g
            o_ref[h * HEAD_DIM:(h + 1) * HEAD_DIM, :] = out[:, g * tq:(g + 1) * tq].astype(BF16)


def _attention_fast(q_t, k_all, v_all, sink, bound, *, mode, use_sink):
    b, _, tq_total = q_t.shape
    nk = k_all.shape[1]
    nc = v_all.shape[1]
    tq = CHUNK
    kern = functools.partial(_attn_fast_kernel, mode=mode, use_sink=use_sink, n_lat=nc - 1, tq=tq)
    return pl.pallas_call(
        kern,
        grid=(b, tq_total // tq),
        in_specs=[
            pl.BlockSpec((None, Q_WIDTH, tq), lambda bb, i: (bb, 0, i)),
            pl.BlockSpec((None, nk, K_AUG), lambda bb, i: (bb, 0, 0)),
            pl.BlockSpec((None, nc, N_KV * V_ROWS, CHUNK), lambda bb, i: (bb, 0, 0, 0)),
            pl.BlockSpec(memory_space=pltpu.SMEM),
            pl.BlockSpec(memory_space=pltpu.SMEM),
        ],
        out_specs=pl.BlockSpec((None, Q_WIDTH, tq), lambda bb, i: (bb, 0, i)),
        out_shape=jax.ShapeDtypeStruct((b, Q_WIDTH, tq_total), BF16),
        scratch_shapes=[pltpu.VMEM((N_KV, K_AUG, GROUP * tq), BF16),
                        pltpu.VMEM((N_KV, V_ROWS, GROUP * tq), F32),
                        pltpu.VMEM((2, N_KV, (GLOBAL_STEP if mode == "global" else 1) * CHUNK,
                                    GROUP * tq), BF16)],
        compiler_params=_params("parallel", "parallel"),
        name="attn_fast_" + mode,
    )(q_t, k_all, v_all, sink, bound)


def _attend(fast_ok, *args, **kw):
    return lax.cond(fast_ok, functools.partial(_attention_fast, **kw),
                    functools.partial(_attention, **kw), *args)


_TN_DIMS = (((0,), (0,)), ((), ()))


def _attn_out_kernel(oa_ref, ob_ref, w_ref, x_ref, g1_ref, n2_ref, sc2_ref, sh2_ref, x1_ref, h2_ref):
    mix = lax.dot_general(oa_ref[...], w_ref[0:Q_WIDTH, :], _TN_DIMS, preferred_element_type=F32)
    mix = mix + lax.dot_general(ob_ref[...], w_ref[Q_WIDTH:2 * Q_WIDTH, :], _TN_DIMS,
                                preferred_element_type=F32)
    x1 = x_ref[...] + g1_ref[...] * mix
    x1_ref[...] = x1
    h2_ref[...] = _norm_mod(x1, n2_ref[...], sc2_ref[...], sh2_ref[...]).astype(BF16)


def _attn_out(oa, ob, w_bf16, x, g1, n2, sc2, sh2):
    b, t, d = x.shape
    tm = _pick(t, (512, 256))
    o_spec = pl.BlockSpec((None, Q_WIDTH, tm), lambda bb, i: (bb, 0, i))
    x_spec = pl.BlockSpec((None, tm, d), lambda bb, i: (bb, i, 0))
    return pl.pallas_call(
        _attn_out_kernel,
        grid=(b, t // tm),
        in_specs=[o_spec, o_spec,
                  pl.BlockSpec((2 * Q_WIDTH, d), lambda bb, i: (0, 0)),
                  x_spec, _vec_spec(g1),
                  pl.BlockSpec((1, d), lambda bb, i: (0, 0)),
                  _vec_spec(sc2), _vec_spec(sh2)],
        out_specs=[x_spec, x_spec],
        out_shape=[jax.ShapeDtypeStruct((b, t, d), F32), jax.ShapeDtypeStruct((b, t, d), BF16)],
        compiler_params=_params("parallel", "parallel"),
        name="attn_out",
    )(oa, ob, w_bf16, x, g1, n2, sc2, sh2)


def _ffn_kernel(*refs, gated, tm):
    if gated:
        h_ref, wg_ref, wu_ref, wd_ref, gate_ref, x_ref, g2_ref, o_ref, acc_ref = refs
    else:
        h_ref, wg_ref, wu_ref, wd_ref, x_ref, g2_ref, o_ref, acc_ref = refs
    e = pl.program_id(2)
    f = pl.program_id(3)
    first = jnp.logical_and(e == 0, f == 0)
    last = jnp.logical_and(e == pl.num_programs(2) - 1, f == pl.num_programs(3) - 1)

    @pl.when(first)
    def _():
        acc_ref[...] = jnp.zeros_like(acc_ref)

    h = h_ref[...]
    a = jnp.dot(h, wg_ref[...].astype(BF16), preferred_element_type=F32)
    u = jnp.dot(h, wu_ref[...].astype(BF16), preferred_element_type=F32)
    act = (a * _sigmoid(a) * u).astype(BF16)
    y = jnp.dot(act, wd_ref[...].astype(BF16), preferred_element_type=F32)
    if gated:
        lane = lax.broadcasted_iota(jnp.int32, (tm, LANES), 1)
        ge = jnp.sum(jnp.where(lane == e, gate_ref[...], 0.0), axis=-1, keepdims=True)
        y = y * ge
    acc_ref[...] += y

    @pl.when(last)
    def _():
        o_ref[...] = x_ref[...] + g2_ref[...] * acc_ref[...]


def _ffn(h2, wg, wu, wd, gates, x1, g2, e0=0, n_e=1):
    b, t, d = x1.shape
    f = wg.shape[-1]
    gated = gates is not None
    tm = _pick(t, (1024, 512, 256) if gated else (512, 256))
    resident = n_e == 1 and 3 * d * f * 2 <= VMEM_LIMIT_BYTES // 3
    tf = f if resident else _pick(f, (512, 256))
    mode = dict(pipeline_mode=pl.Buffered(1)) if resident else {}
    x_spec = pl.BlockSpec((None, tm, d), lambda bb, i, e, j: (bb, i, 0))
    in_specs = [x_spec,
                pl.BlockSpec((None, d, tf), lambda bb, i, e, j: (e0 + e, 0, j), **mode),
                pl.BlockSpec((None, d, tf), lambda bb, i, e, j: (e0 + e, 0, j), **mode),
                pl.BlockSpec((None, tf, d), lambda bb, i, e, j: (e0 + e, j, 0), **mode)]
    args = [h2, wg, wu, wd]
    if gated:
        in_specs.append(pl.BlockSpec((None, tm, LANES), lambda bb, i, e, j: (bb, i, 0)))
        args.append(gates)
    in_specs += [x_spec, _vec_spec(g2)]
    args += [x1, g2]
    return pl.pallas_call(
        functools.partial(_ffn_kernel, gated=gated, tm=tm),
        grid=(b, t // tm, n_e, f // tf),
        in_specs=in_specs,
        out_specs=x_spec,
        out_shape=jax.ShapeDtypeStruct((b, t, d), F32),
        scratch_shapes=[pltpu.VMEM((tm, d), F32)],
        compiler_params=_params("parallel", "parallel", "arbitrary", "arbitrary"),
        name="ffn_gated" if gated else "ffn_dense",
    )(*args)


def _conv_in_kernel(x_ref, g_ref, sc_ref, sh_ref, w_ref, bg_ref, z_ref, *, d):
    h = _norm_mod(x_ref[...], g_ref[...], sc_ref[...], sh_ref[...]).astype(BF16)
    y = jnp.dot(h, w_ref[...], preferred_element_type=F32)
    bg_ref[...] = y[:, 0:d].astype(BF16)
    z_ref[...] = (y[:, d:2 * d] * y[:, 2 * d:3 * d]).astype(BF16)


def _conv_in(x, g, sc, sh, w_bf16):
    b, t, d = x.shape
    tm = _pick(t, (512, 256))
    x_spec = pl.BlockSpec((None, tm, d), lambda bb, i: (bb, i, 0))
    return pl.pallas_call(
        functools.partial(_conv_in_kernel, d=d),
        grid=(b, t // tm),
        in_specs=[x_spec, pl.BlockSpec((1, d), lambda bb, i: (0, 0)), _vec_spec(sc), _vec_spec(sh),
                  pl.BlockSpec((d, 3 * d), lambda bb, i: (0, 0))],
        out_specs=[x_spec, x_spec],
        out_shape=[jax.ShapeDtypeStruct((b, t, d), BF16), jax.ShapeDtypeStruct((b, t, d), BF16)],
        compiler_params=_params("parallel", "parallel"),
        name="conv_in",
    )(x, g, sc, sh, w_bf16)


HALO = 16


def _conv_out_kernel(z_ref, zp_ref, zn_ref, bg_ref, cw_ref, w_ref, x_ref, g1_ref, n2_ref, sc2_ref,
                     sh2_ref, rw_ref, x1_ref, h2_ref, gate_ref, *, tm, n_experts):
    i = pl.program_id(1)
    z = z_ref[...].astype(F32)
    prev = jnp.where(i > 0, zp_ref[HALO - 1:HALO, :].astype(F32), 0.0)
    nxt = jnp.where(i < pl.num_programs(1) - 1, zn_ref[0:1, :].astype(F32), 0.0)
    row = lax.broadcasted_iota(jnp.int32, z.shape, 0)
    z_dn = jnp.where(row == 0, prev, pltpu.roll(z, 1, 0))
    z_up = jnp.where(row == tm - 1, nxt, pltpu.roll(z, tm - 1, 0))
    conv = z_dn * cw_ref[0:1, :] + z * cw_ref[1:2, :] + z_up * cw_ref[2:3, :]
    v = (bg_ref[...].astype(F32) * conv).astype(BF16)
    mix = jnp.dot(v, w_ref[...], preferred_element_type=F32)
    x1 = x_ref[...] + g1_ref[...] * mix
    x1_ref[...] = x1
    h2 = _norm_mod(x1, n2_ref[...], sc2_ref[...], sh2_ref[...])
    h2_ref[...] = h2.astype(h2_ref.dtype)

    h_hi = h2.astype(BF16)
    h_lo = (h2 - h_hi.astype(F32)).astype(BF16)
    logits = (jnp.dot(h_hi, rw_ref[0], preferred_element_type=F32)
              + jnp.dot(h_lo, rw_ref[0], preferred_element_type=F32)
              + jnp.dot(h_hi, rw_ref[1], preferred_element_type=F32))
    lane = lax.broadcasted_iota(jnp.int32, logits.shape, 1)
    lanef = lane.astype(F32)
    logits = jnp.where(lane < n_experts, logits, -jnp.inf)
    m1 = jnp.max(logits, axis=-1, keepdims=True)
    i1 = jnp.min(jnp.where(logits == m1, lanef, float(LANES)), axis=-1, keepdims=True)
    rest = jnp.where(lanef == i1, -jnp.inf, logits)
    m2 = jnp.max(rest, axis=-1, keepdims=True)
    i2 = jnp.min(jnp.where(rest == m2, lanef, float(LANES)), axis=-1, keepdims=True)
    e2 = jnp.exp(m2 - m1)
    w1 = 1.0 / (1.0 + e2)
    w2 = e2 * w1
    rec = jnp.where(lanef == i1, w1, 0.0) + jnp.where(lanef == i2, w2, 0.0)
    for k, val in enumerate((i1, i2, w1, w2)):
        rec = jnp.where(lane == n_experts + k, val, rec)
    gate_ref[...] = rec


def _conv_out(z, bg, conv_w, w_bf16, x, g1, n2, sc2, sh2, router_pad, n_experts, h2_dtype):
    b, t, d = x.shape
    tm = _pick(t, (512, 256))
    hb = tm // HALO
    n_halo = t // HALO
    x_spec = pl.BlockSpec((None, tm, d), lambda bb, i: (bb, i, 0))
    prev_spec = pl.BlockSpec((None, HALO, d), lambda bb, i: (bb, jnp.maximum(i * hb - 1, 0), 0))
    next_spec = pl.BlockSpec((None, HALO, d), lambda bb, i: (bb, jnp.minimum((i + 1) * hb, n_halo - 1), 0))
    full = lambda r, c: pl.BlockSpec((r, c), lambda bb, i: (0, 0))
    return pl.pallas_call(
        functools.partial(_conv_out_kernel, tm=tm, n_experts=n_experts),
        grid=(b, t // tm),
        in_specs=[x_spec, prev_spec, next_spec, x_spec, full(3, d), full(d, d), x_spec, _vec_spec(g1),
                  full(1, d), _vec_spec(sc2), _vec_spec(sh2),
                  pl.BlockSpec((2, d, LANES), lambda bb, i: (0, 0, 0))],
        out_specs=[x_spec, x_spec, pl.BlockSpec((None, tm, LANES), lambda bb, i: (bb, i, 0))],
        out_shape=[jax.ShapeDtypeStruct((b, t, d), F32), jax.ShapeDtypeStruct((b, t, d), h2_dtype),
                   jax.ShapeDtypeStruct((b, t, LANES), F32)],
        compiler_params=_params("parallel", "parallel"),
        name="conv_out",
    )(z, z, z, bg, conv_w, w_bf16, x, g1, n2, sc2, sh2, router_pad)


EXPERT_TILE = 1024


def _rank_kernel(route_ref, rank_ref, count_ref, base_ref, *, tm, n_experts):
    @pl.when(pl.program_id(0) == 0)
    def _():
        base_ref[...] = jnp.zeros_like(base_ref)

    route = route_ref[...]
    lane = lax.broadcasted_iota(jnp.int32, (tm, LANES), 1)
    lanef = lane.astype(F32)
    i1 = jnp.sum(jnp.where(lane == n_experts, route, 0.0), axis=-1, keepdims=True)
    i2 = jnp.sum(jnp.where(lane == n_experts + 1, route, 0.0), axis=-1, keepdims=True)
    onehot = jnp.where(lanef == i1, 1.0, 0.0) + jnp.where(lanef == i2, 1.0, 0.0)
    r = lax.broadcasted_iota(jnp.int32, (tm, tm), 0)
    c = lax.broadcasted_iota(jnp.int32, (tm, tm), 1)
    lower = jnp.where(c < r, 1.0, 0.0).astype(BF16)
    before = jnp.dot(lower, onehot.astype(BF16), preferred_element_type=F32) + base_ref[...]
    r1 = jnp.sum(jnp.where(lanef == i1, before, 0.0), axis=-1, keepdims=True)
    r2 = jnp.sum(jnp.where(lanef == i2, before, 0.0), axis=-1, keepdims=True)
    packed = jnp.where(lane == 0, r1, jnp.where(lane == 1, r2, jnp.where(lane == 2, i1,
                       jnp.where(lane == 3, i2, 0.0))))
    rank_ref[...] = packed.T[0:8, :]
    base_ref[...] += jnp.sum(onehot, axis=0, keepdims=True)
    count_ref[...] = base_ref[...]


def _rank(route, n_experts):
    n = route.shape[0]
    tm = _pick(n, (512, 256))
    return pl.pallas_call(
        functools.partial(_rank_kernel, tm=tm, n_experts=n_experts),
        grid=(n // tm,),
        in_specs=[pl.BlockSpec((tm, LANES), lambda i: (i, 0))],
        out_specs=[pl.BlockSpec((8, tm), lambda i: (0, i)), pl.BlockSpec((1, LANES), lambda i: (0, 0))],
        out_shape=[jax.ShapeDtypeStruct((8, n), F32), jax.ShapeDtypeStruct((1, LANES), F32)],
        scratch_shapes=[pltpu.VMEM((1, LANES), F32)],
        compiler_params=_params("arbitrary"),
        name="moe_rank",
    )(route)


def _dispatch_kernel(dest_ref, h_ref, xs_in_ref, xs_ref, sem, *, tm):
    del xs_in_ref

    def row_copy(r, k):
        d = dest_ref[0, k * tm + r]
        return pltpu.make_async_copy(h_ref.at[pl.ds(r, 1), :], xs_ref.at[pl.ds(d, 1), :], sem)

    def issue(r, carry):
        row_copy(r, 0).start()
        row_copy(r, 1).start()
        return carry

    lax.fori_loop(0, tm, issue, 0, unroll=8)

    def drain(r, carry):
        row_copy(r, 0).wait()
        row_copy(r, 1).wait()
        return carry

    lax.fori_loop(0, tm, drain, 0, unroll=8)


def _dispatch(dest, h2, n_rows):
    n, d = h2.shape
    tm = dest.shape[-1] // 2
    zeros = jnp.zeros((n_rows, d), h2.dtype)
    return pl.pallas_call(
        functools.partial(_dispatch_kernel, tm=tm),
        grid=(n // tm,),
        in_specs=[pl.BlockSpec((None, 1, 2 * tm), lambda i: (i, 0, 0), memory_space=pltpu.SMEM),
                  pl.BlockSpec((tm, d), lambda i: (i, 0)),
                  pl.BlockSpec(memory_space=pl.ANY)],
        out_specs=pl.BlockSpec(memory_space=pl.ANY),
        out_shape=jax.ShapeDtypeStruct((n_rows, d), h2.dtype),
        scratch_shapes=[pltpu.SemaphoreType.DMA(())],
        input_output_aliases={2: 0},
        compiler_params=_params("arbitrary"),
        name="moe_dispatch",
    )(dest, h2, zeros)


def _expert_kernel(te_ref, nu_ref, xs_ref, wg_ref, wu_ref, wd_ref, o_ref, hb_ref, acc_ref):
    del te_ref
    j = pl.program_id(0)
    f = pl.program_id(1)

    @pl.when(j < nu_ref[0])
    def _():
        @pl.when(f == 0)
        def _():
            hb_ref[...] = xs_ref[...].astype(BF16)
            acc_ref[...] = jnp.zeros_like(acc_ref)

        h = hb_ref[...]
        a = jnp.dot(h, wg_ref[...].astype(BF16), preferred_element_type=F32)
        u = jnp.dot(h, wu_ref[...].astype(BF16), preferred_element_type=F32)
        act = (a * _sigmoid(a) * u).astype(BF16)
        acc_ref[...] += jnp.dot(act, wd_ref[...].astype(BF16), preferred_element_type=F32)

        @pl.when(f == pl.num_programs(1) - 1)
        def _():
            o_ref[...] = acc_ref[...]

    @pl.when(jnp.logical_and(j >= nu_ref[0], f == pl.num_programs(1) - 1))
    def _():
        o_ref[...] = jnp.zeros_like(o_ref)


def _experts(tile_expert, n_used, xs, wg, wu, wd):
    n_rows, d = xs.shape
    f = wg.shape[-1]
    tf = _pick(f, (512, 256))
    nf = f // tf
    n_tiles = n_rows // EXPERT_TILE

    def row_map(j, ff, te, nu):
        return (jnp.minimum(j, nu[0] - 1), 0)

    def w_up_map(j, ff, te, nu):
        live = j < nu[0]
        return (te[jnp.minimum(j, nu[0] - 1)], 0, jnp.where(live, ff, nf - 1))

    def w_down_map(j, ff, te, nu):
        live = j < nu[0]
        return (te[jnp.minimum(j, nu[0] - 1)], jnp.where(live, ff, nf - 1), 0)

    grid_spec = pltpu.PrefetchScalarGridSpec(
        num_scalar_prefetch=2,
        grid=(n_tiles, nf),
        in_specs=[pl.BlockSpec((EXPERT_TILE, d), row_map),
                  pl.BlockSpec((None, d, tf), w_up_map),
                  pl.BlockSpec((None, d, tf), w_up_map),
                  pl.BlockSpec((None, tf, d), w_down_map)],
        out_specs=pl.BlockSpec((EXPERT_TILE, d), lambda j, ff, te, nu: (j, 0)),
        scratch_shapes=[pltpu.VMEM((EXPERT_TILE, d), BF16), pltpu.VMEM((EXPERT_TILE, d), F32)],
    )
    return pl.pallas_call(
        _expert_kernel,
        grid_spec=grid_spec,
        out_shape=jax.ShapeDtypeStruct((n_rows, d), F32),
        compiler_params=_params("arbitrary", "arbitrary"),
        name="moe_experts",
    )(tile_expert, n_used, xs, wg, wu, wd)


def _combine_kernel(dest_ref, os_ref, route_ref, x_ref, g2_ref, o_ref, buf_ref, sem, *, tm, n_experts):
    def row_copy(r, k):
        d = dest_ref[0, k * tm + r]
        return pltpu.make_async_copy(os_ref.at[pl.ds(d, 1), :], buf_ref.at[k, pl.ds(r, 1), :], sem)

    def issue(r, carry):
        row_copy(r, 0).start()
        row_copy(r, 1).start()
        return carry

    lax.fori_loop(0, tm, issue, 0, unroll=8)

    def drain(r, carry):
        row_copy(r, 0).wait()
        row_copy(r, 1).wait()
        return carry

    lax.fori_loop(0, tm, drain, 0, unroll=8)

    route = route_ref[...]
    lane = lax.broadcasted_iota(jnp.int32, (tm, LANES), 1)
    w1 = jnp.sum(jnp.where(lane == n_experts + 2, route, 0.0), axis=-1, keepdims=True)
    w2 = jnp.sum(jnp.where(lane == n_experts + 3, route, 0.0), axis=-1, keepdims=True)
    o_ref[...] = x_ref[...] + g2_ref[...] * (w1 * buf_ref[0] + w2 * buf_ref[1])


def _combine(dest, os, route, x1, g2, n_experts):
    b, t, d = x1.shape
    tm = dest.shape[-1] // 2
    tiles_per_seq = t // tm
    x_spec = pl.BlockSpec((None, tm, d), lambda bb, i: (bb, i, 0))
    return pl.pallas_call(
        functools.partial(_combine_kernel, tm=tm, n_experts=n_experts),
        grid=(b, tiles_per_seq),
        in_specs=[pl.BlockSpec((None, 1, 2 * tm), lambda bb, i: (bb * tiles_per_seq + i, 0, 0),
                               memory_space=pltpu.SMEM),
                  pl.BlockSpec(memory_space=pl.ANY),
                  pl.BlockSpec((None, tm, LANES), lambda bb, i: (bb, i, 0)),
                  x_spec, _vec_spec(g2)],
        out_specs=x_spec,
        out_shape=jax.ShapeDtypeStruct((b, t, d), F32),
        scratch_shapes=[pltpu.VMEM((2, tm, d), F32), pltpu.SemaphoreType.DMA(())],
        compiler_params=_params("arbitrary", "arbitrary"),
        name="moe_combine",
    )(dest, os, route, x1, g2)


def _moe_routed(h2, route, x1, g2, wg, wu, wd, e0, n_experts):
    b, t, d = x1.shape
    n = b * t
    tm = _pick(t, (512, 256))
    rec = route.reshape(n, LANES)
    rank, counts = _rank(rec, n_experts)

    cnt = counts[0, :n_experts].astype(jnp.int32)
    padded = ((cnt + EXPERT_TILE - 1) // EXPERT_TILE) * EXPERT_TILE
    ends = jnp.cumsum(padded)
    offs = ends - padded
    n_tiles = (2 * n) // EXPERT_TILE + n_experts
    n_used = (ends[-1] // EXPERT_TILE).astype(jnp.int32).reshape(1)
    tile_start = jnp.arange(n_tiles, dtype=jnp.int32) * EXPERT_TILE
    tile_expert = jnp.minimum(jnp.sum(tile_start[:, None] >= ends[None, :], axis=1), n_experts - 1)
    slot = rank.astype(jnp.int32)
    d1 = jnp.take(offs, slot[2]) + slot[0]
    d2 = jnp.take(offs, slot[3]) + slot[1]
    dest = jnp.concatenate([d1.reshape(n // tm, tm), d2.reshape(n // tm, tm)], axis=1)[:, None, :]

    xs = _dispatch(dest, h2.reshape(n, d), n_tiles * EXPERT_TILE)
    os = _experts((e0 + tile_expert).astype(jnp.int32), n_used, xs, wg, wu, wd)
    return _combine(dest, os, route, x1, g2, n_experts)


def _rope_tables(n_tokens):
    rows = n_tokens // GRID_W
    row, col = jnp.meshgrid(jnp.arange(rows, dtype=F32), jnp.arange(GRID_W, dtype=F32), indexing="ij")
    half = HEAD_DIM // 2
    inv_freq = ROPE_THETA ** (-jnp.arange(0, half, 2, dtype=F32) / half)
    ang = jnp.concatenate([row.reshape(-1, 1) * inv_freq, col.reshape(-1, 1) * inv_freq], axis=-1)
    cos = jnp.repeat(jnp.cos(ang), 2, axis=-1)
    sign = jnp.tile(jnp.array([-1.0, 1.0], F32), HEAD_DIM // 2)
    sin = jnp.repeat(jnp.sin(ang), 2, axis=-1) * sign
    reps = LANES // HEAD_DIM
    return jnp.tile(cos, (1, reps)), jnp.tile(sin, (1, reps))


def kernel(x, c, ctx, c_ctx, ada_w, ada_b, norm1_g, norm2_g, attn_w_in, attn_w_out, qnorm_a, knorm_a, qnorm_b, knorm_b, sink_b, ffn_w_gate, ffn_w_up, ffn_w_down, conv_w_in, conv_w, conv_w_out, router_w, moe_w_gate, moe_w_up, moe_w_down):
    bsz, n_tok, d = x.shape
    n_ctx = ctx.shape[1]
    depth = ada_w.shape[0]
    n_experts = router_w.shape[-1]
    assert n_ctx == CHUNK and n_tok % CHUNK == 0 and bsz + 1 <= 8
    assert attn_w_in.shape[-1] == IN_PROJ_WIDTH and d % LANES == 0 and n_experts + 4 <= LANES

    cc = jnp.zeros((8, d), F32).at[:bsz].set(c).at[bsz].set(c_ctx)
    mod = _modulation(cc, ada_w, ada_b)

    def mod_vec(layer, j, is_ctx):
        m = mod[layer, :, j * d:(j + 1) * d]
        return m[bsz:bsz + 1].reshape(1, 1, d) if is_ctx else m[:bsz].reshape(bsz, 1, d)

    cos_l, sin_l = _rope_tables(n_tok)
    cos_c = jnp.ones((n_ctx, LANES), F32)
    sin_c = jnp.zeros((n_ctx, LANES), F32)
    head_id = jnp.arange(LANES) // HEAD_DIM
    bd = (head_id[:, None] == head_id[None, :]).astype(BF16)
    no_sink = jnp.zeros((N_HEADS,), F32)
    tile2 = lambda v: jnp.tile(v, LANES // HEAD_DIM)

    xc = ctx
    for layer in range(depth):
        i = layer // 2
        ctx_needed = any(j % 2 == 0 for j in range(layer + 1, depth))
        n1 = norm1_g[layer].reshape(1, d)
        n2 = norm2_g[layer].reshape(1, d)
        mv = lambda j, is_ctx: mod_vec(layer, j, is_ctx)

        if layer % 2 == 0:
            w_in = attn_w_in[i].astype(BF16)
            w_out = attn_w_out[i].astype(BF16)
            qscale = ATTN_SCALE * LOG2E
            gains = jnp.stack([tile2(qnorm_a[i]) * qscale, tile2(knorm_a[i]),
                               tile2(qnorm_b[i]) * qscale, tile2(knorm_b[i])]).astype(F32)
            sink = sink_b[i].astype(F32)
            def logit_bound(gq, gk):
                raw = HEAD_DIM * jnp.max(jnp.abs(gq)) * jnp.max(jnp.abs(gk)) * qscale * 1.02
                return raw.astype(BF16).astype(F32)
            bound_a = logit_bound(qnorm_a[i], knorm_a[i])
            bound_b = jnp.maximum(logit_bound(qnorm_b[i], knorm_b[i]),
                                  (jnp.max(sink) * LOG2E * 1.02).astype(BF16).astype(F32))
            fast_a = bound_a <= FAST_MAX_LOGIT
            fast_b = jnp.logical_and(bound_b <= FAST_MAX_LOGIT,
                                     jnp.max(jnp.abs(sink)) * LOG2E <= FAST_MAX_LOGIT)
            bound_a = bound_a.reshape(1)
            bound_b = bound_b.reshape(1)
            qa_l, ka_l, va_l, qb_l, kb_l, vb_l = _attn_in(
                x, n1, mv(1, False), mv(0, False), w_in, gains, cos_l, sin_l, bd, rope=True)
            qa_c, ka_c, va_c, qb_c, kb_c, vb_c = _attn_in(
                xc, n1, mv(1, True), mv(0, True), w_in, gains, cos_c, sin_c, bd, rope=False)
            ka_all = jnp.concatenate([ka_c, ka_l], axis=1)
            va_all = jnp.concatenate([va_c, va_l], axis=1)
            kb_all = jnp.concatenate([kb_c, kb_l], axis=1)
            vb_all = jnp.concatenate([vb_c, vb_l], axis=1)
            oa = _attend(fast_a, qa_l, ka_all, va_all, no_sink, bound_a, mode="global", use_sink=False)
            ob = _attend(fast_b, qb_l, kb_all, vb_all, sink, bound_b, mode="window", use_sink=True)
            x, h2 = _attn_out(oa, ob, w_out, x, mv(2, False), n2, mv(4, False), mv(3, False))
            if ctx_needed:
                oa_c = _attend(fast_a, qa_c, ka_c, va_c, no_sink, bound_a, mode="ctx", use_sink=False)
                ob_c = _attend(fast_b, qb_c, kb_c, vb_c, sink, bound_b, mode="ctx", use_sink=True)
                xc, h2c = _attn_out(oa_c, ob_c, w_out, xc, mv(2, True), n2, mv(4, True), mv(3, True))
            wg = ffn_w_gate.astype(BF16)
            wu = ffn_w_up.astype(BF16)
            wd = ffn_w_down.astype(BF16)
            x = _ffn(h2, wg, wu, wd, None, x, mv(5, False), e0=i)
            if ctx_needed:
                xc = _ffn(h2c, wg, wu, wd, None, xc, mv(5, True), e0=i)
        else:
            w_in = conv_w_in[i].astype(BF16)
            w_out = conv_w_out[i].astype(BF16)
            cw = conv_w[i].astype(F32)
            router_f32 = jnp.zeros((d, LANES), F32).at[:, :n_experts].set(router_w[i])
            router_hi = router_f32.astype(BF16)
            router_pad = jnp.stack([router_hi, (router_f32 - router_hi.astype(F32)).astype(BF16)])
            wg = moe_w_gate.reshape((-1,) + moe_w_gate.shape[2:])
            wu = moe_w_up.reshape((-1,) + moe_w_up.shape[2:])
            wd = moe_w_down.reshape((-1,) + moe_w_down.shape[2:])
            e0 = i * n_experts
            bg, z = _conv_in(x, n1, mv(1, False), mv(0, False), w_in)
            x, h2, route = _conv_out(z, bg, cw, w_out, x, mv(2, False), n2, mv(4, False), mv(3, False),
                                     router_pad, n_experts, F32)
            x = _moe_routed(h2, route, x, mv(5, False), wg, wu, wd, e0, n_experts)
            if ctx_needed:
                bg, z = _conv_in(xc, n1, mv(1, True), mv(0, True), w_in)
                xc, h2c, gates_c = _conv_out(z, bg, cw, w_out, xc, mv(2, True), n2, mv(4, True),
                                             mv(3, True), router_pad, n_experts, BF16)
                flat = lambda a: a.reshape((1, bsz * n_ctx) + a.shape[2:])
                xc = _ffn(flat(h2c), wg, wu, wd, flat(gates_c), flat(xc), mv(5, True),
                          e0=e0, n_e=n_experts).reshape(bsz, n_ctx, d)
    return x
```

```python
import functools

import jax
import jax.numpy as jnp
from jax import lax
from jax.experimental import pallas as pl
from jax.experimental.pallas import tpu as pltpu

F32 = jnp.float32
BF16 = jnp.bfloat16

HEAD_DIM = 64
N_KV = 2
GROUP = 4
N_HEADS = N_KV * GROUP
Q_WIDTH = N_HEADS * HEAD_DIM
KV_WIDTH = N_KV * HEAD_DIM
IN_PROJ_WIDTH = 2 * (Q_WIDTH + 2 * KV_WIDTH)
GRID_W = 64
WINDOW = 128
ROPE_THETA = 10000.0
ATTN_SCALE = HEAD_DIM ** -0.5
EPS = 1e-6
N_ADA = 6
LOG2E = 1.4426950408889634
LANES = 128
CHUNK = 256
K_AUG = 2 * KV_WIDTH
V_ROWS = HEAD_DIM + 16
GLOBAL_STEP = 4
FAST_MAX_LOGIT = 40.0
NEG_BIG = -1e30
VMEM_LIMIT_BYTES = 56 * 1024 * 1024


def _params(*sem):
    return pltpu.CompilerParams(dimension_semantics=sem, vmem_limit_bytes=VMEM_LIMIT_BYTES)


def _pick(n, candidates):
    for t in candidates:
        if n % t == 0:
            return t
    return n


def _sigmoid(a):
    return 1.0 / (1.0 + jnp.exp(-a))


def _norm_mod(x, g, sc, sh):
    ms = jnp.mean(x * x, axis=-1, keepdims=True)
    return x * lax.rsqrt(ms + EPS) * g * (1.0 + sc) + sh


def _mod_kernel(c_ref, w_ref, b_ref, o_ref):
    c = c_ref[...]
    s = c * _sigmoid(c)
    o_ref[...] = jnp.dot(s, w_ref[...], preferred_element_type=F32,
                         precision=lax.Precision.HIGHEST) + b_ref[...]


def _modulation(cc, ada_w, ada_b):
    depth, d, n = ada_w.shape
    tn = _pick(n, (1536, 1024, 512))
    rows = cc.shape[0]
    return pl.pallas_call(
        _mod_kernel,
        grid=(depth, n // tn),
        in_specs=[
            pl.BlockSpec((rows, d), lambda l, j: (0, 0)),
            pl.BlockSpec((None, d, tn), lambda l, j: (l, 0, j)),
            pl.BlockSpec((None, 1, tn), lambda l, j: (l, 0, j)),
        ],
        out_specs=pl.BlockSpec((None, rows, tn), lambda l, j: (l, 0, j)),
        out_shape=jax.ShapeDtypeStruct((depth, rows, n), F32),
        compiler_params=_params("arbitrary", "arbitrary"),
        name="modulation",
    )(cc, ada_w, ada_b.reshape(depth, 1, n))


def _vec_spec(arr):
    d = arr.shape[-1]
    if arr.shape[0] == 1:
        return pl.BlockSpec((None, 1, d), lambda b, i, *_: (0, 0, 0))
    return pl.BlockSpec((None, 1, d), lambda b, i, *_: (b, 0, 0))


def _attn_in_kernel(x_ref, g_ref, sc_ref, sh_ref, w_ref, gains_ref, cos_ref, sin_ref, bd_ref,
                    qa_ref, ka_ref, va_ref, qb_ref, kb_ref, vb_ref, *, rope, tm):
    lane = lax.broadcasted_iota(jnp.int32, (CHUNK, LANES), 1)
    even = (lane & 1) == 0
    bd = bd_ref[...]
    ones_rows = jnp.where(lax.broadcasted_iota(jnp.int32, (V_ROWS - HEAD_DIM, CHUNK), 0) == 0,
                          1.0, 0.0).astype(BF16)
    groups = ((0, qa_ref, ka_ref, va_ref, 0), (Q_WIDTH + 2 * KV_WIDTH, qb_ref, kb_ref, vb_ref, 2))

    for j in range(tm // CHUNK):
        rows = slice(j * CHUNK, (j + 1) * CHUNK)
        h = _norm_mod(x_ref[rows, :], g_ref[...], sc_ref[...], sh_ref[...]).astype(BF16)
        y = jnp.dot(h, w_ref[...], preferred_element_type=F32)

        def head_norm(yc, gain, rows=rows):
            ss = jnp.dot((yc * yc).astype(BF16), bd, preferred_element_type=F32)
            t = yc * lax.rsqrt(ss * (1.0 / HEAD_DIM) + EPS) * gain
            if rope:
                partner = jnp.where(even, pltpu.roll(t, LANES - 1, 1), pltpu.roll(t, 1, 1))
                t = t * cos_ref[rows, :] + partner * sin_ref[rows, :]
            return t

        for col0, q_ref, k_ref, v_ref, grow in groups:
            gq = gains_ref[grow:grow + 1, :]
            gk = gains_ref[grow + 1:grow + 2, :]
            for c in range(Q_WIDTH // LANES):
                t = head_norm(y[:, col0 + c * LANES:col0 + (c + 1) * LANES], gq)
                q_ref[c * LANES:(c + 1) * LANES, rows] = t.T.astype(BF16)
            kcol = col0 + Q_WIDTH
            k_ref[rows, 0:KV_WIDTH] = head_norm(y[:, kcol:kcol + KV_WIDTH], gk).astype(BF16)
            k_ref[rows, KV_WIDTH:K_AUG] = jnp.where(lane == 0, 1.0, 0.0).astype(BF16)
            vt = y[:, kcol + KV_WIDTH:kcol + 2 * KV_WIDTH].T.astype(BF16)
            for kv in range(N_KV):
                r0 = kv * V_ROWS
                v_ref[j, r0:r0 + HEAD_DIM, :] = vt[kv * HEAD_DIM:(kv + 1) * HEAD_DIM, :]
                v_ref[j, r0 + HEAD_DIM:r0 + V_ROWS, :] = ones_rows


def _attn_in(x, g, sc, sh, w_bf16, gains, cos_t, sin_t, bd, *, rope):
    b, t, d = x.shape
    tm = _pick(t, (512, 256))
    nch = t // CHUNK
    kern = functools.partial(_attn_in_kernel, rope=rope, tm=tm)
    q_spec = pl.BlockSpec((None, Q_WIDTH, tm), lambda bb, i: (bb, 0, i))
    k_spec = pl.BlockSpec((None, tm, K_AUG), lambda bb, i: (bb, i, 0))
    v_spec = pl.BlockSpec((None, tm // CHUNK, N_KV * V_ROWS, CHUNK), lambda bb, i: (bb, i, 0, 0))
    q_shape = jax.ShapeDtypeStruct((b, Q_WIDTH, t), BF16)
    k_shape = jax.ShapeDtypeStruct((b, t, K_AUG), BF16)
    v_shape = jax.ShapeDtypeStruct((b, nch, N_KV * V_ROWS, CHUNK), BF16)
    return pl.pallas_call(
        kern,
        grid=(b, t // tm),
        in_specs=[
            pl.BlockSpec((None, tm, d), lambda bb, i: (bb, i, 0)),
            pl.BlockSpec((1, d), lambda bb, i: (0, 0)),
            _vec_spec(sc), _vec_spec(sh),
            pl.BlockSpec((d, IN_PROJ_WIDTH), lambda bb, i: (0, 0)),
            pl.BlockSpec((4, LANES), lambda bb, i: (0, 0)),
            pl.BlockSpec((tm, LANES), lambda bb, i: (i, 0)),
            pl.BlockSpec((tm, LANES), lambda bb, i: (i, 0)),
            pl.BlockSpec((LANES, LANES), lambda bb, i: (0, 0)),
        ],
        out_specs=[q_spec, k_spec, v_spec, q_spec, k_spec, v_spec],
        out_shape=[q_shape, k_shape, v_shape, q_shape, k_shape, v_shape],
        compiler_params=_params("parallel", "parallel"),
        name="attn_in",
    )(x, g, sc, sh, w_bf16, gains, cos_t, sin_t, bd)


def _attn_kernel(q_ref, k_ref, v_ref, sink_ref, o_ref, qpad_ref, s_ref, cmax_ref, m_ref, l_ref,
                 acc_ref, *, mode, use_sink, n_lat, tq):
    i = pl.program_id(1)
    w = GROUP * tq
    zeros = jnp.zeros((HEAD_DIM, w), BF16)
    for kv in range(N_KV):
        heads = range(kv * GROUP, (kv + 1) * GROUP)
        q4 = jnp.concatenate([q_ref[h * HEAD_DIM:(h + 1) * HEAD_DIM, :] for h in heads], axis=1)
        qpad_ref[kv] = jnp.concatenate([q4, zeros] if kv == 0 else [zeros, q4], axis=0)
        if use_sink:
            m_ref[kv] = jnp.concatenate(
                [jnp.full((1, tq), sink_ref[h] * LOG2E, F32) for h in heads], axis=1)
            l_ref[kv] = jnp.ones((1, w), F32)
        else:
            m_ref[kv] = jnp.full((1, w), NEG_BIG, F32)
            l_ref[kv] = jnp.zeros((1, w), F32)
        acc_ref[kv] = jnp.zeros((HEAD_DIM, w), F32)

    def scores(c, slot, mask):
        start = c * CHUNK if isinstance(c, int) else pl.multiple_of(c * CHUNK, CHUNK)
        kc = k_ref[pl.ds(start, CHUNK), 0:KV_WIDTH]
        for kv in range(N_KV):
            s = jnp.dot(kc, qpad_ref[kv], preferred_element_type=F32)
            if mask is not None:
                s = jnp.where(mask, s, -jnp.inf)
            s_ref[slot, kv] = s
            cmax_ref[slot, kv] = jnp.max(s, axis=0, keepdims=True)

    def absorb(c, slot):
        for kv in range(N_KV):
            m = m_ref[kv]
            m_new = jnp.maximum(m, cmax_ref[slot, kv])
            alpha = jnp.exp2(m - m_new)
            p = jnp.exp2(s_ref[slot, kv] - m_new)
            l_ref[kv] = alpha * l_ref[kv] + jnp.sum(p, axis=0, keepdims=True)
            m_ref[kv] = m_new
            vc = v_ref[c, kv * V_ROWS:kv * V_ROWS + HEAD_DIM, :]
            acc_ref[kv] = alpha * acc_ref[kv] + jnp.dot(vc, p.astype(BF16),
                                                        preferred_element_type=F32)

    scores(0, 0, None)
    if mode == "global":
        def body(j, carry):
            c = 2 * j
            scores(c + 1, 1, None)
            absorb(c, 0)
            scores(c + 2, 0, None)
            absorb(c + 1, 1)
            return carry
        lax.fori_loop(0, n_lat // 2, body, 0)
        c_end = 2 * (n_lat // 2)
        if n_lat % 2:
            scores(c_end + 1, 1, None)
            absorb(c_end, 0)
            absorb(c_end + 1, 1)
        else:
            absorb(c_end, 0)
    elif mode == "window":
        row = lax.broadcasted_iota(jnp.int32, (CHUNK, w), 0)
        col = lax.broadcasted_iota(jnp.int32, (CHUNK, w), 1) & (tq - 1)
        rel0 = col - row
        prev_c, prev_slot = 0, 0
        for d in (-1, 0, 1):
            cl = i + d
            ok = jnp.logical_and(cl >= 0, cl < n_lat)
            c = 1 + jnp.clip(cl, 0, n_lat - 1)
            mask = jnp.logical_and(jnp.abs(rel0 - d * CHUNK) <= WINDOW, ok)
            scores(c, 1 - prev_slot, mask)
            absorb(prev_c, prev_slot)
            prev_c, prev_slot = c, 1 - prev_slot
        absorb(prev_c, prev_slot)
    else:
        absorb(0, 0)

    for kv in range(N_KV):
        out = acc_ref[kv] * (1.0 / l_ref[kv])
        for g in range(GROUP):
            h = kv * GROUP + g
            o_ref[h * HEAD_DIM:(h + 1) * HEAD_DIM, :] = out[:, g * tq:(g + 1) * tq].astype(BF16)


def _attention(q_t, k_all, v_all, sink, bound, *, mode, use_sink):
    del bound
    b, _, tq_total = q_t.shape
    nk = k_all.shape[1]
    nc = v_all.shape[1]
    tq = CHUNK
    kern = functools.partial(_attn_kernel, mode=mode, use_sink=use_sink, n_lat=nc - 1, tq=tq)
    return pl.pallas_call(
        kern,
        grid=(b, tq_total // tq),
        in_specs=[
            pl.BlockSpec((None, Q_WIDTH, tq), lambda bb, i: (bb, 0, i)),
            pl.BlockSpec((None, nk, K_AUG), lambda bb, i: (bb, 0, 0)),
            pl.BlockSpec((None, nc, N_KV * V_ROWS, CHUNK), lambda bb, i: (bb, 0, 0, 0)),
            pl.BlockSpec(memory_space=pltpu.SMEM),
        ],
        out_specs=pl.BlockSpec((None, Q_WIDTH, tq), lambda bb, i: (bb, 0, i)),
        out_shape=jax.ShapeDtypeStruct((b, Q_WIDTH, tq_total), BF16),
        scratch_shapes=[pltpu.VMEM((N_KV, KV_WIDTH, GROUP * tq), BF16),
                        pltpu.VMEM((2, N_KV, CHUNK, GROUP * tq), F32),
                        pltpu.VMEM((2, N_KV, 1, GROUP * tq), F32),
                        pltpu.VMEM((N_KV, 1, GROUP * tq), F32),
                        pltpu.VMEM((N_KV, 1, GROUP * tq), F32),
                        pltpu.VMEM((N_KV, HEAD_DIM, GROUP * tq), F32)],
        compiler_params=_params("parallel", "parallel"),
        name="attn_" + mode,
    )(q_t, k_all, v_all, sink)


def _attn_fast_kernel(q_ref, k_ref, v_ref, sink_ref, bound_ref, o_ref, qa_ref, acc_ref, p_ref, *,
                      mode, use_sink, n_lat, tq):
    i = pl.program_id(1)
    w = GROUP * tq
    bound = bound_ref[0]
    zeros = jnp.zeros((HEAD_DIM, w), BF16)
    row = lax.broadcasted_iota(jnp.int32, (KV_WIDTH, w), 0)
    shift = jnp.where(row == 0, -bound, 0.0).astype(BF16)
    for kv in range(N_KV):
        heads = range(kv * GROUP, (kv + 1) * GROUP)
        q4 = jnp.concatenate([q_ref[h * HEAD_DIM:(h + 1) * HEAD_DIM, :] for h in heads], axis=1)
        qa_ref[kv] = jnp.concatenate(([q4, zeros] if kv == 0 else [zeros, q4]) + [shift], axis=0)
        acc_ref[kv] = jnp.zeros((V_ROWS, w), F32)

    def update(c, mask, n=1):
        start = c * CHUNK if isinstance(c, int) else pl.multiple_of(c * CHUNK, CHUNK)
        kc = k_ref[pl.ds(start, n * CHUNK), :]
        for kv in range(N_KV):
            s = jnp.dot(kc, qa_ref[kv], preferred_element_type=F32)
            if mask is not None:
                s = jnp.where(mask, s, -jnp.inf)
            p = jnp.exp2(s).astype(BF16)
            vc = jnp.concatenate([v_ref[c + j, kv * V_ROWS:(kv + 1) * V_ROWS, :] for j in range(n)],
                                 axis=1)
            acc_ref[kv] += jnp.dot(vc, p, preferred_element_type=F32)

    def probs(c, n, slot):
        start = c * CHUNK if isinstance(c, int) else pl.multiple_of(c * CHUNK, CHUNK)
        kc = k_ref[pl.ds(start, n * CHUNK), :]
        for kv in range(N_KV):
            s = jnp.dot(kc, qa_ref[kv], preferred_element_type=F32)
            p_ref[slot, kv, 0:n * CHUNK, :] = jnp.exp2(s).astype(BF16)

    def values(c, n, slot):
        for kv in range(N_KV):
            vc = jnp.concatenate([v_ref[c + j, kv * V_ROWS:(kv + 1) * V_ROWS, :] for j in range(n)],
                                 axis=1)
            acc_ref[kv] += jnp.dot(vc, p_ref[slot, kv, 0:n * CHUNK, :], preferred_element_type=F32)

    if mode == "global":
        n_full = (1 + n_lat) // GLOBAL_STEP
        steps = [(GLOBAL_STEP * s, GLOBAL_STEP) for s in range(n_full)]
        steps += [(c, 1) for c in range(GLOBAL_STEP * n_full, 1 + n_lat)]
        n_pairs = max((n_full - 1) // 2, 0)

        probs(*steps[0], 0)

        def body(t, carry):
            c = 2 * GLOBAL_STEP * t
            probs(c + GLOBAL_STEP, GLOBAL_STEP, 1)
            values(c, GLOBAL_STEP, 0)
            probs(c + 2 * GLOBAL_STEP, GLOBAL_STEP, 0)
            values(c + GLOBAL_STEP, GLOBAL_STEP, 1)
            return carry
        lax.fori_loop(0, n_pairs, body, 0)
        pending, slot = steps[2 * n_pairs], 0
        for nxt in steps[2 * n_pairs + 1:]:
            probs(*nxt, 1 - slot)
            values(*pending, slot)
            pending, slot = nxt, 1 - slot
        values(*pending, slot)
    else:
        update(0, None)
    if mode == "window":
        rowk =lax.broadcasted_iota(jnp.int32, (CHUNK, w), 0)
        col = lax.broadcasted_iota(jnp.int32, (CHUNK, w), 1) & (tq - 1)
        rel0 = col - rowk
        for d in (-1, 0, 1):
            cl = i + d
            ok = jnp.logical_and(cl >= 0, cl < n_lat)
            c = 1 + jnp.clip(cl, 0, n_lat - 1)
            update(c, jnp.logical_and(jnp.abs(rel0 - d * CHUNK) <= WINDOW, ok))

    for kv in range(N_KV):
        acc = acc_ref[kv]
        l = acc[HEAD_DIM:HEAD_DIM + 1, :]
        if use_sink:
            l = l + jnp.concatenate(
                [jnp.full((1, tq), jnp.exp2(sink_ref[kv * GROUP + g] * LOG2E - bound), F32)
                 for g in range(GROUP)], axis=1)
        out = acc[0:HEAD_DIM, :] * (1.0 / l)
        for g in range(GROUP):
            h = kv * GROUP + g
            o_ref[h * HEAD_DIM:(h + 1) * HEAD_DIM, :] = out[:, g * tq:(g + 1) * tq].astype(BF16)


def _attention_fast(q_t, k_all, v_all, sink, bound, *, mode, use_sink):
    b, _, tq_total = q_t.shape
    nk = k_all.shape[1]
    nc = v_all.shape[1]
    tq = CHUNK
    kern = functools.partial(_attn_fast_kernel, mode=mode, use_sink=use_sink, n_lat=nc - 1, tq=tq)
    return pl.pallas_call(
        kern,
        grid=(b, tq_total // tq),
        in_specs=[
            pl.BlockSpec((None, Q_WIDTH, tq), lambda bb, i: (bb, 0, i)),
            pl.BlockSpec((None, nk, K_AUG), lambda bb, i: (bb, 0, 0)),
            pl.BlockSpec((None, nc, N_KV * V_ROWS, CHUNK), lambda bb, i: (bb, 0, 0, 0)),
            pl.BlockSpec(memory_space=pltpu.SMEM),
            pl.BlockSpec(memory_space=pltpu.SMEM),
        ],
        out_specs=pl.BlockSpec((None, Q_WIDTH, tq), lambda bb, i: (bb, 0, i)),
        out_shape=jax.ShapeDtypeStruct((b, Q_WIDTH, tq_total), BF16),
        scratch_shapes=[pltpu.VMEM((N_KV, K_AUG, GROUP * tq), BF16),
                        pltpu.VMEM((N_KV, V_ROWS, GROUP * tq), F32),
                        pltpu.VMEM((2, N_KV, (GLOBAL_STEP if mode == "global" else 1) * CHUNK,
                                    GROUP * tq), BF16)],
        compiler_params=_params("parallel", "parallel"),
        name="attn_fast_" + mode,
    )(q_t, k_all, v_all, sink, bound)


def _attend(fast_ok, *args, **kw):
    return lax.cond(fast_ok, functools.partial(_attention_fast, **kw),
                    functools.partial(_attention, **kw), *args)


_TN_DIMS = (((0,), (0,)), ((), ()))


def _attn_out_kernel(oa_ref, ob_ref, w_ref, x_ref, g1_ref, n2_ref, sc2_ref, sh2_ref, x1_ref, h2_ref):
    mix = lax.dot_general(oa_ref[...], w_ref[0:Q_WIDTH, :], _TN_DIMS, preferred_element_type=F32)
    mix = mix + lax.dot_general(ob_ref[...], w_ref[Q_WIDTH:2 * Q_WIDTH, :], _TN_DIMS,
                                preferred_element_type=F32)
    x1 = x_ref[...] + g1_ref[...] * mix
    x1_ref[...] = x1
    h2_ref[...] = _norm_mod(x1, n2_ref[...], sc2_ref[...], sh2_ref[...]).astype(BF16)


def _attn_out(oa, ob, w_bf16, x, g1, n2, sc2, sh2):
    b, t, d = x.shape
    tm = _pick(t, (512, 256))
    o_spec = pl.BlockSpec((None, Q_WIDTH, tm), lambda bb, i: (bb, 0, i))
    x_spec = pl.BlockSpec((None, tm, d), lambda bb, i: (bb, i, 0))
    return pl.pallas_call(
        _attn_out_kernel,
        grid=(b, t // tm),
        in_specs=[o_spec, o_spec,
                  pl.BlockSpec((2 * Q_WIDTH, d), lambda bb, i: (0, 0)),
                  x_spec, _vec_spec(g1),
                  pl.BlockSpec((1, d), lambda bb, i: (0, 0)),
                  _vec_spec(sc2), _vec_spec(sh2)],
        out_specs=[x_spec, x_spec],
        out_shape=[jax.ShapeDtypeStruct((b, t, d), F32), jax.ShapeDtypeStruct((b, t, d), BF16)],
        compiler_params=_params("parallel", "parallel"),
        name="attn_out",
    )(oa, ob, w_bf16, x, g1, n2, sc2, sh2)


def _ffn_kernel(*refs, gated, tm):
    if gated:
        h_ref, wg_ref, wu_ref, wd_ref, gate_ref, x_ref, g2_ref, o_ref, acc_ref = refs
    else:
        h_ref, wg_ref, wu_ref, wd_ref, x_ref, g2_ref, o_ref, acc_ref = refs
    e = pl.program_id(2)
    f = pl.program_id(3)
    first = jnp.logical_and(e == 0, f == 0)
    last = jnp.logical_and(e == pl.num_programs(2) - 1, f == pl.num_programs(3) - 1)

    @pl.when(first)
    def _():
        acc_ref[...] = jnp.zeros_like(acc_ref)

    h = h_ref[...]
    a = jnp.dot(h, wg_ref[...].astype(BF16), preferred_element_type=F32)
    u = jnp.dot(h, wu_ref[...].astype(BF16), preferred_element_type=F32)
    act = (a * _sigmoid(a) * u).astype(BF16)
    y = jnp.dot(act, wd_ref[...].astype(BF16), preferred_element_type=F32)
    if gated:
        lane = lax.broadcasted_iota(jnp.int32, (tm, LANES), 1)
        ge = jnp.sum(jnp.where(lane == e, gate_ref[...], 0.0), axis=-1, keepdims=True)
        y = y * ge
    acc_ref[...] += y

    @pl.when(last)
    def _():
        o_ref[...] = x_ref[...] + g2_ref[...] * acc_ref[...]


def _ffn(h2, wg, wu, wd, gates, x1, g2, e0=0, n_e=1):
    b, t, d = x1.shape
    f = wg.shape[-1]
    gated = gates is not None
    tm = _pick(t, (1024, 512, 256) if gated else (512, 256))
    resident = n_e == 1 and 3 * d * f * 2 <= VMEM_LIMIT_BYTES // 3
    tf = f if resident else _pick(f, (512, 256))
    mode = dict(pipeline_mode=pl.Buffered(1)) if resident else {}
    x_spec = pl.BlockSpec((None, tm, d), lambda bb, i, e, j: (bb, i, 0))
    in_specs = [x_spec,
                pl.BlockSpec((None, d, tf), lambda bb, i, e, j: (e0 + e, 0, j), **mode),
                pl.BlockSpec((None, d, tf), lambda bb, i, e, j: (e0 + e, 0, j), **mode),
                pl.BlockSpec((None, tf, d), lambda bb, i, e, j: (e0 + e, j, 0), **mode)]
    args = [h2, wg, wu, wd]
    if gated:
        in_specs.append(pl.BlockSpec((None, tm, LANES), lambda bb, i, e, j: (bb, i, 0)))
        args.append(gates)
    in_specs += [x_spec, _vec_spec(g2)]
    args += [x1, g2]
    return pl.pallas_call(
        functools.partial(_ffn_kernel, gated=gated, tm=tm),
        grid=(b, t // tm, n_e, f // tf),
        in_specs=in_specs,
        out_specs=x_spec,
        out_shape=jax.ShapeDtypeStruct((b, t, d), F32),
        scratch_shapes=[pltpu.VMEM((tm, d), F32)],
        compiler_params=_params("parallel", "parallel", "arbitrary", "arbitrary"),
        name="ffn_gated" if gated else "ffn_dense",
    )(*args)


def _conv_in_kernel(x_ref, g_ref, sc_ref, sh_ref, w_ref, bg_ref, z_ref, *, d):
    h = _norm_mod(x_ref[...], g_ref[...], sc_ref[...], sh_ref[...]).astype(BF16)
    y = jnp.dot(h, w_ref[...], preferred_element_type=F32)
    bg_ref[...] = y[:, 0:d].astype(BF16)
    z_ref[...] = (y[:, d:2 * d] * y[:, 2 * d:3 * d]).astype(BF16)


def _conv_in(x, g, sc, sh, w_bf16):
    b, t, d = x.shape
    tm = _pick(t, (512, 256))
    x_spec = pl.BlockSpec((None, tm, d), lambda bb, i: (bb, i, 0))
    return pl.pallas_call(
        functools.partial(_conv_in_kernel, d=d),
        grid=(b, t // tm),
        in_specs=[x_spec, pl.BlockSpec((1, d), lambda bb, i: (0, 0)), _vec_spec(sc), _vec_spec(sh),
                  pl.BlockSpec((d, 3 * d), lambda bb, i: (0, 0))],
        out_specs=[x_spec, x_spec],
        out_shape=[jax.ShapeDtypeStruct((b, t, d), BF16), jax.ShapeDtypeStruct((b, t, d), BF16)],
        compiler_params=_params("parallel", "parallel"),
        name="conv_in",
    )(x, g, sc, sh, w_bf16)


HALO = 16


def _conv_out_kernel(z_ref, zp_ref, zn_ref, bg_ref, cw_ref, w_ref, x_ref, g1_ref, n2_ref, sc2_ref,
                     sh2_ref, rw_ref, x1_ref, h2_ref, gate_ref, *, tm, n_experts):
    i = pl.program_id(1)
    z = z_ref[...].astype(F32)
    prev = jnp.where(i > 0, zp_ref[HALO - 1:HALO, :].astype(F32), 0.0)
    nxt = jnp.where(i < pl.num_programs(1) - 1, zn_ref[0:1, :].astype(F32), 0.0)
    row = lax.broadcasted_iota(jnp.int32, z.shape, 0)
    z_dn = jnp.where(row == 0, prev, pltpu.roll(z, 1, 0))
    z_up = jnp.where(row == tm - 1, nxt, pltpu.roll(z, tm - 1, 0))
    conv = z_dn * cw_ref[0:1, :] + z * cw_ref[1:2, :] + z_up * cw_ref[2:3, :]
    v = (bg_ref[...].astype(F32) * conv).astype(BF16)
    mix = jnp.dot(v, w_ref[...], preferred_element_type=F32)
    x1 = x_ref[...] + g1_ref[...] * mix
    x1_ref[...] = x1
    h2 = _norm_mod(x1, n2_ref[...], sc2_ref[...], sh2_ref[...])
    h2_ref[...] = h2.astype(h2_ref.dtype)

    h_hi = h2.astype(BF16)
    h_lo = (h2 - h_hi.astype(F32)).astype(BF16)
    logits = (jnp.dot(h_hi, rw_ref[0], preferred_element_type=F32)
              + jnp.dot(h_lo, rw_ref[0], preferred_element_type=F32)
              + jnp.dot(h_hi, rw_ref[1], preferred_element_type=F32))
    lane = lax.broadcasted_iota(jnp.int32, logits.shape, 1)
    lanef = lane.astype(F32)
    logits = jnp.where(lane < n_experts, logits, -jnp.inf)
    m1 = jnp.max(logits, axis=-1, keepdims=True)
    i1 = jnp.min(jnp.where(logits == m1, lanef, float(LANES)), axis=-1, keepdims=True)
    rest = jnp.where(lanef == i1, -jnp.inf, logits)
    m2 = jnp.max(rest, axis=-1, keepdims=True)
    i2 = jnp.min(jnp.where(rest == m2, lanef, float(LANES)), axis=-1, keepdims=True)
    e2 = jnp.exp(m2 - m1)
    w1 = 1.0 / (1.0 + e2)
    w2 = e2 * w1
    rec = jnp.where(lanef == i1, w1, 0.0) + jnp.where(lanef == i2, w2, 0.0)
    for k, val in enumerate((i1, i2, w1, w2)):
        rec = jnp.where(lane == n_experts + k, val, rec)
    gate_ref[...] = rec


def _conv_out(z, bg, conv_w, w_bf16, x, g1, n2, sc2, sh2, router_pad, n_experts, h2_dtype):
    b, t, d = x.shape
    tm = _pick(t, (512, 256))
    hb = tm // HALO
    n_halo = t // HALO
    x_spec = pl.BlockSpec((None, tm, d), lambda bb, i: (bb, i, 0))
    prev_spec = pl.BlockSpec((None, HALO, d), lambda bb, i: (bb, jnp.maximum(i * hb - 1, 0), 0))
    next_spec = pl.BlockSpec((None, HALO, d), lambda bb, i: (bb, jnp.minimum((i + 1) * hb, n_halo - 1), 0))
    full = lambda r, c: pl.BlockSpec((r, c), lambda bb, i: (0, 0))
    return pl.pallas_call(
        functools.partial(_conv_out_kernel, tm=tm, n_experts=n_experts),
        grid=(b, t // tm),
        in_specs=[x_spec, prev_spec, next_spec, x_spec, full(3, d), full(d, d), x_spec, _vec_spec(g1),
                  full(1, d), _vec_spec(sc2), _vec_spec(sh2),
                  pl.BlockSpec((2, d, LANES), lambda bb, i: (0, 0, 0))],
        out_specs=[x_spec, x_spec, pl.BlockSpec((None, tm, LANES), lambda bb, i: (bb, i, 0))],
        out_shape=[jax.ShapeDtypeStruct((b, t, d), F32), jax.ShapeDtypeStruct((b, t, d), h2_dtype),
                   jax.ShapeDtypeStruct((b, t, LANES), F32)],
        compiler_params=_params("parallel", "parallel"),
        name="conv_out",
    )(z, z, z, bg, conv_w, w_bf16, x, g1, n2, sc2, sh2, router_pad)


EXPERT_TILE = 1024
TOP_K = 2


def _rank_kernel(route_ref, rank_ref, count_ref, base_ref, *, tm, n_experts):
    @pl.when(pl.program_id(0) == 0)
    def _():
        base_ref[...] = jnp.zeros_like(base_ref)

    route = route_ref[...]
    lane = lax.broadcasted_iota(jnp.int32, (tm, LANES), 1)
    lanef = lane.astype(F32)
    i1 = jnp.sum(jnp.where(lane == n_experts, route, 0.0), axis=-1, keepdims=True)
    i2 = jnp.sum(jnp.where(lane == n_experts + 1, route, 0.0), axis=-1, keepdims=True)
    onehot = jnp.where(lanef == i1, 1.0, 0.0) + jnp.where(lanef == i2, 1.0, 0.0)
    r = lax.broadcasted_iota(jnp.int32, (tm, tm), 0)
    c = lax.broadcasted_iota(jnp.int32, (tm, tm), 1)
    lower = jnp.where(c < r, 1.0, 0.0).astype(BF16)
    before = jnp.dot(lower, onehot.astype(BF16), preferred_element_type=F32) + base_ref[...]
    r1 = jnp.sum(jnp.where(lanef == i1, before, 0.0), axis=-1, keepdims=True)
    r2 = jnp.sum(jnp.where(lanef == i2, before, 0.0), axis=-1, keepdims=True)
    packed = jnp.where(lane == 0, r1, jnp.where(lane == 1, r2, jnp.where(lane == 2, i1,
                       jnp.where(lane == 3, i2, 0.0))))
    rank_ref[...] = packed.T[0:8, :]
    base_ref[...] += jnp.sum(onehot, axis=0, keepdims=True)
    count_ref[...] = base_ref[...]


def _rank(route, n_experts):
    n = route.shape[0]
    tm = _pick(n, (512, 256))
    return pl.pallas_call(
        functools.partial(_rank_kernel, tm=tm, n_experts=n_experts),
        grid=(n // tm,),
        in_specs=[pl.BlockSpec((tm, LANES), lambda i: (i, 0))],
        out_specs=[pl.BlockSpec((8, tm), lambda i: (0, i)), pl.BlockSpec((1, LANES), lambda i: (0, 0))],
        out_shape=[jax.ShapeDtypeStruct((8, n), F32), jax.ShapeDtypeStruct((1, LANES), F32)],
        scratch_shapes=[pltpu.VMEM((1, LANES), F32)],
        compiler_params=_params("arbitrary"),
        name="moe_rank",
    )(route)


def _dispatch_kernel(dest_ref, h_ref, xs_in_ref, xs_ref, sem, *, tm):
    del xs_in_ref
    copies = []
    for r in range(tm):
        for k in range(TOP_K):
            cp = pltpu.make_async_copy(h_ref.at[pl.ds(r, 1), :],
                                       xs_ref.at[pl.ds(dest_ref[0, k * tm + r], 1), :], sem)
            cp.start(priority=k)
            copies.append(cp)
    for cp in copies:
        cp.wait()


def _dispatch(dest, h2, n_rows):
    n, d = h2.shape
    tm = dest.shape[-1] // TOP_K
    zeros = jnp.zeros((n_rows, d), h2.dtype)
    return pl.pallas_call(
        functools.partial(_dispatch_kernel, tm=tm),
        grid=(n // tm,),
        in_specs=[pl.BlockSpec((None, 1, TOP_K * tm), lambda i: (i, 0, 0), memory_space=pltpu.SMEM),
                  pl.BlockSpec((tm, d), lambda i: (i, 0)),
                  pl.BlockSpec(memory_space=pl.ANY)],
        out_specs=pl.BlockSpec(memory_space=pl.ANY),
        out_shape=jax.ShapeDtypeStruct((n_rows, d), h2.dtype),
        scratch_shapes=[pltpu.SemaphoreType.DMA(())],
        input_output_aliases={2: 0},
        compiler_params=_params("arbitrary"),
        name="moe_dispatch",
    )(dest, h2, zeros)


def _expert_kernel(te_ref, nu_ref, xs_ref, wg_ref, wu_ref, wd_ref, o_ref, hb_ref, acc_ref):
    del te_ref
    j = pl.program_id(0)
    f = pl.program_id(1)

    @pl.when(j < nu_ref[0])
    def _():
        @pl.when(f == 0)
        def _():
            hb_ref[...] = xs_ref[...].astype(BF16)
            acc_ref[...] = jnp.zeros_like(acc_ref)

        h = hb_ref[...]
        a = jnp.dot(h, wg_ref[...].astype(BF16), preferred_element_type=F32)
        u = jnp.dot(h, wu_ref[...].astype(BF16), preferred_element_type=F32)
        act = (a * _sigmoid(a) * u).astype(BF16)
        acc_ref[...] += jnp.dot(act, wd_ref[...].astype(BF16), preferred_element_type=F32)

        @pl.when(f == pl.num_programs(1) - 1)
        def _():
            o_ref[...] = acc_ref[...]

    @pl.when(jnp.logical_and(j >= nu_ref[0], f == pl.num_programs(1) - 1))
    def _():
        o_ref[...] = jnp.zeros_like(o_ref)


def _experts(tile_expert, n_used, xs, wg, wu, wd):
    n_rows, d = xs.shape
    f = wg.shape[-1]
    tf = _pick(f, (512, 256))
    nf = f // tf
    n_tiles = n_rows // EXPERT_TILE

    def row_map(j, ff, te, nu):
        return (jnp.minimum(j, nu[0] - 1), 0)

    def w_up_map(j, ff, te, nu):
        live = j < nu[0]
        return (te[jnp.minimum(j, nu[0] - 1)], 0, jnp.where(live, ff, nf - 1))

    def w_down_map(j, ff, te, nu):
        live = j < nu[0]
        return (te[jnp.minimum(j, nu[0] - 1)], jnp.where(live, ff, nf - 1), 0)

    grid_spec = pltpu.PrefetchScalarGridSpec(
        num_scalar_prefetch=2,
        grid=(n_tiles, nf),
        in_specs=[pl.BlockSpec((EXPERT_TILE, d), row_map),
                  pl.BlockSpec((None, d, tf), w_up_map),
                  pl.BlockSpec((None, d, tf), w_up_map),
                  pl.BlockSpec((None, tf, d), w_down_map)],
        out_specs=pl.BlockSpec((EXPERT_TILE, d), lambda j, ff, te, nu: (j, 0)),
        scratch_shapes=[pltpu.VMEM((EXPERT_TILE, d), BF16), pltpu.VMEM((EXPERT_TILE, d), F32)],
    )
    return pl.pallas_call(
        _expert_kernel,
        grid_spec=grid_spec,
        out_shape=jax.ShapeDtypeStruct((n_rows, d), F32),
        compiler_params=_params("arbitrary", "arbitrary"),
        name="moe_experts",
    )(tile_expert, n_used, xs, wg, wu, wd)


def _combine_kernel(dest_ref, next_ref, os_ref, route_ref, x_ref, g2_ref, o_ref, buf_ref, sem, *,
                    tm, n_experts):
    i = pl.program_id(0)
    slot = i % 2

    def gather(idx_ref, s, start):
        for r in range(tm):
            for k in range(TOP_K):
                cp = pltpu.make_async_copy(os_ref.at[pl.ds(idx_ref[0, k * tm + r], 1), :],
                                           buf_ref.at[s, k, pl.ds(r, 1), :], sem.at[s])
                if start:
                    cp.start(priority=k)
                else:
                    cp.wait()

    @pl.when(i == 0)
    def _():
        gather(dest_ref, slot, True)

    @pl.when(i + 1 < pl.num_programs(0))
    def _():
        gather(next_ref, 1 - slot, True)

    gather(dest_ref, slot, False)

    route = route_ref[...]
    lane = lax.broadcasted_iota(jnp.int32, (tm, LANES), 1)
    w1 = jnp.sum(jnp.where(lane == n_experts + 2, route, 0.0), axis=-1, keepdims=True)
    w2 = jnp.sum(jnp.where(lane == n_experts + 3, route, 0.0), axis=-1, keepdims=True)
    o_ref[...] = x_ref[...] + g2_ref[...] * (w1 * buf_ref[slot, 0] + w2 * buf_ref[slot, 1])


def _combine(dest, os, route, x1, g2, n_experts):
    b, t, d = x1.shape
    tm = dest.shape[-1] // TOP_K
    per_seq = t // tm
    n_steps = b * per_seq
    x_spec = pl.BlockSpec((None, tm, d), lambda i: (i // per_seq, i % per_seq, 0))
    if g2.shape[0] == 1:
        g_spec = pl.BlockSpec((None, 1, d), lambda i: (0, 0, 0))
    else:
        g_spec = pl.BlockSpec((None, 1, d), lambda i: (i // per_seq, 0, 0))
    return pl.pallas_call(
        functools.partial(_combine_kernel, tm=tm, n_experts=n_experts),
        grid=(n_steps,),
        in_specs=[pl.BlockSpec((None, 1, TOP_K * tm), lambda i: (i, 0, 0), memory_space=pltpu.SMEM),
                  pl.BlockSpec((None, 1, TOP_K * tm), lambda i: (jnp.minimum(i + 1, n_steps - 1), 0, 0),
                               memory_space=pltpu.SMEM),
                  pl.BlockSpec(memory_space=pl.ANY),
                  pl.BlockSpec((None, tm, LANES), lambda i: (i // per_seq, i % per_seq, 0)),
                  x_spec, g_spec],
        out_specs=x_spec,
        out_shape=jax.ShapeDtypeStruct((b, t, d), F32),
        scratch_shapes=[pltpu.VMEM((2, TOP_K, tm, d), F32), pltpu.SemaphoreType.DMA((2,))],
        compiler_params=_params("arbitrary"),
        name="moe_combine",
    )(dest, dest, os, route, x1, g2)


def _moe_routed(h2, route, x1, g2, wg, wu, wd, e0, n_experts):
    b, t, d = x1.shape
    n = b * t
    tm = _pick(t, (512, 256))
    rec = route.reshape(n, LANES)
    rank, counts = _rank(rec, n_experts)

    cnt = counts[0, :n_experts].astype(jnp.int32)
    padded = ((cnt + EXPERT_TILE - 1) // EXPERT_TILE) * EXPERT_TILE
    ends = jnp.cumsum(padded)
    offs = ends - padded
    n_tiles = (2 * n) // EXPERT_TILE + n_experts
    n_used = (ends[-1] // EXPERT_TILE).astype(jnp.int32).reshape(1)
    tile_start = jnp.arange(n_tiles, dtype=jnp.int32) * EXPERT_TILE
    tile_expert = jnp.minimum(jnp.sum(tile_start[:, None] >= ends[None, :], axis=1), n_experts - 1)
    slot = rank.astype(jnp.int32)
    d1 = jnp.take(offs, slot[2]) + slot[0]
    d2 = jnp.take(offs, slot[3]) + slot[1]
    dest = jnp.concatenate([d1.reshape(n // tm, tm), d2.reshape(n // tm, tm)], axis=1)[:, None, :]

    xs = _dispatch(dest, h2.reshape(n, d), n_tiles * EXPERT_TILE)
    os = _experts((e0 + tile_expert).astype(jnp.int32), n_used, xs, wg, wu, wd)
    return _combine(dest, os, route, x1, g2, n_experts)


def _rope_tables(n_tokens):
    rows = n_tokens // GRID_W
    row, col = jnp.meshgrid(jnp.arange(rows, dtype=F32), jnp.arange(GRID_W, dtype=F32), indexing="ij")
    half = HEAD_DIM // 2
    inv_freq = ROPE_THETA ** (-jnp.arange(0, half, 2, dtype=F32) / half)
    ang = jnp.concatenate([row.reshape(-1, 1) * inv_freq, col.reshape(-1, 1) * inv_freq], axis=-1)
    cos = jnp.repeat(jnp.cos(ang), 2, axis=-1)
    sign = jnp.tile(jnp.array([-1.0, 1.0], F32), HEAD_DIM // 2)
    sin = jnp.repeat(jnp.sin(ang), 2, axis=-1) * sign
    reps = LANES // HEAD_DIM
    return jnp.tile(cos, (1, reps)), jnp.tile(sin, (1, reps))


def kernel(x, c, ctx, c_ctx, ada_w, ada_b, norm1_g, norm2_g, attn_w_in, attn_w_out, qnorm_a, knorm_a, qnorm_b, knorm_b, sink_b, ffn_w_gate, ffn_w_up, ffn_w_down, conv_w_in, conv_w, conv_w_out, router_w, moe_w_gate, moe_w_up, moe_w_down):
    bsz, n_tok, d = x.shape
    n_ctx = ctx.shape[1]
    depth = ada_w.shape[0]
    n_experts = router_w.shape[-1]
    assert n_ctx == CHUNK and n_tok % CHUNK == 0 and bsz + 1 <= 8
    assert attn_w_in.shape[-1] == IN_PROJ_WIDTH and d % LANES == 0 and n_experts + 4 <= LANES

    cc = jnp.zeros((8, d), F32).at[:bsz].set(c).at[bsz].set(c_ctx)
    mod = _modulation(cc, ada_w, ada_b)

    def mod_vec(layer, j, is_ctx):
        m = mod[layer, :, j * d:(j + 1) * d]
        return m[bsz:bsz + 1].reshape(1, 1, d) if is_ctx else m[:bsz].reshape(bsz, 1, d)

    cos_l, sin_l = _rope_tables(n_tok)
    cos_c = jnp.ones((n_ctx, LANES), F32)
    sin_c = jnp.zeros((n_ctx, LANES), F32)
    head_id = jnp.arange(LANES) // HEAD_DIM
    bd = (head_id[:, None] == head_id[None, :]).astype(BF16)
    no_sink = jnp.zeros((N_HEADS,), F32)
    tile2 = lambda v: jnp.tile(v, LANES // HEAD_DIM)

    xc = ctx
    for layer in range(depth):
        i = layer // 2
        ctx_needed = any(j % 2 == 0 for j in range(layer + 1, depth))
        n1 = norm1_g[layer].reshape(1, d)
        n2 = norm2_g[layer].reshape(1, d)
        mv = lambda j, is_ctx: mod_vec(layer, j, is_ctx)

        if layer % 2 == 0:
            w_in = attn_w_in[i].astype(BF16)
            w_out = attn_w_out[i].astype(BF16)
            qscale = ATTN_SCALE * LOG2E
            gains = jnp.stack([tile2(qnorm_a[i]) * qscale, tile2(knorm_a[i]),
                               tile2(qnorm_b[i]) * qscale, tile2(knorm_b[i])]).astype(F32)
            sink = sink_b[i].astype(F32)
            def logit_bound(gq, gk):
                raw = HEAD_DIM * jnp.max(jnp.abs(gq)) * jnp.max(jnp.abs(gk)) * qscale * 1.02
                return raw.astype(BF16).astype(F32)
            bound_a = logit_bound(qnorm_a[i], knorm_a[i])
            bound_b = jnp.maximum(logit_bound(qnorm_b[i], knorm_b[i]),
                                  (jnp.max(sink) * LOG2E * 1.02).astype(BF16).astype(F32))
            fast_a = bound_a <= FAST_MAX_LOGIT
            fast_b = jnp.logical_and(bound_b <= FAST_MAX_LOGIT,
                                     jnp.max(jnp.abs(sink)) * LOG2E <= FAST_MAX_LOGIT)
            bound_a = bound_a.reshape(1)
            bound_b = bound_b.reshape(1)
            qa_l, ka_l, va_l, qb_l, kb_l, vb_l = _attn_in(
                x, n1, mv(1, False), mv(0, False), w_in, gains, cos_l, sin_l, bd, rope=True)
            qa_c, ka_c, va_c, qb_c, kb_c, vb_c = _attn_in(
                xc, n1, mv(1, True), mv(0, True), w_in, gains, cos_c, sin_c, bd, rope=False)
            ka_all = jnp.concatenate([ka_c, ka_l], axis=1)
            va_all = jnp.concatenate([va_c, va_l], axis=1)
            kb_all = jnp.concatenate([kb_c, kb_l], axis=1)
            vb_all = jnp.concatenate([vb_c, vb_l], axis=1)
            oa = _attend(fast_a, qa_l, ka_all, va_all, no_sink, bound_a, mode="global", use_sink=False)
            ob = _attend(fast_b, qb_l, kb_all, vb_all, sink, bound_b, mode="window", use_sink=True)
            x, h2 = _attn_out(oa, ob, w_out, x, mv(2, False), n2, mv(4, False), mv(3, False))
            if ctx_needed:
                oa_c = _attend(fast_a, qa_c, ka_c, va_c, no_sink, bound_a, mode="ctx", use_sink=False)
                ob_c = _attend(fast_b, qb_c, kb_c, vb_c, sink, bound_b, mode="ctx", use_sink=True)
                xc, h2c = _attn_out(oa_c, ob_c, w_out, xc, mv(2, True), n2, mv(4, True), mv(3, True))
            wg = ffn_w_gate.astype(BF16)
            wu = ffn_w_up.astype(BF16)
            wd = ffn_w_down.astype(BF16)
            x = _ffn(h2, wg, wu, wd, None, x, mv(5, False), e0=i)
            if ctx_needed:
                xc = _ffn(h2c, wg, wu, wd, None, xc, mv(5, True), e0=i)
        else:
            w_in = conv_w_in[i].astype(BF16)
            w_out = conv_w_out[i].astype(BF16)
            cw = conv_w[i].astype(F32)
            router_f32 = jnp.zeros((d, LANES), F32).at[:, :n_experts].set(router_w[i])
            router_hi = router_f32.astype(BF16)
            router_pad = jnp.stack([router_hi, (router_f32 - router_hi.astype(F32)).astype(BF16)])
            wg = moe_w_gate.reshape((-1,) + moe_w_gate.shape[2:])
            wu = moe_w_up.reshape((-1,) + moe_w_up.shape[2:])
            wd = moe_w_down.reshape((-1,) + moe_w_down.shape[2:])
            e0 = i * n_experts
            bg, z = _conv_in(x, n1, mv(1, False), mv(0, False), w_in)
            x, h2, route = _conv_out(z, bg, cw, w_out, x, mv(2, False), n2, mv(4, False), mv(3, False),
                                     router_pad, n_experts, F32)
            x = _moe_routed(h2, route, x, mv(5, False), wg, wu, wd, e0, n_experts)
            if ctx_needed:
                bg, z = _conv_in(xc, n1, mv(1, True), mv(0, True), w_in)
                xc, h2c, gates_c = _conv_out(z, bg, cw, w_out, xc, mv(2, True), n2, mv(4, True),
                                             mv(3, True), router_pad, n_experts, BF16)
                flat = lambda a: a.reshape((1, bsz * n_ctx) + a.shape[2:])
                xc = _ffn(flat(h2c), wg, wu, wd, flat(gates_c), flat(xc), mv(5, True),
                          e0=e0, n_e=n_experts).reshape(bsz, n_ctx, d)
    return x
```

```python
import functools

import jax
import jax.numpy as jnp
from jax import lax
from jax.experimental import pallas as pl
from jax.experimental.pallas import tpu as pltpu

F32 = jnp.float32
BF16 = jnp.bfloat16

HEAD_DIM = 64
N_KV = 2
GROUP = 4
N_HEADS = N_KV * GROUP
Q_WIDTH = N_HEADS * HEAD_DIM
KV_WIDTH = N_KV * HEAD_DIM
IN_PROJ_WIDTH = 2 * (Q_WIDTH + 2 * KV_WIDTH)
GRID_W = 64
WINDOW = 128
ROPE_THETA = 10000.0
ATTN_SCALE = HEAD_DIM ** -0.5
EPS = 1e-6
N_ADA = 6
LOG2E = 1.4426950408889634
LANES = 128
CHUNK = 256
K_AUG = 2 * KV_WIDTH
V_ROWS = HEAD_DIM + 16
GLOBAL_STEP = 8
FAST_MAX_LOGIT = 40.0
NEG_BIG = -1e30
VMEM_LIMIT_BYTES = 56 * 1024 * 1024


def _params(*sem):
    return pltpu.CompilerParams(dimension_semantics=sem, vmem_limit_bytes=VMEM_LIMIT_BYTES)


def _pick(n, candidates):
    for t in candidates:
        if n % t == 0:
            return t
    return n


def _sigmoid(a):
    return 1.0 / (1.0 + jnp.exp(-a))


def _norm_mod(x, g, sc, sh):
    ms = jnp.mean(x * x, axis=-1, keepdims=True)
    return x * lax.rsqrt(ms + EPS) * g * (1.0 + sc) + sh


def _mod_kernel(c_ref, w_ref, b_ref, o_ref):
    c = c_ref[...]
    s = c * _sigmoid(c)
    o_ref[...] = jnp.dot(s, w_ref[...], preferred_element_type=F32,
                         precision=lax.Precision.HIGHEST) + b_ref[...]


def _modulation(cc, ada_w, ada_b):
    depth, d, n = ada_w.shape
    tn = _pick(n, (1536, 1024, 512))
    rows = cc.shape[0]
    return pl.pallas_call(
        _mod_kernel,
        grid=(depth, n // tn),
        in_specs=[
            pl.BlockSpec((rows, d), lambda l, j: (0, 0)),
            pl.BlockSpec((None, d, tn), lambda l, j: (l, 0, j)),
            pl.BlockSpec((None, 1, tn), lambda l, j: (l, 0, j)),
        ],
        out_specs=pl.BlockSpec((None, rows, tn), lambda l, j: (l, 0, j)),
        out_shape=jax.ShapeDtypeStruct((depth, rows, n), F32),
        compiler_params=_params("arbitrary", "arbitrary"),
        name="modulation",
    )(cc, ada_w, ada_b.reshape(depth, 1, n))


def _vec_spec(arr):
    d = arr.shape[-1]
    if arr.shape[0] == 1:
        return pl.BlockSpec((None, 1, d), lambda b, i, *_: (0, 0, 0))
    return pl.BlockSpec((None, 1, d), lambda b, i, *_: (b, 0, 0))


def _attn_in_kernel(x_ref, g_ref, sc_ref, sh_ref, w_ref, gains_ref, cos_ref, sin_ref, bd_ref,
                    qa_ref, ka_ref, va_ref, qb_ref, kb_ref, vb_ref, *, rope, tm):
    lane = lax.broadcasted_iota(jnp.int32, (CHUNK, LANES), 1)
    even = (lane & 1) == 0
    bd = bd_ref[...]
    ones_rows = jnp.where(lax.broadcasted_iota(jnp.int32, (V_ROWS - HEAD_DIM, CHUNK), 0) == 0,
                          1.0, 0.0).astype(BF16)
    groups = ((0, qa_ref, ka_ref, va_ref, 0), (Q_WIDTH + 2 * KV_WIDTH, qb_ref, kb_ref, vb_ref, 2))

    for j in range(tm // CHUNK):
        rows = slice(j * CHUNK, (j + 1) * CHUNK)
        h = _norm_mod(x_ref[rows, :], g_ref[...], sc_ref[...], sh_ref[...]).astype(BF16)
        y = jnp.dot(h, w_ref[...], preferred_element_type=F32)

        def head_norm(yc, gain, rows=rows):
            ss = jnp.dot((yc * yc).astype(BF16), bd, preferred_element_type=F32)
            t = yc * lax.rsqrt(ss * (1.0 / HEAD_DIM) + EPS) * gain
            if rope:
                partner = jnp.where(even, pltpu.roll(t, LANES - 1, 1), pltpu.roll(t, 1, 1))
                t = t * cos_ref[rows, :] + partner * sin_ref[rows, :]
            return t

        for col0, q_ref, k_ref, v_ref, grow in groups:
            gq = gains_ref[grow:grow + 1, :]
            gk = gains_ref[grow + 1:grow + 2, :]
            for c in range(Q_WIDTH // LANES):
                t = head_norm(y[:, col0 + c * LANES:col0 + (c + 1) * LANES], gq)
                q_ref[c * LANES:(c + 1) * LANES, rows] = t.T.astype(BF16)
            kcol = col0 + Q_WIDTH
            k_ref[rows, 0:KV_WIDTH] = head_norm(y[:, kcol:kcol + KV_WIDTH], gk).astype(BF16)
            k_ref[rows, KV_WIDTH:K_AUG] = jnp.where(lane == 0, 1.0, 0.0).astype(BF16)
            vt = y[:, kcol + KV_WIDTH:kcol + 2 * KV_WIDTH].T.astype(BF16)
            for kv in range(N_KV):
                r0 = kv * V_ROWS
                v_ref[j, r0:r0 + HEAD_DIM, :] = vt[kv * HEAD_DIM:(kv + 1) * HEAD_DIM, :]
                v_ref[j, r0 + HEAD_DIM:r0 + V_ROWS, :] = ones_rows


def _attn_in(x, g, sc, sh, w_bf16, gains, cos_t, sin_t, bd, *, rope):
    b, t, d = x.shape
    tm = _pick(t, (512, 256))
    nch = t // CHUNK
    kern = functools.partial(_attn_in_kernel, rope=rope, tm=tm)
    q_spec = pl.BlockSpec((None, Q_WIDTH, tm), lambda bb, i: (bb, 0, i))
    k_spec = pl.BlockSpec((None, tm, K_AUG), lambda bb, i: (bb, i, 0))
    v_spec = pl.BlockSpec((None, tm // CHUNK, N_KV * V_ROWS, CHUNK), lambda bb, i: (bb, i, 0, 0))
    q_shape = jax.ShapeDtypeStruct((b, Q_WIDTH, t), BF16)
    k_shape = jax.ShapeDtypeStruct((b, t, K_AUG), BF16)
    v_shape = jax.ShapeDtypeStruct((b, nch, N_KV * V_ROWS, CHUNK), BF16)
    return pl.pallas_call(
        kern,
        grid=(b, t // tm),
        in_specs=[
            pl.BlockSpec((None, tm, d), lambda bb, i: (bb, i, 0)),
            pl.BlockSpec((1, d), lambda bb, i: (0, 0)),
            _vec_spec(sc), _vec_spec(sh),
            pl.BlockSpec((d, IN_PROJ_WIDTH), lambda bb, i: (0, 0)),
            pl.BlockSpec((4, LANES), lambda bb, i: (0, 0)),
            pl.BlockSpec((tm, LANES), lambda bb, i: (i, 0)),
            pl.BlockSpec((tm, LANES), lambda bb, i: (i, 0)),
            pl.BlockSpec((LANES, LANES), lambda bb, i: (0, 0)),
        ],
        out_specs=[q_spec, k_spec, v_spec, q_spec, k_spec, v_spec],
        out_shape=[q_shape, k_shape, v_shape, q_shape, k_shape, v_shape],
        compiler_params=_params("parallel", "parallel"),
        name="attn_in",
    )(x, g, sc, sh, w_bf16, gains, cos_t, sin_t, bd)


def _attn_kernel(q_ref, k_ref, v_ref, sink_ref, o_ref, qpad_ref, s_ref, cmax_ref, m_ref, l_ref,
                 acc_ref, *, mode, use_sink, n_lat, tq):
    i = pl.program_id(1)
    w = GROUP * tq
    zeros = jnp.zeros((HEAD_DIM, w), BF16)
    for kv in range(N_KV):
        heads = range(kv * GROUP, (kv + 1) * GROUP)
        q4 = jnp.concatenate([q_ref[h * HEAD_DIM:(h + 1) * HEAD_DIM, :] for h in heads], axis=1)
        qpad_ref[kv] = jnp.concatenate([q4, zeros] if kv == 0 else [zeros, q4], axis=0)
        if use_sink:
            m_ref[kv] = jnp.concatenate(
                [jnp.full((1, tq), sink_ref[h] * LOG2E, F32) for h in heads], axis=1)
            l_ref[kv] = jnp.ones((1, w), F32)
        else:
            m_ref[kv] = jnp.full((1, w), NEG_BIG, F32)
            l_ref[kv] = jnp.zeros((1, w), F32)
        acc_ref[kv] = jnp.zeros((HEAD_DIM, w), F32)

    def scores(c, slot, mask):
        start = c * CHUNK if isinstance(c, int) else pl.multiple_of(c * CHUNK, CHUNK)
        kc = k_ref[pl.ds(start, CHUNK), 0:KV_WIDTH]
        for kv in range(N_KV):
            s = jnp.dot(kc, qpad_ref[kv], preferred_element_type=F32)
            if mask is not None:
                s = jnp.where(mask, s, -jnp.inf)
            s_ref[slot, kv] = s
            cmax_ref[slot, kv] = jnp.max(s, axis=0, keepdims=True)

    def absorb(c, slot):
        for kv in range(N_KV):
            m = m_ref[kv]
            m_new = jnp.maximum(m, cmax_ref[slot, kv])
            alpha = jnp.exp2(m - m_new)
            p = jnp.exp2(s_ref[slot, kv] - m_new)
            l_ref[kv] = alpha * l_ref[kv] + jnp.sum(p, axis=0, keepdims=True)
            m_ref[kv] = m_new
            vc = v_ref[c, kv * V_ROWS:kv * V_ROWS + HEAD_DIM, :]
            acc_ref[kv] = alpha * acc_ref[kv] + jnp.dot(vc, p.astype(BF16),
                                                        preferred_element_type=F32)

    scores(0, 0, None)
    if mode == "global":
        def body(j, carry):
            c = 2 * j
            scores(c + 1, 1, None)
            absorb(c, 0)
            scores(c + 2, 0, None)
            absorb(c + 1, 1)
            return carry
        lax.fori_loop(0, n_lat // 2, body, 0)
        c_end = 2 * (n_lat // 2)
        if n_lat % 2:
            scores(c_end + 1, 1, None)
            absorb(c_end, 0)
            absorb(c_end + 1, 1)
        else:
            absorb(c_end, 0)
    elif mode == "window":
        row = lax.broadcasted_iota(jnp.int32, (CHUNK, w), 0)
        col = lax.broadcasted_iota(jnp.int32, (CHUNK, w), 1) & (tq - 1)
        rel0 = col - row
        prev_c, prev_slot = 0, 0
        for d in (-1, 0, 1):
            cl = i + d
            ok = jnp.logical_and(cl >= 0, cl < n_lat)
            c = 1 + jnp.clip(cl, 0, n_lat - 1)
            mask = jnp.logical_and(jnp.abs(rel0 - d * CHUNK) <= WINDOW, ok)
            scores(c, 1 - prev_slot, mask)
            absorb(prev_c, prev_slot)
            prev_c, prev_slot = c, 1 - prev_slot
        absorb(prev_c, prev_slot)
    else:
        absorb(0, 0)

    for kv in range(N_KV):
        out = acc_ref[kv] * (1.0 / l_ref[kv])
        for g in range(GROUP):
            h = kv * GROUP + g
            o_ref[h * HEAD_DIM:(h + 1) * HEAD_DIM, :] = out[:, g * tq:(g + 1) * tq].astype(BF16)


def _attention(q_t, k_all, v_all, sink, bound, *, mode, use_sink):
    del bound
    b, _, tq_total = q_t.shape
    nk = k_all.shape[1]
    nc = v_all.shape[1]
    tq = CHUNK
    kern = functools.partial(_attn_kernel, mode=mode, use_sink=use_sink, n_lat=nc - 1, tq=tq)
    return pl.pallas_call(
        kern,
        grid=(b, tq_total // tq),
        in_specs=[
            pl.BlockSpec((None, Q_WIDTH, tq), lambda bb, i: (bb, 0, i)),
            pl.BlockSpec((None, nk, K_AUG), lambda bb, i: (bb, 0, 0)),
            pl.BlockSpec((None, nc, N_KV * V_ROWS, CHUNK), lambda bb, i: (bb, 0, 0, 0)),
            pl.BlockSpec(memory_space=pltpu.SMEM),
        ],
        out_specs=pl.BlockSpec((None, Q_WIDTH, tq), lambda bb, i: (bb, 0, i)),
        out_shape=jax.ShapeDtypeStruct((b, Q_WIDTH, tq_total), BF16),
        scratch_shapes=[pltpu.VMEM((N_KV, KV_WIDTH, GROUP * tq), BF16),
                        pltpu.VMEM((2, N_KV, CHUNK, GROUP * tq), F32),
                        pltpu.VMEM((2, N_KV, 1, GROUP * tq), F32),
                        pltpu.VMEM((N_KV, 1, GROUP * tq), F32),
                        pltpu.VMEM((N_KV, 1, GROUP * tq), F32),
                        pltpu.VMEM((N_KV, HEAD_DIM, GROUP * tq), F32)],
        compiler_params=_params("parallel", "parallel"),
        name="attn_" + mode,
    )(q_t, k_all, v_all, sink)


def _attn_fast_kernel(q_ref, k_ref, v_ref, sink_ref, bound_ref, o_ref, qa_ref, acc_ref, p_ref, *,
                      mode, use_sink, n_lat, tq):
    i = pl.program_id(1)
    w = GROUP * tq
    bound = bound_ref[0]
    zeros = jnp.zeros((HEAD_DIM, w), BF16)
    row = lax.broadcasted_iota(jnp.int32, (KV_WIDTH, w), 0)
    shift = jnp.where(row == 0, -bound, 0.0).astype(BF16)
    for kv in range(N_KV):
        heads = range(kv * GROUP, (kv + 1) * GROUP)
        q4 = jnp.concatenate([q_ref[h * HEAD_DIM:(h + 1) * HEAD_DIM, :] for h in heads], axis=1)
        qa_ref[kv] = jnp.concatenate(([q4, zeros] if kv == 0 else [zeros, q4]) + [shift], axis=0)
        acc_ref[kv] = jnp.zeros((V_ROWS, w), F32)

    def update(c, mask, n=1):
        start = c * CHUNK if isinstance(c, int) else pl.multiple_of(c * CHUNK, CHUNK)
        kc = k_ref[pl.ds(start, n * CHUNK), :]
        for kv in range(N_KV):
            s = jnp.dot(kc, qa_ref[kv], preferred_element_type=F32)
            if mask is not None:
                s = jnp.where(mask, s, -jnp.inf)
            p = jnp.exp2(s).astype(BF16)
            vc = jnp.concatenate([v_ref[c + j, kv * V_ROWS:(kv + 1) * V_ROWS, :] for j in range(n)],
                                 axis=1)
            acc_ref[kv] += jnp.dot(vc, p, preferred_element_type=F32)

    def probs(c, n, slot):
        start = c * CHUNK if isinstance(c, int) else pl.multiple_of(c * CHUNK, CHUNK)
        kc = k_ref[pl.ds(start, n * CHUNK), :]
        for kv in range(N_KV):
            s = jnp.dot(kc, qa_ref[kv], preferred_element_type=F32)
            p_ref[slot, kv, 0:n * CHUNK, :] = jnp.exp2(s).astype(BF16)

    def values(c, n, slot):
        for kv in range(N_KV):
            vc = jnp.concatenate([v_ref[c + j, kv * V_ROWS:(kv + 1) * V_ROWS, :] for j in range(n)],
                                 axis=1)
            acc_ref[kv] += jnp.dot(vc, p_ref[slot, kv, 0:n * CHUNK, :], preferred_element_type=F32)

    if mode == "global":
        n_full = (1 + n_lat) // GLOBAL_STEP
        steps = [(GLOBAL_STEP * s, GLOBAL_STEP) for s in range(n_full)]
        steps += [(c, 1) for c in range(GLOBAL_STEP * n_full, 1 + n_lat)]
        n_pairs = max((n_full - 1) // 2, 0)

        probs(*steps[0], 0)

        def body(t, carry):
            c = 2 * GLOBAL_STEP * t
            probs(c + GLOBAL_STEP, GLOBAL_STEP, 1)
            values(c, GLOBAL_STEP, 0)
            probs(c + 2 * GLOBAL_STEP, GLOBAL_STEP, 0)
            values(c + GLOBAL_STEP, GLOBAL_STEP, 1)
            return carry
        lax.fori_loop(0, n_pairs, body, 0)
        pending, slot = steps[2 * n_pairs], 0
        for nxt in steps[2 * n_pairs + 1:]:
            probs(*nxt, 1 - slot)
            values(*pending, slot)
            pending, slot = nxt, 1 - slot
        values(*pending, slot)
    elif mode == "window":
        half = CHUNK - WINDOW
        c_next = jnp.minimum(i + 2, n_lat)
        starts = (0, pl.multiple_of(i * CHUNK + half, half), pl.multiple_of((i + 1) * CHUNK, CHUNK),
                  pl.multiple_of(c_next * CHUNK, CHUNK))
        sizes = (CHUNK, WINDOW, CHUNK, WINDOW)
        kc = jnp.concatenate([k_ref[pl.ds(st, sz), :] for st, sz in zip(starts, sizes)], axis=0)
        n_keys = sum(sizes)
        rowk = lax.broadcasted_iota(jnp.int32, (n_keys, w), 0) - CHUNK
        col = lax.broadcasted_iota(jnp.int32, (n_keys, w), 1) & (tq - 1)
        in_band = jnp.abs(col - rowk + WINDOW) <= WINDOW
        first = jnp.where(i > 0, 0, WINDOW)
        last = jnp.where(i + 1 < n_lat, 2 * WINDOW + CHUNK, WINDOW + CHUNK)
        exists = jnp.logical_and(rowk >= first, rowk < last)
        mask = jnp.logical_or(rowk < 0, jnp.logical_and(in_band, exists))
        for kv in range(N_KV):
            rows = slice(kv * V_ROWS, (kv + 1) * V_ROWS)
            s = jnp.where(mask, jnp.dot(kc, qa_ref[kv], preferred_element_type=F32), -jnp.inf)
            p = jnp.exp2(s).astype(BF16)
            vc = jnp.concatenate([v_ref[0, rows, :], v_ref[i, rows, half:CHUNK], v_ref[i + 1, rows, :],
                                  v_ref[c_next, rows, 0:WINDOW]], axis=1)
            acc_ref[kv] += jnp.dot(vc, p, preferred_element_type=F32)
    else:
        update(0, None)

    for kv in range(N_KV):
        acc = acc_ref[kv]
        l = acc[HEAD_DIM:HEAD_DIM + 1, :]
        if use_sink:
            l = l + jnp.concatenate(
                [jnp.full((1, tq), jnp.exp2(sink_ref[kv * GROUP + g] * LOG2E - bound), F32)
                 for g in range(GROUP)], axis=1)
        out = acc[0:HEAD_DIM, :] * (1.0 / l)
        for g in range(GROUP):
            h = kv * GROUP + g
            o_ref[h * HEAD_DIM:(h + 1) * HEAD_DIM, :] = out[:, g * tq:(g + 1) * tq].astype(BF16)


def _attention_fast(q_t, k_all, v_all, sink, bound, *, mode, use_sink):
    b, _, tq_total = q_t.shape
    nk = k_all.shape[1]
    nc = v_all.shape[1]
    tq = CHUNK
    kern = functools.partial(_attn_fast_kernel, mode=mode, use_sink=use_sink, n_lat=nc - 1, tq=tq)
    return pl.pallas_call(
        kern,
        grid=(b, tq_total // tq),
        in_specs=[
            pl.BlockSpec((None, Q_WIDTH, tq), lambda bb, i: (bb, 0, i)),
            pl.BlockSpec((None, nk, K_AUG), lambda bb, i: (bb, 0, 0)),
            pl.BlockSpec((None, nc, N_KV * V_ROWS, CHUNK), lambda bb, i: (bb, 0, 0, 0)),
            pl.BlockSpec(memory_space=pltpu.SMEM),
            pl.BlockSpec(memory_space=pltpu.SMEM),
        ],
        out_specs=pl.BlockSpec((None, Q_WIDTH, tq), lambda bb, i: (bb, 0, i)),
        out_shape=jax.ShapeDtypeStruct((b, Q_WIDTH, tq_total), BF16),
        scratch_shapes=[pltpu.VMEM((N_KV, K_AUG, GROUP * tq), BF16),
                        pltpu.VMEM((N_KV, V_ROWS, GROUP * tq), F32),
                        pltpu.VMEM((2, N_KV, (GLOBAL_STEP if mode == "global" else 1) * CHUNK,
                                    GROUP * tq), BF16)],
        compiler_params=_params("parallel", "parallel"),
        name="attn_fast_" + mode,
    )(q_t, k_all, v_all, sink, bound)


def _attend(fast_ok, *args, **kw):
    return lax.cond(fast_ok, functools.partial(_attention_fast, **kw),
                    functools.partial(_attention, **kw), *args)


_TN_DIMS = (((0,), (0,)), ((), ()))


def _attn_out_kernel(oa_ref, ob_ref, w_ref, x_ref, g1_ref, n2_ref, sc2_ref, sh2_ref, x1_ref, h2_ref):
    mix = lax.dot_general(oa_ref[...], w_ref[0:Q_WIDTH, :], _TN_DIMS, preferred_element_type=F32)
    mix = mix + lax.dot_general(ob_ref[...], w_ref[Q_WIDTH:2 * Q_WIDTH, :], _TN_DIMS,
                                preferred_element_type=F32)
    x1 = x_ref[...] + g1_ref[...] * mix
    x1_ref[...] = x1
    h2_ref[...] = _norm_mod(x1, n2_ref[...], sc2_ref[...], sh2_ref[...]).astype(BF16)


def _attn_out(oa, ob, w_bf16, x, g1, n2, sc2, sh2):
    b, t, d = x.shape
    tm = _pick(t, (512, 256))
    o_spec = pl.BlockSpec((None, Q_WIDTH, tm), lambda bb, i: (bb, 0, i))
    x_spec = pl.BlockSpec((None, tm, d), lambda bb, i: (bb, i, 0))
    return pl.pallas_call(
        _attn_out_kernel,
        grid=(b, t // tm),
        in_specs=[o_spec, o_spec,
                  pl.BlockSpec((2 * Q_WIDTH, d), lambda bb, i: (0, 0)),
                  x_spec, _vec_spec(g1),
                  pl.BlockSpec((1, d), lambda bb, i: (0, 0)),
                  _vec_spec(sc2), _vec_spec(sh2)],
        out_specs=[x_spec, x_spec],
        out_shape=[jax.ShapeDtypeStruct((b, t, d), F32), jax.ShapeDtypeStruct((b, t, d), BF16)],
        compiler_params=_params("parallel", "parallel"),
        name="attn_out",
    )(oa, ob, w_bf16, x, g1, n2, sc2, sh2)


def _ffn_kernel(*refs, gated, tm):
    if gated:
        h_ref, wg_ref, wu_ref, wd_ref, gate_ref, x_ref, g2_ref, o_ref, acc_ref = refs
    else:
        h_ref, wg_ref, wu_ref, wd_ref, x_ref, g2_ref, o_ref, acc_ref = refs
    e = pl.program_id(2)
    f = pl.program_id(3)
    first = jnp.logical_and(e == 0, f == 0)
    last = jnp.logical_and(e == pl.num_programs(2) - 1, f == pl.num_programs(3) - 1)

    @pl.when(first)
    def _():
        acc_ref[...] = jnp.zeros_like(acc_ref)

    h = h_ref[...]
    a = jnp.dot(h, wg_ref[...].astype(BF16), preferred_element_type=F32)
    u = jnp.dot(h, wu_ref[...].astype(BF16), preferred_element_type=F32)
    act = (a * _sigmoid(a) * u).astype(BF16)
    y = jnp.dot(act, wd_ref[...].astype(BF16), preferred_element_type=F32)
    if gated:
        lane = lax.broadcasted_iota(jnp.int32, (tm, LANES), 1)
        ge = jnp.sum(jnp.where(lane == e, gate_ref[...], 0.0), axis=-1, keepdims=True)
        y = y * ge
    acc_ref[...] += y

    @pl.when(last)
    def _():
        o_ref[...] = x_ref[...] + g2_ref[...] * acc_ref[...]


def _ffn(h2, wg, wu, wd, gates, x1, g2, e0=0, n_e=1):
    b, t, d = x1.shape
    f = wg.shape[-1]
    gated = gates is not None
    tm = _pick(t, (1024, 512, 256) if gated else (512, 256))
    resident = n_e == 1 and 3 * d * f * 2 <= VMEM_LIMIT_BYTES // 3
    tf = f if resident else _pick(f, (512, 256))
    mode = dict(pipeline_mode=pl.Buffered(1)) if resident else {}
    x_spec = pl.BlockSpec((None, tm, d), lambda bb, i, e, j: (bb, i, 0))
    in_specs = [x_spec,
                pl.BlockSpec((None, d, tf), lambda bb, i, e, j: (e0 + e, 0, j), **mode),
                pl.BlockSpec((None, d, tf), lambda bb, i, e, j: (e0 + e, 0, j), **mode),
                pl.BlockSpec((None, tf, d), lambda bb, i, e, j: (e0 + e, j, 0), **mode)]
    args = [h2, wg, wu, wd]
    if gated:
        in_specs.append(pl.BlockSpec((None, tm, LANES), lambda bb, i, e, j: (bb, i, 0)))
        args.append(gates)
    in_specs += [x_spec, _vec_spec(g2)]
    args += [x1, g2]
    return pl.pallas_call(
        functools.partial(_ffn_kernel, gated=gated, tm=tm),
        grid=(b, t // tm, n_e, f // tf),
        in_specs=in_specs,
        out_specs=x_spec,
        out_shape=jax.ShapeDtypeStruct((b, t, d), F32),
        scratch_shapes=[pltpu.VMEM((tm, d), F32)],
        compiler_params=_params("parallel", "parallel", "arbitrary", "arbitrary"),
        name="ffn_gated" if gated else "ffn_dense",
    )(*args)


def _conv_in_kernel(x_ref, g_ref, sc_ref, sh_ref, w_ref, bg_ref, z_ref, *, d):
    h = _norm_mod(x_ref[...], g_ref[...], sc_ref[...], sh_ref[...]).astype(BF16)
    y = jnp.dot(h, w_ref[...], preferred_element_type=F32)
    bg_ref[...] = y[:, 0:d].astype(BF16)
    z_ref[...] = (y[:, d:2 * d] * y[:, 2 * d:3 * d]).astype(BF16)


def _conv_in(x, g, sc, sh, w_bf16):
    b, t, d = x.shape
    tm = _pick(t, (512, 256))
    x_spec = pl.BlockSpec((None, tm, d), lambda bb, i: (bb, i, 0))
    return pl.pallas_call(
        functools.partial(_conv_in_kernel, d=d),
        grid=(b, t // tm),
        in_specs=[x_spec, pl.BlockSpec((1, d), lambda bb, i: (0, 0)), _vec_spec(sc), _vec_spec(sh),
                  pl.BlockSpec((d, 3 * d), lambda bb, i: (0, 0))],
        out_specs=[x_spec, x_spec],
        out_shape=[jax.ShapeDtypeStruct((b, t, d), BF16), jax.ShapeDtypeStruct((b, t, d), BF16)],
        compiler_params=_params("parallel", "parallel"),
        name="conv_in",
    )(x, g, sc, sh, w_bf16)


HALO = 16


def _conv_out_kernel(z_ref, zp_ref, zn_ref, bg_ref, cw_ref, w_ref, x_ref, g1_ref, n2_ref, sc2_ref,
                     sh2_ref, rw_ref, x1_ref, h2_ref, gate_ref, *, tm, n_experts):
    i = pl.program_id(1)
    z = z_ref[...].astype(F32)
    prev = jnp.where(i > 0, zp_ref[HALO - 1:HALO, :].astype(F32), 0.0)
    nxt = jnp.where(i < pl.num_programs(1) - 1, zn_ref[0:1, :].astype(F32), 0.0)
    row = lax.broadcasted_iota(jnp.int32, z.shape, 0)
    z_dn = jnp.where(row == 0, prev, pltpu.roll(z, 1, 0))
    z_up = jnp.where(row == tm - 1, nxt, pltpu.roll(z, tm - 1, 0))
    conv = z_dn * cw_ref[0:1, :] + z * cw_ref[1:2, :] + z_up * cw_ref[2:3, :]
    v = (bg_ref[...].astype(F32) * conv).astype(BF16)
    mix = jnp.dot(v, w_ref[...], preferred_element_type=F32)
    x1 = x_ref[...] + g1_ref[...] * mix
    x1_ref[...] = x1
    h2 = _norm_mod(x1, n2_ref[...], sc2_ref[...], sh2_ref[...])
    h2_ref[...] = h2.astype(h2_ref.dtype)

    h_hi = h2.astype(BF16)
    h_lo = (h2 - h_hi.astype(F32)).astype(BF16)
    logits = (jnp.dot(h_hi, rw_ref[0], preferred_element_type=F32)
              + jnp.dot(h_lo, rw_ref[0], preferred_element_type=F32)
              + jnp.dot(h_hi, rw_ref[1], preferred_element_type=F32))
    lane = lax.broadcasted_iota(jnp.int32, logits.shape, 1)
    lanef = lane.astype(F32)
    logits = jnp.where(lane < n_experts, logits, -jnp.inf)
    m1 = jnp.max(logits, axis=-1, keepdims=True)
    i1 = jnp.min(jnp.where(logits == m1, lanef, float(LANES)), axis=-1, keepdims=True)
    rest = jnp.where(lanef == i1, -jnp.inf, logits)
    m2 = jnp.max(rest, axis=-1, keepdims=True)
    i2 = jnp.min(jnp.where(rest == m2, lanef, float(LANES)), axis=-1, keepdims=True)
    e2 = jnp.exp(m2 - m1)
    w1 = 1.0 / (1.0 + e2)
    w2 = e2 * w1
    rec = jnp.where(lanef == i1, w1, 0.0) + jnp.where(lanef == i2, w2, 0.0)
    for k, val in enumerate((i1, i2, w1, w2)):
        rec = jnp.where(lane == n_experts + k, val, rec)
    gate_ref[...] = rec


def _conv_out(z, bg, conv_w, w_bf16, x, g1, n2, sc2, sh2, router_pad, n_experts, h2_dtype):
    b, t, d = x.shape
    tm = _pick(t, (512, 256))
    hb = tm // HALO
    n_halo = t // HALO
    x_spec = pl.BlockSpec((None, tm, d), lambda bb, i: (bb, i, 0))
    prev_spec = pl.BlockSpec((None, HALO, d), lambda bb, i: (bb, jnp.maximum(i * hb - 1, 0), 0))
    next_spec = pl.BlockSpec((None, HALO, d), lambda bb, i: (bb, jnp.minimum((i + 1) * hb, n_halo - 1), 0))
    full = lambda r, c: pl.BlockSpec((r, c), lambda bb, i: (0, 0))
    return pl.pallas_call(
        functools.partial(_conv_out_kernel, tm=tm, n_experts=n_experts),
        grid=(b, t // tm),
        in_specs=[x_spec, prev_spec, next_spec, x_spec, full(3, d), full(d, d), x_spec, _vec_spec(g1),
                  full(1, d), _vec_spec(sc2), _vec_spec(sh2),
                  pl.BlockSpec((2, d, LANES), lambda bb, i: (0, 0, 0))],
        out_specs=[x_spec, x_spec, pl.BlockSpec((None, tm, LANES), lambda bb, i: (bb, i, 0))],
        out_shape=[jax.ShapeDtypeStruct((b, t, d), F32), jax.ShapeDtypeStruct((b, t, d), h2_dtype),
                   jax.ShapeDtypeStruct((b, t, LANES), F32)],
        compiler_params=_params("parallel", "parallel"),
        name="conv_out",
    )(z, z, z, bg, conv_w, w_bf16, x, g1, n2, sc2, sh2, router_pad)


EXPERT_TILE = 1024
TOP_K = 2


def _rank_kernel(route_ref, rank_ref, count_ref, base_ref, *, tm, n_experts):
    @pl.when(pl.program_id(0) == 0)
    def _():
        base_ref[...] = jnp.zeros_like(base_ref)

    route = route_ref[...]
    lane = lax.broadcasted_iota(jnp.int32, (tm, LANES), 1)
    lanef = lane.astype(F32)
    i1 = jnp.sum(jnp.where(lane == n_experts, route, 0.0), axis=-1, keepdims=True)
    i2 = jnp.sum(jnp.where(lane == n_experts + 1, route, 0.0), axis=-1, keepdims=True)
    onehot = jnp.where(lanef == i1, 1.0, 0.0) + jnp.where(lanef == i2, 1.0, 0.0)
    r = lax.broadcasted_iota(jnp.int32, (tm, tm), 0)
    c = lax.broadcasted_iota(jnp.int32, (tm, tm), 1)
    lower = jnp.where(c < r, 1.0, 0.0).astype(BF16)
    before = jnp.dot(lower, onehot.astype(BF16), preferred_element_type=F32) + base_ref[...]
    r1 = jnp.sum(jnp.where(lanef == i1, before, 0.0), axis=-1, keepdims=True)
    r2 = jnp.sum(jnp.where(lanef == i2, before, 0.0), axis=-1, keepdims=True)
    packed = jnp.where(lane == 0, r1, jnp.where(lane == 1, r2, jnp.where(lane == 2, i1,
                       jnp.where(lane == 3, i2, 0.0))))
    rank_ref[...] = packed.T[0:8, :]
    base_ref[...] += jnp.sum(onehot, axis=0, keepdims=True)
    count_ref[...] = base_ref[...]


def _rank(route, n_experts):
    n = route.shape[0]
    tm = _pick(n, (512, 256))
    return pl.pallas_call(
        functools.partial(_rank_kernel, tm=tm, n_experts=n_experts),
        grid=(n // tm,),
        in_specs=[pl.BlockSpec((tm, LANES), lambda i: (i, 0))],
        out_specs=[pl.BlockSpec((8, tm), lambda i: (0, i)), pl.BlockSpec((1, LANES), lambda i: (0, 0))],
        out_shape=[jax.ShapeDtypeStruct((8, n), F32), jax.ShapeDtypeStruct((1, LANES), F32)],
        scratch_shapes=[pltpu.VMEM((1, LANES), F32)],
        compiler_params=_params("arbitrary"),
        name="moe_rank",
    )(route)


def _dispatch_kernel(dest_ref, h_ref, xs_in_ref, xs_ref, sem, *, tm):
    del xs_in_ref
    copies = []
    for r in range(tm):
        for k in range(TOP_K):
            cp = pltpu.make_async_copy(h_ref.at[pl.ds(r, 1), :],
                                       xs_ref.at[pl.ds(dest_ref[0, k * tm + r], 1), :], sem)
            cp.start(priority=k)
            copies.append(cp)
    for cp in copies:
        cp.wait()


def _dispatch(dest, h2, n_rows):
    n, d = h2.shape
    tm = dest.shape[-1] // TOP_K
    zeros = jnp.zeros((n_rows, d), h2.dtype)
    return pl.pallas_call(
        functools.partial(_dispatch_kernel, tm=tm),
        grid=(n // tm,),
        in_specs=[pl.BlockSpec((None, 1, TOP_K * tm), lambda i: (i, 0, 0), memory_space=pltpu.SMEM),
                  pl.BlockSpec((tm, d), lambda i: (i, 0)),
                  pl.BlockSpec(memory_space=pl.ANY)],
        out_specs=pl.BlockSpec(memory_space=pl.ANY),
        out_shape=jax.ShapeDtypeStruct((n_rows, d), h2.dtype),
        scratch_shapes=[pltpu.SemaphoreType.DMA(())],
        input_output_aliases={2: 0},
        compiler_params=_params("arbitrary"),
        name="moe_dispatch",
    )(dest, h2, zeros)


def _expert_kernel(te_ref, nu_ref, xs_ref, wg_ref, wu_ref, wd_ref, o_ref, hb_ref, acc_ref):
    del te_ref
    j = pl.program_id(0)
    f = pl.program_id(1)

    @pl.when(j < nu_ref[0])
    def _():
        @pl.when(f == 0)
        def _():
            hb_ref[...] = xs_ref[...].astype(BF16)
            acc_ref[...] = jnp.zeros_like(acc_ref)

        h = hb_ref[...]
        a = jnp.dot(h, wg_ref[...].astype(BF16), preferred_element_type=F32)
        u = jnp.dot(h, wu_ref[...].astype(BF16), preferred_element_type=F32)
        act = (a * _sigmoid(a) * u).astype(BF16)
        acc_ref[...] += jnp.dot(act, wd_ref[...].astype(BF16), preferred_element_type=F32)

        @pl.when(f == pl.num_programs(1) - 1)
        def _():
            o_ref[...] = acc_ref[...]

    @pl.when(jnp.logical_and(j >= nu_ref[0], f == pl.num_programs(1) - 1))
    def _():
        o_ref[...] = jnp.zeros_like(o_ref)


def _experts(tile_expert, n_used, xs, wg, wu, wd):
    n_rows, d = xs.shape
    f = wg.shape[-1]
    tf = _pick(f, (512, 256))
    nf = f // tf
    n_tiles = n_rows // EXPERT_TILE

    def row_map(j, ff, te, nu):
        return (jnp.minimum(j, nu[0] - 1), 0)

    def w_up_map(j, ff, te, nu):
        live = j < nu[0]
        return (te[jnp.minimum(j, nu[0] - 1)], 0, jnp.where(live, ff, nf - 1))

    def w_down_map(j, ff, te, nu):
        live = j < nu[0]
        return (te[jnp.minimum(j, nu[0] - 1)], jnp.where(live, ff, nf - 1), 0)

    grid_spec = pltpu.PrefetchScalarGridSpec(
        num_scalar_prefetch=2,
        grid=(n_tiles, nf),
        in_specs=[pl.BlockSpec((EXPERT_TILE, d), row_map),
                  pl.BlockSpec((None, d, tf), w_up_map),
                  pl.BlockSpec((None, d, tf), w_up_map),
                  pl.BlockSpec((None, tf, d), w_down_map)],
        out_specs=pl.BlockSpec((EXPERT_TILE, d), lambda j, ff, te, nu: (j, 0)),
        scratch_shapes=[pltpu.VMEM((EXPERT_TILE, d), BF16), pltpu.VMEM((EXPERT_TILE, d), F32)],
    )
    return pl.pallas_call(
        _expert_kernel,
        grid_spec=grid_spec,
        out_shape=jax.ShapeDtypeStruct((n_rows, d), F32),
        compiler_params=_params("arbitrary", "arbitrary"),
        name="moe_experts",
    )(tile_expert, n_used, xs, wg, wu, wd)


def _combine_kernel(dest_ref, next_ref, os_ref, route_ref, x_ref, g2_ref, o_ref, buf_ref, sem, *,
                    tm, n_experts):
    i = pl.program_id(0)
    slot = i % 2

    def gather(idx_ref, s, start):
        for r in range(tm):
            for k in range(TOP_K):
                cp = pltpu.make_async_copy(os_ref.at[pl.ds(idx_ref[0, k * tm + r], 1), :],
                                           buf_ref.at[s, k, pl.ds(r, 1), :], sem.at[s])
                if start:
                    cp.start(priority=k)
                else:
                    cp.wait()

    @pl.when(i == 0)
    def _():
        gather(dest_ref, slot, True)

    @pl.when(i + 1 < pl.num_programs(0))
    def _():
        gather(next_ref, 1 - slot, True)

    gather(dest_ref, slot, False)

    route = route_ref[...]
    lane = lax.broadcasted_iota(jnp.int32, (tm, LANES), 1)
    w1 = jnp.sum(jnp.where(lane == n_experts + 2, route, 0.0), axis=-1, keepdims=True)
    w2 = jnp.sum(jnp.where(lane == n_experts + 3, route, 0.0), axis=-1, keepdims=True)
    o_ref[...] = x_ref[...] + g2_ref[...] * (w1 * buf_ref[slot, 0] + w2 * buf_ref[slot, 1])


def _combine(dest, os, route, x1, g2, n_experts):
    b, t, d = x1.shape
    tm = dest.shape[-1] // TOP_K
    per_seq = t // tm
    n_steps = b * per_seq
    x_spec = pl.BlockSpec((None, tm, d), lambda i: (i // per_seq, i % per_seq, 0))
    if g2.shape[0] == 1:
        g_spec = pl.BlockSpec((None, 1, d), lambda i: (0, 0, 0))
    else:
        g_spec = pl.BlockSpec((None, 1, d), lambda i: (i // per_seq, 0, 0))
    return pl.pallas_call(
        functools.partial(_combine_kernel, tm=tm, n_experts=n_experts),
        grid=(n_steps,),
        in_specs=[pl.BlockSpec((None, 1, TOP_K * tm), lambda i: (i, 0, 0), memory_space=pltpu.SMEM),
                  pl.BlockSpec((None, 1, TOP_K * tm), lambda i: (jnp.minimum(i + 1, n_steps - 1), 0, 0),
                               memory_space=pltpu.SMEM),
                  pl.BlockSpec(memory_space=pl.ANY),
                  pl.BlockSpec((None, tm, LANES), lambda i: (i // per_seq, i % per_seq, 0)),
                  x_spec, g_spec],
        out_specs=x_spec,
        out_shape=jax.ShapeDtypeStruct((b, t, d), F32),
        scratch_shapes=[pltpu.VMEM((2, TOP_K, tm, d), F32), pltpu.SemaphoreType.DMA((2,))],
        compiler_params=_params("arbitrary"),
        name="moe_combine",
    )(dest, dest, os, route, x1, g2)


def _moe_routed(h2, route, x1, g2, wg, wu, wd, e0, n_experts):
    b, t, d = x1.shape
    n = b * t
    tm = _pick(t, (512, 256))
    rec = route.reshape(n, LANES)
    rank, counts = _rank(rec, n_experts)

    cnt = counts[0, :n_experts].astype(jnp.int32)
    padded = ((cnt + EXPERT_TILE - 1) // EXPERT_TILE) * EXPERT_TILE
    ends = jnp.cumsum(padded)
    offs = ends - padded
    n_tiles = (2 * n) // EXPERT_TILE + n_experts
    n_used = (ends[-1] // EXPERT_TILE).astype(jnp.int32).reshape(1)
    tile_start = jnp.arange(n_tiles, dtype=jnp.int32) * EXPERT_TILE
    tile_expert = jnp.minimum(jnp.sum(tile_start[:, None] >= ends[None, :], axis=1), n_experts - 1)
    slot = rank.astype(jnp.int32)
    d1 = jnp.take(offs, slot[2]) + slot[0]
    d2 = jnp.take(offs, slot[3]) + slot[1]
    dest = jnp.concatenate([d1.reshape(n // tm, tm), d2.reshape(n // tm, tm)], axis=1)[:, None, :]

    xs = _dispatch(dest, h2.reshape(n, d), n_tiles * EXPERT_TILE)
    os = _experts((e0 + tile_expert).astype(jnp.int32), n_used, xs, wg, wu, wd)
    return _combine(dest, os, route, x1, g2, n_experts)


def _rope_tables(n_tokens):
    rows = n_tokens // GRID_W
    row, col = jnp.meshgrid(jnp.arange(rows, dtype=F32), jnp.arange(GRID_W, dtype=F32), indexing="ij")
    half = HEAD_DIM // 2
    inv_freq = ROPE_THETA ** (-jnp.arange(0, half, 2, dtype=F32) / half)
    ang = jnp.concatenate([row.reshape(-1, 1) * inv_freq, col.reshape(-1, 1) * inv_freq], axis=-1)
    cos = jnp.repeat(jnp.cos(ang), 2, axis=-1)
    sign = jnp.tile(jnp.array([-1.0, 1.0], F32), HEAD_DIM // 2)
    sin = jnp.repeat(jnp.sin(ang), 2, axis=-1) * sign
    reps = LANES // HEAD_DIM
    return jnp.tile(cos, (1, reps)), jnp.tile(sin, (1, reps))


def kernel(x, c, ctx, c_ctx, ada_w, ada_b, norm1_g, norm2_g, attn_w_in, attn_w_out, qnorm_a, knorm_a, qnorm_b, knorm_b, sink_b, ffn_w_gate, ffn_w_up, ffn_w_down, conv_w_in, conv_w, conv_w_out, router_w, moe_w_gate, moe_w_up, moe_w_down):
    bsz, n_tok, d = x.shape
    n_ctx = ctx.shape[1]
    depth = ada_w.shape[0]
    n_experts = router_w.shape[-1]
    assert n_ctx == CHUNK and n_tok % CHUNK == 0 and bsz + 1 <= 8
    assert attn_w_in.shape[-1] == IN_PROJ_WIDTH and d % LANES == 0 and n_experts + 4 <= LANES

    cc = jnp.zeros((8, d), F32).at[:bsz].set(c).at[bsz].set(c_ctx)
    mod = _modulation(cc, ada_w, ada_b)

    def mod_vec(layer, j, is_ctx):
        m = mod[layer, :, j * d:(j + 1) * d]
        return m[bsz:bsz + 1].reshape(1, 1, d) if is_ctx else m[:bsz].reshape(bsz, 1, d)

    cos_l, sin_l = _rope_tables(n_tok)
    cos_c = jnp.ones((n_ctx, LANES), F32)
    sin_c = jnp.zeros((n_ctx, LANES), F32)
    head_id = jnp.arange(LANES) // HEAD_DIM
    bd = (head_id[:, None] == head_id[None, :]).astype(BF16)
    no_sink = jnp.zeros((N_HEADS,), F32)
    tile2 = lambda v: jnp.tile(v, LANES // HEAD_DIM)

    xc = ctx
    for layer in range(depth):
        i = layer // 2
        ctx_needed = any(j % 2 == 0 for j in range(layer + 1, depth))
        n1 = norm1_g[layer].reshape(1, d)
        n2 = norm2_g[layer].reshape(1, d)
        mv = lambda j, is_ctx: mod_vec(layer, j, is_ctx)

        if layer % 2 == 0:
            w_in = attn_w_in[i].astype(BF16)
            w_out = attn_w_out[i].astype(BF16)
            qscale = ATTN_SCALE * LOG2E
            gains = jnp.stack([tile2(qnorm_a[i]) * qscale, tile2(knorm_a[i]),
                               tile2(qnorm_b[i]) * qscale, tile2(knorm_b[i])]).astype(F32)
            sink = sink_b[i].astype(F32)
            def logit_bound(gq, gk):
                raw = HEAD_DIM * jnp.max(jnp.abs(gq)) * jnp.max(jnp.abs(gk)) * qscale * 1.02
                return raw.astype(BF16).astype(F32)
            bound_a = logit_bound(qnorm_a[i], knorm_a[i])
            bound_b = jnp.maximum(logit_bound(qnorm_b[i], knorm_b[i]),
                                  (jnp.max(sink) * LOG2E * 1.02).astype(BF16).astype(F32))
            fast_a = bound_a <= FAST_MAX_LOGIT
            fast_b = jnp.logical_and(bound_b <= FAST_MAX_LOGIT,
                                     jnp.max(jnp.abs(sink)) * LOG2E <= FAST_MAX_LOGIT)
            bound_a = bound_a.reshape(1)
            bound_b = bound_b.reshape(1)
            qa_l, ka_l, va_l, qb_l, kb_l, vb_l = _attn_in(
                x, n1, mv(1, False), mv(0, False), w_in, gains, cos_l, sin_l, bd, rope=True)
            qa_c, ka_c, va_c, qb_c, kb_c, vb_c = _attn_in(
                xc, n1, mv(1, True), mv(0, True), w_in, gains, cos_c, sin_c, bd, rope=False)
            ka_all = jnp.concatenate([ka_c, ka_l], axis=1)
            va_all = jnp.concatenate([va_c, va_l], axis=1)
            kb_all = jnp.concatenate([kb_c, kb_l], axis=1)
            vb_all = jnp.concatenate([vb_c, vb_l], axis=1)
            oa = _attend(fast_a, qa_l, ka_all, va_all, no_sink, bound_a, mode="global", use_sink=False)
            ob = _attend(fast_b, qb_l, kb_all, vb_all, sink, bound_b, mode="window", use_sink=True)
            x, h2 = _attn_out(oa, ob, w_out, x, mv(2, False), n2, mv(4, False), mv(3, False))
            if ctx_needed:
                oa_c = _attend(fast_a, qa_c, ka_c, va_c, no_sink, bound_a, mode="ctx", use_sink=False)
                ob_c = _attend(fast_b, qb_c, kb_c, vb_c, sink, bound_b, mode="ctx", use_sink=True)
                xc, h2c = _attn_out(oa_c, ob_c, w_out, xc, mv(2, True), n2, mv(4, True), mv(3, True))
            wg = ffn_w_gate.astype(BF16)
            wu = ffn_w_up.astype(BF16)
            wd = ffn_w_down.astype(BF16)
            x = _ffn(h2, wg, wu, wd, None, x, mv(5, False), e0=i)
            if ctx_needed:
                xc = _ffn(h2c, wg, wu, wd, None, xc, mv(5, True), e0=i)
        else:
            w_in = conv_w_in[i].astype(BF16)
            w_out = conv_w_out[i].astype(BF16)
            cw = conv_w[i].astype(F32)
            router_f32 = jnp.zeros((d, LANES), F32).at[:, :n_experts].set(router_w[i])
            router_hi = router_f32.astype(BF16)
            router_pad = jnp.stack([router_hi, (router_f32 - router_hi.astype(F32)).astype(BF16)])
            wg = moe_w_gate.reshape((-1,) + moe_w_gate.shape[2:])
            wu = moe_w_up.reshape((-1,) + moe_w_up.shape[2:])
            wd = moe_w_down.reshape((-1,) + moe_w_down.shape[2:])
            e0 = i * n_experts
            bg, z = _conv_in(x, n1, mv(1, False), mv(0, False), w_in)
            x, h2, route = _conv_out(z, bg, cw, w_out, x, mv(2, False), n2, mv(4, False), mv(3, False),
                                     router_pad, n_experts, F32)
            x = _moe_routed(h2, route, x, mv(5, False), wg, wu, wd, e0, n_experts)
            if ctx_needed:
                bg, z = _conv_in(xc, n1, mv(1, True), mv(0, True), w_in)
                xc, h2c, gates_c = _conv_out(z, bg, cw, w_out, xc, mv(2, True), n2, mv(4, True),
                                             mv(3, True), router_pad, n_experts, BF16)
                flat = lambda a: a.reshape((1, bsz * n_ctx) + a.shape[2:])
                xc = _ffn(flat(h2c), wg, wu, wd, flat(gates_c), flat(xc), mv(5, True),
                          e0=e0, n_e=n_experts).reshape(bsz, n_ctx, d)
    return x
```

```python
import functools

import jax
import jax.numpy as jnp
from jax import lax
from jax.experimental import pallas as pl
from jax.experimental.pallas import tpu as pltpu

F32 = jnp.float32
BF16 = jnp.bfloat16

HEAD_DIM = 64
N_KV = 2
GROUP = 4
N_HEADS = N_KV * GROUP
Q_WIDTH = N_HEADS * HEAD_DIM
KV_WIDTH = N_KV * HEAD_DIM
IN_PROJ_WIDTH = 2 * (Q_WIDTH + 2 * KV_WIDTH)
GRID_W = 64
WINDOW = 128
ROPE_THETA = 10000.0
ATTN_SCALE = HEAD_DIM ** -0.5
EPS = 1e-6
N_ADA = 6
LOG2E = 1.4426950408889634
LANES = 128
CHUNK = 256
K_AUG = 2 * KV_WIDTH
V_ROWS = HEAD_DIM + 16
GLOBAL_STEP = 8
FAST_MAX_LOGIT = 40.0
NEG_BIG = -1e30
VMEM_LIMIT_BYTES = 56 * 1024 * 1024


def _params(*sem):
    return pltpu.CompilerParams(dimension_semantics=sem, vmem_limit_bytes=VMEM_LIMIT_BYTES)


def _pick(n, candidates):
    for t in candidates:
        if n % t == 0:
            return t
    return n


def _sigmoid(a):
    return 1.0 / (1.0 + jnp.exp(-a))


def _norm_mod(x, g, sc, sh):
    ms = jnp.mean(x * x, axis=-1, keepdims=True)
    return x * lax.rsqrt(ms + EPS) * g * (1.0 + sc) + sh


def _mod_kernel(c_ref, w_ref, b_ref, o_ref):
    c = c_ref[...]
    s = c * _sigmoid(c)
    o_ref[...] = jnp.dot(s, w_ref[...], preferred_element_type=F32,
                         precision=lax.Precision.HIGHEST) + b_ref[...]


def _modulation(cc, ada_w, ada_b):
    depth, d, n = ada_w.shape
    tn = _pick(n, (1536, 1024, 512))
    rows = cc.shape[0]
    return pl.pallas_call(
        _mod_kernel,
        grid=(depth, n // tn),
        in_specs=[
            pl.BlockSpec((rows, d), lambda l, j: (0, 0)),
            pl.BlockSpec((None, d, tn), lambda l, j: (l, 0, j)),
            pl.BlockSpec((None, 1, tn), lambda l, j: (l, 0, j)),
        ],
        out_specs=pl.BlockSpec((None, rows, tn), lambda l, j: (l, 0, j)),
        out_shape=jax.ShapeDtypeStruct((depth, rows, n), F32),
        compiler_params=_params("arbitrary", "arbitrary"),
        name="modulation",
    )(cc, ada_w, ada_b.reshape(depth, 1, n))


def _vec_spec(arr):
    d = arr.shape[-1]
    if arr.shape[0] == 1:
        return pl.BlockSpec((None, 1, d), lambda b, i, *_: (0, 0, 0))
    return pl.BlockSpec((None, 1, d), lambda b, i, *_: (b, 0, 0))


def _attn_in_kernel(x_ref, g_ref, sc_ref, sh_ref, w_ref, gains_ref, cos_ref, sin_ref, bd_ref,
                    qa_ref, ka_ref, va_ref, qb_ref, kb_ref, vb_ref, *, rope, tm):
    lane = lax.broadcasted_iota(jnp.int32, (CHUNK, LANES), 1)
    even = (lane & 1) == 0
    bd = bd_ref[...]
    ones_rows = jnp.where(lax.broadcasted_iota(jnp.int32, (V_ROWS - HEAD_DIM, CHUNK), 0) == 0,
                          1.0, 0.0).astype(BF16)
    groups = ((0, qa_ref, ka_ref, va_ref, 0), (Q_WIDTH + 2 * KV_WIDTH, qb_ref, kb_ref, vb_ref, 2))

    for j in range(tm // CHUNK):
        rows = slice(j * CHUNK, (j + 1) * CHUNK)
        h = _norm_mod(x_ref[rows, :], g_ref[...], sc_ref[...], sh_ref[...]).astype(BF16)
        y = jnp.dot(h, w_ref[...], preferred_element_type=F32)

        def head_norm(yc, gain, rows=rows):
            ss = jnp.dot((yc * yc).astype(BF16), bd, preferred_element_type=F32)
            t = yc * lax.rsqrt(ss * (1.0 / HEAD_DIM) + EPS) * gain
            if rope:
                partner = jnp.where(even, pltpu.roll(t, LANES - 1, 1), pltpu.roll(t, 1, 1))
                t = t * cos_ref[rows, :] + partner * sin_ref[rows, :]
            return t

        for col0, q_ref, k_ref, v_ref, grow in groups:
            gq = gains_ref[grow:grow + 1, :]
            gk = gains_ref[grow + 1:grow + 2, :]
            for c in range(Q_WIDTH // LANES):
                t = head_norm(y[:, col0 + c * LANES:col0 + (c + 1) * LANES], gq)
                q_ref[c * LANES:(c + 1) * LANES, rows] = t.T.astype(BF16)
            kcol = col0 + Q_WIDTH
            k_ref[rows, 0:KV_WIDTH] = head_norm(y[:, kcol:kcol + KV_WIDTH], gk).astype(BF16)
            k_ref[rows, KV_WIDTH:K_AUG] = jnp.where(lane == 0, 1.0, 0.0).astype(BF16)
            vt = y[:, kcol + KV_WIDTH:kcol + 2 * KV_WIDTH].T.astype(BF16)
            for kv in range(N_KV):
                r0 = kv * V_ROWS
                v_ref[j, r0:r0 + HEAD_DIM, :] = vt[kv * HEAD_DIM:(kv + 1) * HEAD_DIM, :]
                v_ref[j, r0 + HEAD_DIM:r0 + V_ROWS, :] = ones_rows


def _attn_in(x, g, sc, sh, w_bf16, gains, cos_t, sin_t, bd, *, rope):
    b, t, d = x.shape
    tm = _pick(t, (512, 256))
    nch = t // CHUNK
    kern = functools.partial(_attn_in_kernel, rope=rope, tm=tm)
    q_spec = pl.BlockSpec((None, Q_WIDTH, tm), lambda bb, i: (bb, 0, i))
    k_spec = pl.BlockSpec((None, tm, K_AUG), lambda bb, i: (bb, i, 0))
    v_spec = pl.BlockSpec((None, tm // CHUNK, N_KV * V_ROWS, CHUNK), lambda bb, i: (bb, i, 0, 0))
    q_shape = jax.ShapeDtypeStruct((b, Q_WIDTH, t), BF16)
    k_shape = jax.ShapeDtypeStruct((b, t, K_AUG), BF16)
    v_shape = jax.ShapeDtypeStruct((b, nch, N_KV * V_ROWS, CHUNK), BF16)
    return pl.pallas_call(
        kern,
        grid=(b, t // tm),
        in_specs=[
            pl.BlockSpec((None, tm, d), lambda bb, i: (bb, i, 0)),
            pl.BlockSpec((1, d), lambda bb, i: (0, 0)),
            _vec_spec(sc), _vec_spec(sh),
            pl.BlockSpec((d, IN_PROJ_WIDTH), lambda bb, i: (0, 0)),
            pl.BlockSpec((4, LANES), lambda bb, i: (0, 0)),
            pl.BlockSpec((tm, LANES), lambda bb, i: (i, 0)),
            pl.BlockSpec((tm, LANES), lambda bb, i: (i, 0)),
            pl.BlockSpec((LANES, LANES), lambda bb, i: (0, 0)),
        ],
        out_specs=[q_spec, k_spec, v_spec, q_spec, k_spec, v_spec],
        out_shape=[q_shape, k_shape, v_shape, q_shape, k_shape, v_shape],
        compiler_params=_params("parallel", "parallel"),
        name="attn_in",
    )(x, g, sc, sh, w_bf16, gains, cos_t, sin_t, bd)


def _attn_kernel(q_ref, k_ref, v_ref, sink_ref, o_ref, qpad_ref, s_ref, cmax_ref, m_ref, l_ref,
                 acc_ref, *, mode, use_sink, n_lat, tq):
    i = pl.program_id(1)
    w = GROUP * tq
    zeros = jnp.zeros((HEAD_DIM, w), BF16)
    for kv in range(N_KV):
        heads = range(kv * GROUP, (kv + 1) * GROUP)
        q4 = jnp.concatenate([q_ref[h * HEAD_DIM:(h + 1) * HEAD_DIM, :] for h in heads], axis=1)
        qpad_ref[kv] = jnp.concatenate([q4, zeros] if kv == 0 else [zeros, q4], axis=0)
        if use_sink:
            m_ref[kv] = jnp.concatenate(
                [jnp.full((1, tq), sink_ref[h] * LOG2E, F32) for h in heads], axis=1)
            l_ref[kv] = jnp.ones((1, w), F32)
        else:
            m_ref[kv] = jnp.full((1, w), NEG_BIG, F32)
            l_ref[kv] = jnp.zeros((1, w), F32)
        acc_ref[kv] = jnp.zeros((HEAD_DIM, w), F32)

    def scores(c, slot, mask):
        start = c * CHUNK if isinstance(c, int) else pl.multiple_of(c * CHUNK, CHUNK)
        kc = k_ref[pl.ds(start, CHUNK), 0:KV_WIDTH]
        for kv in range(N_KV):
            s = jnp.dot(kc, qpad_ref[kv], preferred_element_type=F32)
            if mask is not None:
                s = jnp.where(mask, s, -jnp.inf)
            s_ref[slot, kv] = s
            cmax_ref[slot, kv] = jnp.max(s, axis=0, keepdims=True)

    def absorb(c, slot):
        for kv in range(N_KV):
            m = m_ref[kv]
            m_new = jnp.maximum(m, cmax_ref[slot, kv])
            alpha = jnp.exp2(m - m_new)
            p = jnp.exp2(s_ref[slot, kv] - m_new)
            l_ref[kv] = alpha * l_ref[kv] + jnp.sum(p, axis=0, keepdims=True)
            m_ref[kv] = m_new
            vc = v_ref[c, kv * V_ROWS:kv * V_ROWS + HEAD_DIM, :]
            acc_ref[kv] = alpha * acc_ref[kv] + jnp.dot(vc, p.astype(BF16),
                                                        preferred_element_type=F32)

    scores(0, 0, None)
    if mode == "global":
        def body(j, carry):
            c = 2 * j
            scores(c + 1, 1, None)
            absorb(c, 0)
            scores(c + 2, 0, None)
            absorb(c + 1, 1)
            return carry
        lax.fori_loop(0, n_lat // 2, body, 0)
        c_end = 2 * (n_lat // 2)
        if n_lat % 2:
            scores(c_end + 1, 1, None)
            absorb(c_end, 0)
            absorb(c_end + 1, 1)
        else:
            absorb(c_end, 0)
    elif mode == "window":
        row = lax.broadcasted_iota(jnp.int32, (CHUNK, w), 0)
        col = lax.broadcasted_iota(jnp.int32, (CHUNK, w), 1) & (tq - 1)
        rel0 = col - row
        prev_c, prev_slot = 0, 0
        for d in (-1, 0, 1):
            cl = i + d
            ok = jnp.logical_and(cl >= 0, cl < n_lat)
            c = 1 + jnp.clip(cl, 0, n_lat - 1)
            mask = jnp.logical_and(jnp.abs(rel0 - d * CHUNK) <= WINDOW, ok)
            scores(c, 1 - prev_slot, mask)
            absorb(prev_c, prev_slot)
            prev_c, prev_slot = c, 1 - prev_slot
        absorb(prev_c, prev_slot)
    else:
        absorb(0, 0)

    for kv in range(N_KV):
        out = acc_ref[kv] * (1.0 / l_ref[kv])
        for g in range(GROUP):
            h = kv * GROUP + g
            o_ref[h * HEAD_DIM:(h + 1) * HEAD_DIM, :] = out[:, g * tq:(g + 1) * tq].astype(BF16)


def _attention(q_t, k_all, v_all, sink, bound, *, mode, use_sink):
    del bound
    b, _, tq_total = q_t.shape
    nk = k_all.shape[1]
    nc = v_all.shape[1]
    tq = CHUNK
    kern = functools.partial(_attn_kernel, mode=mode, use_sink=use_sink, n_lat=nc - 1, tq=tq)
    return pl.pallas_call(
        kern,
        grid=(b, tq_total // tq),
        in_specs=[
            pl.BlockSpec((None, Q_WIDTH, tq), lambda bb, i: (bb, 0, i)),
            pl.BlockSpec((None, nk, K_AUG), lambda bb, i: (bb, 0, 0)),
            pl.BlockSpec((None, nc, N_KV * V_ROWS, CHUNK), lambda bb, i: (bb, 0, 0, 0)),
            pl.BlockSpec(memory_space=pltpu.SMEM),
        ],
        out_specs=pl.BlockSpec((None, Q_WIDTH, tq), lambda bb, i: (bb, 0, i)),
        out_shape=jax.ShapeDtypeStruct((b, Q_WIDTH, tq_total), BF16),
        scratch_shapes=[pltpu.VMEM((N_KV, KV_WIDTH, GROUP * tq), BF16),
                        pltpu.VMEM((2, N_KV, CHUNK, GROUP * tq), F32),
                        pltpu.VMEM((2, N_KV, 1, GROUP * tq), F32),
                        pltpu.VMEM((N_KV, 1, GROUP * tq), F32),
                        pltpu.VMEM((N_KV, 1, GROUP * tq), F32),
                        pltpu.VMEM((N_KV, HEAD_DIM, GROUP * tq), F32)],
        compiler_params=_params("parallel", "parallel"),
        name="attn_" + mode,
    )(q_t, k_all, v_all, sink)


def _attn_fast_kernel(q_ref, k_ref, v_ref, sink_ref, bound_ref, o_ref, qa_ref, acc_ref, p_ref, *,
                      mode, use_sink, n_lat, tq):
    i = pl.program_id(1)
    w = GROUP * tq
    bound = bound_ref[0]
    zeros = jnp.zeros((HEAD_DIM, w), BF16)
    row = lax.broadcasted_iota(jnp.int32, (KV_WIDTH, w), 0)
    shift = jnp.where(row == 0, -bound, 0.0).astype(BF16)
    for kv in range(N_KV):
        heads = range(kv * GROUP, (kv + 1) * GROUP)
        q4 = jnp.concatenate([q_ref[h * HEAD_DIM:(h + 1) * HEAD_DIM, :] for h in heads], axis=1)
        qa_ref[kv] = jnp.concatenate(([q4, zeros] if kv == 0 else [zeros, q4]) + [shift], axis=0)
        acc_ref[kv] = jnp.zeros((V_ROWS, w), F32)

    def update(c, mask, n=1):
        start = c * CHUNK if isinstance(c, int) else pl.multiple_of(c * CHUNK, CHUNK)
        kc = k_ref[pl.ds(start, n * CHUNK), :]
        for kv in range(N_KV):
            s = jnp.dot(kc, qa_ref[kv], preferred_element_type=F32)
            if mask is not None:
                s = jnp.where(mask, s, -jnp.inf)
            p = jnp.exp2(s).astype(BF16)
            vc = jnp.concatenate([v_ref[c + j, kv * V_ROWS:(kv + 1) * V_ROWS, :] for j in range(n)],
                                 axis=1)
            acc_ref[kv] += jnp.dot(vc, p, preferred_element_type=F32)

    def probs(c, n, slot):
        start = c * CHUNK if isinstance(c, int) else pl.multiple_of(c * CHUNK, CHUNK)
        kc = k_ref[pl.ds(start, n * CHUNK), :]
        for kv in range(N_KV):
            s = jnp.dot(kc, qa_ref[kv], preferred_element_type=F32)
            p_ref[slot, kv, 0:n * CHUNK, :] = jnp.exp2(s).astype(BF16)

    def values(c, n, slot):
        for kv in range(N_KV):
            vc = jnp.concatenate([v_ref[c + j, kv * V_ROWS:(kv + 1) * V_ROWS, :] for j in range(n)],
                                 axis=1)
            acc_ref[kv] += jnp.dot(vc, p_ref[slot, kv, 0:n * CHUNK, :], preferred_element_type=F32)

    if mode == "global":
        n_full = (1 + n_lat) // GLOBAL_STEP
        steps = [(GLOBAL_STEP * s, GLOBAL_STEP) for s in range(n_full)]
        steps += [(c, 1) for c in range(GLOBAL_STEP * n_full, 1 + n_lat)]
        n_pairs = max((n_full - 1) // 2, 0)

        probs(*steps[0], 0)

        def body(t, carry):
            c = 2 * GLOBAL_STEP * t
            probs(c + GLOBAL_STEP, GLOBAL_STEP, 1)
            values(c, GLOBAL_STEP, 0)
            probs(c + 2 * GLOBAL_STEP, GLOBAL_STEP, 0)
            values(c + GLOBAL_STEP, GLOBAL_STEP, 1)
            return carry
        lax.fori_loop(0, n_pairs, body, 0)
        pending, slot = steps[2 * n_pairs], 0
        for nxt in steps[2 * n_pairs + 1:]:
            probs(*nxt, 1 - slot)
            values(*pending, slot)
            pending, slot = nxt, 1 - slot
        values(*pending, slot)
    elif mode == "window":
        half = CHUNK - WINDOW
        c_next = jnp.minimum(i + 2, n_lat)
        starts = (0, pl.multiple_of(i * CHUNK + half, half), pl.multiple_of((i + 1) * CHUNK, CHUNK),
                  pl.multiple_of(c_next * CHUNK, CHUNK))
        sizes = (CHUNK, WINDOW, CHUNK, WINDOW)
        kc = jnp.concatenate([k_ref[pl.ds(st, sz), :] for st, sz in zip(starts, sizes)], axis=0)
        n_keys = sum(sizes)
        rowk = lax.broadcasted_iota(jnp.int32, (n_keys, w), 0) - CHUNK
        col = lax.broadcasted_iota(jnp.int32, (n_keys, w), 1) & (tq - 1)
        in_band = jnp.abs(col - rowk + WINDOW) <= WINDOW
        first = jnp.where(i > 0, 0, WINDOW)
        last = jnp.where(i + 1 < n_lat, 2 * WINDOW + CHUNK, WINDOW + CHUNK)
        exists = jnp.logical_and(rowk >= first, rowk < last)
        mask = jnp.logical_or(rowk < 0, jnp.logical_and(in_band, exists))
        for kv in range(N_KV):
            rows = slice(kv * V_ROWS, (kv + 1) * V_ROWS)
            s = jnp.where(mask, jnp.dot(kc, qa_ref[kv], preferred_element_type=F32), -jnp.inf)
            p = jnp.exp2(s).astype(BF16)
            vc = jnp.concatenate([v_ref[0, rows, :], v_ref[i, rows, half:CHUNK], v_ref[i + 1, rows, :],
                                  v_ref[c_next, rows, 0:WINDOW]], axis=1)
            acc_ref[kv] += jnp.dot(vc, p, preferred_element_type=F32)
    else:
        update(0, None)

    for kv in range(N_KV):
        acc = acc_ref[kv]
        l = acc[HEAD_DIM:HEAD_DIM + 1, :]
        if use_sink:
            l = l + jnp.concatenate(
                [jnp.full((1, tq), jnp.exp2(sink_ref[kv * GROUP + g] * LOG2E - bound), F32)
                 for g in range(GROUP)], axis=1)
        out = acc[0:HEAD_DIM, :] * (1.0 / l)
        for g in range(GROUP):
            h = kv * GROUP + g
            o_ref[h * HEAD_DIM:(h + 1) * HEAD_DIM, :] = out[:, g * tq:(g + 1) * tq].astype(BF16)


def _attention_fast(q_t, k_all, v_all, sink, bound, *, mode, use_sink):
    b, _, tq_total = q_t.shape
    nk = k_all.shape[1]
    nc = v_all.shape[1]
    tq = CHUNK
    kern = functools.partial(_attn_fast_kernel, mode=mode, use_sink=use_sink, n_lat=nc - 1, tq=tq)
    return pl.pallas_call(
        kern,
        grid=(b, tq_total // tq),
        in_specs=[
            pl.BlockSpec((None, Q_WIDTH, tq), lambda bb, i: (bb, 0, i)),
            pl.BlockSpec((None, nk, K_AUG), lambda bb, i: (bb, 0, 0)),
            pl.BlockSpec((None, nc, N_KV * V_ROWS, CHUNK), lambda bb, i: (bb, 0, 0, 0)),
            pl.BlockSpec(memory_space=pltpu.SMEM),
            pl.BlockSpec(memory_space=pltpu.SMEM),
        ],
        out_specs=pl.BlockSpec((None, Q_WIDTH, tq), lambda bb, i: (bb, 0, i)),
        out_shape=jax.ShapeDtypeStruct((b, Q_WIDTH, tq_total), BF16),
        scratch_shapes=[pltpu.VMEM((N_KV, K_AUG, GROUP * tq), BF16),
                        pltpu.VMEM((N_KV, V_ROWS, GROUP * tq), F32),
                        pltpu.VMEM((2, N_KV, (GLOBAL_STEP if mode == "global" else 1) * CHUNK,
                                    GROUP * tq), BF16)],
        compiler_params=_params("parallel", "parallel"),
        name="attn_fast_" + mode,
    )(q_t, k_all, v_all, sink, bound)


def _attend(fast_ok, *args, **kw):
    return lax.cond(fast_ok, functools.partial(_attention_fast, **kw),
                    functools.partial(_attention, **kw), *args)


_TN_DIMS = (((0,), (0,)), ((), ()))


def _attn_out_kernel(oa_ref, ob_ref, w_ref, x_ref, g1_ref, n2_ref, sc2_ref, sh2_ref, x1_ref, h2_ref):
    tm = x_ref.shape[0]
    sub = min(tm, CHUNK)
    for j in range(tm // sub):
        rows = slice(j * sub, (j + 1) * sub)
        mix = lax.dot_general(oa_ref[:, rows], w_ref[0:Q_WIDTH, :], _TN_DIMS, preferred_element_type=F32)
        mix = mix + lax.dot_general(ob_ref[:, rows], w_ref[Q_WIDTH:2 * Q_WIDTH, :], _TN_DIMS,
                                    preferred_element_type=F32)
        x1 = x_ref[rows, :] + g1_ref[...] * mix
        x1_ref[rows, :] = x1
        h2_ref[rows, :] = _norm_mod(x1, n2_ref[...], sc2_ref[...], sh2_ref[...]).astype(BF16)


def _attn_out(oa, ob, w_bf16, x, g1, n2, sc2, sh2):
    b, t, d = x.shape
    tm = _pick(t, (512, 256))
    o_spec = pl.BlockSpec((None, Q_WIDTH, tm), lambda bb, i: (bb, 0, i))
    x_spec = pl.BlockSpec((None, tm, d), lambda bb, i: (bb, i, 0))
    return pl.pallas_call(
        _attn_out_kernel,
        grid=(b, t // tm),
        in_specs=[o_spec, o_spec,
                  pl.BlockSpec((2 * Q_WIDTH, d), lambda bb, i: (0, 0)),
                  x_spec, _vec_spec(g1),
                  pl.BlockSpec((1, d), lambda bb, i: (0, 0)),
                  _vec_spec(sc2), _vec_spec(sh2)],
        out_specs=[x_spec, x_spec],
        out_shape=[jax.ShapeDtypeStruct((b, t, d), F32), jax.ShapeDtypeStruct((b, t, d), BF16)],
        compiler_params=_params("parallel", "parallel"),
        name="attn_out",
    )(oa, ob, w_bf16, x, g1, n2, sc2, sh2)


def _ffn_kernel(*refs, gated, tm):
    if gated:
        h_ref, wg_ref, wu_ref, wd_ref, gate_ref, x_ref, g2_ref, o_ref, acc_ref = refs
    else:
        h_ref, wg_ref, wu_ref, wd_ref, x_ref, g2_ref, o_ref, acc_ref = refs
    e = pl.program_id(2)
    f = pl.program_id(3)
    first = jnp.logical_and(e == 0, f == 0)
    last = jnp.logical_and(e == pl.num_programs(2) - 1, f == pl.num_programs(3) - 1)

    @pl.when(first)
    def _():
        acc_ref[...] = jnp.zeros_like(acc_ref)

    h = h_ref[...]
    a = jnp.dot(h, wg_ref[...].astype(BF16), preferred_element_type=F32)
    u = jnp.dot(h, wu_ref[...].astype(BF16), preferred_element_type=F32)
    act = (a * _sigmoid(a) * u).astype(BF16)
    y = jnp.dot(act, wd_ref[...].astype(BF16), preferred_element_type=F32)
    if gated:
        lane = lax.broadcasted_iota(jnp.int32, (tm, LANES), 1)
        ge = jnp.sum(jnp.where(lane == e, gate_ref[...], 0.0), axis=-1, keepdims=True)
        y = y * ge
    acc_ref[...] += y

    @pl.when(last)
    def _():
        o_ref[...] = x_ref[...] + g2_ref[...] * acc_ref[...]


def _ffn(h2, wg, wu, wd, gates, x1, g2, e0=0, n_e=1):
    b, t, d = x1.shape
    f = wg.shape[-1]
    gated = gates is not None
    tm = _pick(t, (1024, 512, 256) if gated else (512, 256))
    resident = n_e == 1 and 3 * d * f * 2 <= VMEM_LIMIT_BYTES // 3
    tf = f if resident else _pick(f, (512, 256))
    mode = dict(pipeline_mode=pl.Buffered(1)) if resident else {}
    x_spec = pl.BlockSpec((None, tm, d), lambda bb, i, e, j: (bb, i, 0))
    in_specs = [x_spec,
                pl.BlockSpec((None, d, tf), lambda bb, i, e, j: (e0 + e, 0, j), **mode),
                pl.BlockSpec((None, d, tf), lambda bb, i, e, j: (e0 + e, 0, j), **mode),
                pl.BlockSpec((None, tf, d), lambda bb, i, e, j: (e0 + e, j, 0), **mode)]
    args = [h2, wg, wu, wd]
    if gated:
        in_specs.append(pl.BlockSpec((None, tm, LANES), lambda bb, i, e, j: (bb, i, 0)))
        args.append(gates)
    in_specs += [x_spec, _vec_spec(g2)]
    args += [x1, g2]
    return pl.pallas_call(
        functools.partial(_ffn_kernel, gated=gated, tm=tm),
        grid=(b, t // tm, n_e, f // tf),
        in_specs=in_specs,
        out_specs=x_spec,
        out_shape=jax.ShapeDtypeStruct((b, t, d), F32),
        scratch_shapes=[pltpu.VMEM((tm, d), F32)],
        compiler_params=_params("parallel", "parallel", "arbitrary", "arbitrary"),
        name="ffn_gated" if gated else "ffn_dense",
    )(*args)


def _conv_in_kernel(x_ref, g_ref, sc_ref, sh_ref, w_ref, bg_ref, z_ref, *, d):
    h = _norm_mod(x_ref[...], g_ref[...], sc_ref[...], sh_ref[...]).astype(BF16)
    y = jnp.dot(h, w_ref[...], preferred_element_type=F32)
    bg_ref[...] = y[:, 0:d].astype(BF16)
    z_ref[...] = (y[:, d:2 * d] * y[:, 2 * d:3 * d]).astype(BF16)


def _conv_in(x, g, sc, sh, w_bf16):
    b, t, d = x.shape
    tm = _pick(t, (512, 256))
    x_spec = pl.BlockSpec((None, tm, d), lambda bb, i: (bb, i, 0))
    return pl.pallas_call(
        functools.partial(_conv_in_kernel, d=d),
        grid=(b, t // tm),
        in_specs=[x_spec, pl.BlockSpec((1, d), lambda bb, i: (0, 0)), _vec_spec(sc), _vec_spec(sh),
                  pl.BlockSpec((d, 3 * d), lambda bb, i: (0, 0))],
        out_specs=[x_spec, x_spec],
        out_shape=[jax.ShapeDtypeStruct((b, t, d), BF16), jax.ShapeDtypeStruct((b, t, d), BF16)],
        compiler_params=_params("parallel", "parallel"),
        name="conv_in",
    )(x, g, sc, sh, w_bf16)


HALO = 16


def _conv_out_kernel(z_ref, zp_ref, zn_ref, bg_ref, cw_ref, w_ref, x_ref, g1_ref, n2_ref, sc2_ref,
                     sh2_ref, rw_ref, x1_ref, h2_ref, gate_ref, *, tm, n_experts):
    i = pl.program_id(1)
    z = z_ref[...].astype(F32)
    prev = jnp.where(i > 0, zp_ref[HALO - 1:HALO, :].astype(F32), 0.0)
    nxt = jnp.where(i < pl.num_programs(1) - 1, zn_ref[0:1, :].astype(F32), 0.0)
    row = lax.broadcasted_iota(jnp.int32, z.shape, 0)
    z_dn = jnp.where(row == 0, prev, pltpu.roll(z, 1, 0))
    z_up = jnp.where(row == tm - 1, nxt, pltpu.roll(z, tm - 1, 0))
    conv = z_dn * cw_ref[0:1, :] + z * cw_ref[1:2, :] + z_up * cw_ref[2:3, :]
    v = (bg_ref[...].astype(F32) * conv).astype(BF16)
    sub = min(tm, CHUNK)
    lane = lax.broadcasted_iota(jnp.int32, (sub, LANES), 1)
    lanef = lane.astype(F32)
    for j in range(tm // sub):
        rows = slice(j * sub, (j + 1) * sub)
        mix = jnp.dot(v[rows, :], w_ref[...], preferred_element_type=F32)
        x1 = x_ref[rows, :] + g1_ref[...] * mix
        x1_ref[rows, :] = x1
        h2 = _norm_mod(x1, n2_ref[...], sc2_ref[...], sh2_ref[...])
        h2_ref[rows, :] = h2.astype(h2_ref.dtype)

        h_hi = h2.astype(BF16)
        h_lo = (h2 - h_hi.astype(F32)).astype(BF16)
        logits = (jnp.dot(h_hi, rw_ref[0], preferred_element_type=F32)
                  + jnp.dot(h_lo, rw_ref[0], preferred_element_type=F32)
                  + jnp.dot(h_hi, rw_ref[1], preferred_element_type=F32))
        logits = jnp.where(lane < n_experts, logits, -jnp.inf)
        m1 = jnp.max(logits, axis=-1, keepdims=True)
        i1 = jnp.min(jnp.where(logits == m1, lanef, float(LANES)), axis=-1, keepdims=True)
        rest = jnp.where(lanef == i1, -jnp.inf, logits)
        m2 = jnp.max(rest, axis=-1, keepdims=True)
        i2 = jnp.min(jnp.where(rest == m2, lanef, float(LANES)), axis=-1, keepdims=True)
        e2 = jnp.exp(m2 - m1)
        w1 = 1.0 / (1.0 + e2)
        w2 = e2 * w1
        rec = jnp.where(lanef == i1, w1, 0.0) + jnp.where(lanef == i2, w2, 0.0)
        for k, val in enumerate((i1, i2, w1, w2)):
            rec = jnp.where(lane == n_experts + k, val, rec)
        gate_ref[rows, :] = rec


def _conv_out(z, bg, conv_w, w_bf16, x, g1, n2, sc2, sh2, router_pad, n_experts, h2_dtype):
    b, t, d = x.shape
    tm = _pick(t, (512, 256))
    hb = tm // HALO
    n_halo = t // HALO
    x_spec = pl.BlockSpec((None, tm, d), lambda bb, i: (bb, i, 0))
    prev_spec = pl.BlockSpec((None, HALO, d), lambda bb, i: (bb, jnp.maximum(i * hb - 1, 0), 0))
    next_spec = pl.BlockSpec((None, HALO, d), lambda bb, i: (bb, jnp.minimum((i + 1) * hb, n_halo - 1), 0))
    full = lambda r, c: pl.BlockSpec((r, c), lambda bb, i: (0, 0))
    return pl.pallas_call(
        functools.partial(_conv_out_kernel, tm=tm, n_experts=n_experts),
        grid=(b, t // tm),
        in_specs=[x_spec, prev_spec, next_spec, x_spec, full(3, d), full(d, d), x_spec, _vec_spec(g1),
                  full(1, d), _vec_spec(sc2), _vec_spec(sh2),
                  pl.BlockSpec((2, d, LANES), lambda bb, i: (0, 0, 0))],
        out_specs=[x_spec, x_spec, pl.BlockSpec((None, tm, LANES), lambda bb, i: (bb, i, 0))],
        out_shape=[jax.ShapeDtypeStruct((b, t, d), F32), jax.ShapeDtypeStruct((b, t, d), h2_dtype),
                   jax.ShapeDtypeStruct((b, t, LANES), F32)],
        compiler_params=_params("parallel", "parallel"),
        name="conv_out",
    )(z, z, z, bg, conv_w, w_bf16, x, g1, n2, sc2, sh2, router_pad)


EXPERT_TILE = 1024
TOP_K = 2


def _rank_kernel(route_ref, rank_ref, count_ref, base_ref, *, tm, n_experts):
    @pl.when(pl.program_id(0) == 0)
    def _():
        base_ref[...] = jnp.zeros_like(base_ref)

    route = route_ref[...]
    lane = lax.broadcasted_iota(jnp.int32, (tm, LANES), 1)
    lanef = lane.astype(F32)
    i1 = jnp.sum(jnp.where(lane == n_experts, route, 0.0), axis=-1, keepdims=True)
    i2 = jnp.sum(jnp.where(lane == n_experts + 1, route, 0.0), axis=-1, keepdims=True)
    onehot = jnp.where(lanef == i1, 1.0, 0.0) + jnp.where(lanef == i2, 1.0, 0.0)
    r = lax.broadcasted_iota(jnp.int32, (tm, tm), 0)
    c = lax.broadcasted_iota(jnp.int32, (tm, tm), 1)
    lower = jnp.where(c < r, 1.0, 0.0).astype(BF16)
    before = jnp.dot(lower, onehot.astype(BF16), preferred_element_type=F32) + base_ref[...]
    r1 = jnp.sum(jnp.where(lanef == i1, before, 0.0), axis=-1, keepdims=True)
    r2 = jnp.sum(jnp.where(lanef == i2, before, 0.0), axis=-1, keepdims=True)
    packed = jnp.where(lane == 0, r1, jnp.where(lane == 1, r2, jnp.where(lane == 2, i1,
                       jnp.where(lane == 3, i2, 0.0))))
    rank_ref[...] = packed.T[0:8, :]
    base_ref[...] += jnp.sum(onehot, axis=0, keepdims=True)
    count_ref[...] = base_ref[...]


def _rank(route, n_experts):
    n = route.shape[0]
    tm = _pick(n, (512, 256))
    return pl.pallas_call(
        functools.partial(_rank_kernel, tm=tm, n_experts=n_experts),
        grid=(n // tm,),
        in_specs=[pl.BlockSpec((tm, LANES), lambda i: (i, 0))],
        out_specs=[pl.BlockSpec((8, tm), lambda i: (0, i)), pl.BlockSpec((1, LANES), lambda i: (0, 0))],
        out_shape=[jax.ShapeDtypeStruct((8, n), F32), jax.ShapeDtypeStruct((1, LANES), F32)],
        scratch_shapes=[pltpu.VMEM((1, LANES), F32)],
        compiler_params=_params("arbitrary"),
        name="moe_rank",
    )(route)


ZERO_ROWS = 128


def _dispatch_kernel(dest_ref, pad_ref, h_ref, xs_ref, zero_ref, sem, zsem, *, tm, n_ranges):
    copies = []
    for r in range(tm):
        for k in range(TOP_K):
            cp = pltpu.make_async_copy(h_ref.at[pl.ds(r, 1), :],
                                       xs_ref.at[pl.ds(dest_ref[0, k * tm + r], 1), :], sem)
            cp.start(priority=k)
            copies.append(cp)

    @pl.when(pl.program_id(0) == pl.num_programs(0) - 1)
    def _():
        zero_ref[...] = jnp.zeros_like(zero_ref)

        def row_clear(r):
            return pltpu.make_async_copy(zero_ref.at[pl.ds(0, 1), :], xs_ref.at[pl.ds(r, 1), :], zsem)

        def block_clear(r):
            start = pl.multiple_of(r, ZERO_ROWS)
            return pltpu.make_async_copy(zero_ref, xs_ref.at[pl.ds(start, ZERO_ROWS), :], zsem)

        for e in range(n_ranges):
            lo, hi = pad_ref[0, e], pad_ref[1, e]
            if e < n_ranges - 1:
                lax.fori_loop(lo, hi, lambda r, c: (row_clear(r).start(), c)[1], 0)
                lax.fori_loop(lo, hi, lambda r, c: (row_clear(r).wait(), c)[1], 0)
            else:
                nb = (hi - lo) // ZERO_ROWS
                lax.fori_loop(0, nb, lambda t, c: (block_clear(lo + t * ZERO_ROWS).start(), c)[1], 0)
                lax.fori_loop(0, nb, lambda t, c: (block_clear(lo + t * ZERO_ROWS).wait(), c)[1], 0)

    for cp in copies:
        cp.wait()


def _dispatch(dest, pad_ranges, h2, n_rows):
    n, d = h2.shape
    tm = dest.shape[-1] // TOP_K
    n_ranges = pad_ranges.shape[1]
    return pl.pallas_call(
        functools.partial(_dispatch_kernel, tm=tm, n_ranges=n_ranges),
        grid=(n // tm,),
        in_specs=[pl.BlockSpec((None, 1, TOP_K * tm), lambda i: (i, 0, 0), memory_space=pltpu.SMEM),
                  pl.BlockSpec(memory_space=pltpu.SMEM),
                  pl.BlockSpec((tm, d), lambda i: (i, 0))],
        out_specs=pl.BlockSpec(memory_space=pl.ANY),
        out_shape=jax.ShapeDtypeStruct((n_rows, d), h2.dtype),
        scratch_shapes=[pltpu.VMEM((ZERO_ROWS, d), h2.dtype), pltpu.SemaphoreType.DMA(()),
                        pltpu.SemaphoreType.DMA(())],
        compiler_params=_params("arbitrary"),
        name="moe_dispatch",
    )(dest, pad_ranges, h2)


def _expert_kernel(te_ref, nu_ref, xs_ref, wg_ref, wu_ref, wd_ref, o_ref, hb_ref, acc_ref):
    del te_ref
    j = pl.program_id(0)
    f = pl.program_id(1)

    @pl.when(j < nu_ref[0])
    def _():
        @pl.when(f == 0)
        def _():
            hb_ref[...] = xs_ref[...].astype(BF16)
            acc_ref[...] = jnp.zeros_like(acc_ref)

        h = hb_ref[...]
        a = jnp.dot(h, wg_ref[...].astype(BF16), preferred_element_type=F32)
        u = jnp.dot(h, wu_ref[...].astype(BF16), preferred_element_type=F32)
        act = (a * _sigmoid(a) * u).astype(BF16)
        acc_ref[...] += jnp.dot(act, wd_ref[...].astype(BF16), preferred_element_type=F32)

        @pl.when(f == pl.num_programs(1) - 1)
        def _():
            o_ref[...] = acc_ref[...]

    @pl.when(jnp.logical_and(j >= nu_ref[0], f == pl.num_programs(1) - 1))
    def _():
        o_ref[...] = jnp.zeros_like(o_ref)


def _experts(tile_expert, n_used, xs, wg, wu, wd):
    n_rows, d = xs.shape
    f = wg.shape[-1]
    tf = _pick(f, (512, 256))
    nf = f // tf
    n_tiles = n_rows // EXPERT_TILE

    def row_map(j, ff, te, nu):
        return (jnp.minimum(j, nu[0] - 1), 0)

    def w_up_map(j, ff, te, nu):
        live = j < nu[0]
        return (te[jnp.minimum(j, nu[0] - 1)], 0, jnp.where(live, ff, nf - 1))

    def w_down_map(j, ff, te, nu):
        live = j < nu[0]
        return (te[jnp.minimum(j, nu[0] - 1)], jnp.where(live, ff, nf - 1), 0)

    grid_spec = pltpu.PrefetchScalarGridSpec(
        num_scalar_prefetch=2,
        grid=(n_tiles, nf),
        in_specs=[pl.BlockSpec((EXPERT_TILE, d), row_map),
                  pl.BlockSpec((None, d, tf), w_up_map),
                  pl.BlockSpec((None, d, tf), w_up_map),
                  pl.BlockSpec((None, tf, d), w_down_map)],
        out_specs=pl.BlockSpec((EXPERT_TILE, d), lambda j, ff, te, nu: (j, 0)),
        scratch_shapes=[pltpu.VMEM((EXPERT_TILE, d), BF16), pltpu.VMEM((EXPERT_TILE, d), F32)],
    )
    return pl.pallas_call(
        _expert_kernel,
        grid_spec=grid_spec,
        out_shape=jax.ShapeDtypeStruct((n_rows, d), F32),
        compiler_params=_params("arbitrary", "arbitrary"),
        name="moe_experts",
    )(tile_expert, n_used, xs, wg, wu, wd)


def _combine_kernel(dest_ref, next_ref, os_ref, route_ref, x_ref, g2_ref, o_ref, buf_ref, sem, *,
                    tm, n_experts):
    i = pl.program_id(0)
    slot = i % 2

    def gather(idx_ref, s, start):
        for r in range(tm):
            for k in range(TOP_K):
                cp = pltpu.make_async_copy(os_ref.at[pl.ds(idx_ref[0, k * tm + r], 1), :],
                                           buf_ref.at[s, k, pl.ds(r, 1), :], sem.at[s])
                if start:
                    cp.start(priority=k)
                else:
                    cp.wait()

    @pl.when(i == 0)
    def _():
        gather(dest_ref, slot, True)

    @pl.when(i + 1 < pl.num_programs(0))
    def _():
        gather(next_ref, 1 - slot, True)

    gather(dest_ref, slot, False)

    route = route_ref[...]
    lane = lax.broadcasted_iota(jnp.int32, (tm, LANES), 1)
    w1 = jnp.sum(jnp.where(lane == n_experts + 2, route, 0.0), axis=-1, keepdims=True)
    w2 = jnp.sum(jnp.where(lane == n_experts + 3, route, 0.0), axis=-1, keepdims=True)
    o_ref[...] = x_ref[...] + g2_ref[...] * (w1 * buf_ref[slot, 0] + w2 * buf_ref[slot, 1])


def _combine(dest, os, route, x1, g2, n_experts):
    b, t, d = x1.shape
    tm = dest.shape[-1] // TOP_K
    per_seq = t // tm
    n_steps = b * per_seq
    x_spec = pl.BlockSpec((None, tm, d), lambda i: (i // per_seq, i % per_seq, 0))
    if g2.shape[0] == 1:
        g_spec = pl.BlockSpec((None, 1, d), lambda i: (0, 0, 0))
    else:
        g_spec = pl.BlockSpec((None, 1, d), lambda i: (i // per_seq, 0, 0))
    return pl.pallas_call(
        functools.partial(_combine_kernel, tm=tm, n_experts=n_experts),
        grid=(n_steps,),
        in_specs=[pl.BlockSpec((None, 1, TOP_K * tm), lambda i: (i, 0, 0), memory_space=pltpu.SMEM),
                  pl.BlockSpec((None, 1, TOP_K * tm), lambda i: (jnp.minimum(i + 1, n_steps - 1), 0, 0),
                               memory_space=pltpu.SMEM),
                  pl.BlockSpec(memory_space=pl.ANY),
                  pl.BlockSpec((None, tm, LANES), lambda i: (i // per_seq, i % per_seq, 0)),
                  x_spec, g_spec],
        out_specs=x_spec,
        out_shape=jax.ShapeDtypeStruct((b, t, d), F32),
        scratch_shapes=[pltpu.VMEM((2, TOP_K, tm, d), F32), pltpu.SemaphoreType.DMA((2,))],
        compiler_params=_params("arbitrary"),
        name="moe_combine",
    )(dest, dest, os, route, x1, g2)


def _moe_routed(h2, route, x1, g2, wg, wu, wd, e0, n_experts):
    b, t, d = x1.shape
    n = b * t
    tm = _pick(t, (512, 256))
    rec = route.reshape(n, LANES)
    rank, counts = _rank(rec, n_experts)

    cnt = counts[0, :n_experts].astype(jnp.int32)
    padded = ((cnt + EXPERT_TILE - 1) // EXPERT_TILE) * EXPERT_TILE
    ends = jnp.cumsum(padded)
    offs = ends - padded
    n_tiles = (2 * n) // EXPERT_TILE + n_experts
    n_used = (ends[-1] // EXPERT_TILE).astype(jnp.int32).reshape(1)
    tile_start = jnp.arange(n_tiles, dtype=jnp.int32) * EXPERT_TILE
    tile_expert = jnp.minimum(jnp.sum(tile_start[:, None] >= ends[None, :], axis=1), n_experts - 1)
    slot = rank.astype(jnp.int32)
    d1 = jnp.take(offs, slot[2]) + slot[0]
    d2 = jnp.take(offs, slot[3]) + slot[1]
    dest = jnp.concatenate([d1.reshape(n // tm, tm), d2.reshape(n // tm, tm)], axis=1)[:, None, :]

    n_rows = n_tiles * EXPERT_TILE
    pad_ranges = jnp.stack([jnp.concatenate([offs + cnt, ends[-1:]]),
                            jnp.concatenate([ends, jnp.full((1,), n_rows, jnp.int32)])]).astype(jnp.int32)
    xs = _dispatch(dest, pad_ranges, h2.reshape(n, d), n_rows)
    os = _experts((e0 + tile_expert).astype(jnp.int32), n_used, xs, wg, wu, wd)
    return _combine(dest, os, route, x1, g2, n_experts)


def _rope_tables(n_tokens):
    rows = n_tokens // GRID_W
    row, col = jnp.meshgrid(jnp.arange(rows, dtype=F32), jnp.arange(GRID_W, dtype=F32), indexing="ij")
    half = HEAD_DIM // 2
    inv_freq = ROPE_THETA ** (-jnp.arange(0, half, 2, dtype=F32) / half)
    ang = jnp.concatenate([row.reshape(-1, 1) * inv_freq, col.reshape(-1, 1) * inv_freq], axis=-1)
    cos = jnp.repeat(jnp.cos(ang), 2, axis=-1)
    sign = jnp.tile(jnp.array([-1.0, 1.0], F32), HEAD_DIM // 2)
    sin = jnp.repeat(jnp.sin(ang), 2, axis=-1) * sign
    reps = LANES // HEAD_DIM
    return jnp.tile(cos, (1, reps)), jnp.tile(sin, (1, reps))


def kernel(x, c, ctx, c_ctx, ada_w, ada_b, norm1_g, norm2_g, attn_w_in, attn_w_out, qnorm_a, knorm_a, qnorm_b, knorm_b, sink_b, ffn_w_gate, ffn_w_up, ffn_w_down, conv_w_in, conv_w, conv_w_out, router_w, moe_w_gate, moe_w_up, moe_w_down):
    bsz, n_tok, d = x.shape
    n_ctx = ctx.shape[1]
    depth = ada_w.shape[0]
    n_experts = router_w.shape[-1]
    assert n_ctx == CHUNK and n_tok % CHUNK == 0 and bsz + 1 <= 8
    assert attn_w_in.shape[-1] == IN_PROJ_WIDTH and d % LANES == 0 and n_experts + 4 <= LANES

    cc = jnp.zeros((8, d), F32).at[:bsz].set(c).at[bsz].set(c_ctx)
    mod = _modulation(cc, ada_w, ada_b)

    def mod_vec(layer, j, is_ctx):
        m = mod[layer, :, j * d:(j + 1) * d]
        return m[bsz:bsz + 1].reshape(1, 1, d) if is_ctx else m[:bsz].reshape(bsz, 1, d)

    cos_l, sin_l = _rope_tables(n_tok)
    cos_c = jnp.ones((n_ctx, LANES), F32)
    sin_c = jnp.zeros((n_ctx, LANES), F32)
    head_id = jnp.arange(LANES) // HEAD_DIM
    bd = (head_id[:, None] == head_id[None, :]).astype(BF16)
    no_sink = jnp.zeros((N_HEADS,), F32)
    tile2 = lambda v: jnp.tile(v, LANES // HEAD_DIM)

    xc = ctx
    for layer in range(depth):
        i = layer // 2
        ctx_needed = any(j % 2 == 0 for j in range(layer + 1, depth))
        n1 = norm1_g[layer].reshape(1, d)
        n2 = norm2_g[layer].reshape(1, d)
        mv = lambda j, is_ctx: mod_vec(layer, j, is_ctx)

        if layer % 2 == 0:
            w_in = attn_w_in[i].astype(BF16)
            w_out = attn_w_out[i].astype(BF16)
            qscale = ATTN_SCALE * LOG2E
            gains = jnp.stack([tile2(qnorm_a[i]) * qscale, tile2(knorm_a[i]),
                               tile2(qnorm_b[i]) * qscale, tile2(knorm_b[i])]).astype(F32)
            sink = sink_b[i].astype(F32)
            def logit_bound(gq, gk):
                raw = HEAD_DIM * jnp.max(jnp.abs(gq)) * jnp.max(jnp.abs(gk)) * qscale * 1.02
                return raw.astype(BF16).astype(F32)
            bound_a = logit_bound(qnorm_a[i], knorm_a[i])
            bound_b = jnp.maximum(logit_bound(qnorm_b[i], knorm_b[i]),
                                  (jnp.max(sink) * LOG2E * 1.02).astype(BF16).astype(F32))
            fast_a = bound_a <= FAST_MAX_LOGIT
            fast_b = jnp.logical_and(bound_b <= FAST_MAX_LOGIT,
                                     jnp.max(jnp.abs(sink)) * LOG2E <= FAST_MAX_LOGIT)
            bound_a = bound_a.reshape(1)
            bound_b = bound_b.reshape(1)
            qa_l, ka_l, va_l, qb_l, kb_l, vb_l = _attn_in(
                x, n1, mv(1, False), mv(0, False), w_in, gains, cos_l, sin_l, bd, rope=True)
            qa_c, ka_c, va_c, qb_c, kb_c, vb_c = _attn_in(
                xc, n1, mv(1, True), mv(0, True), w_in, gains, cos_c, sin_c, bd, rope=False)
            ka_all = jnp.concatenate([ka_c, ka_l], axis=1)
            va_all = jnp.concatenate([va_c, va_l], axis=1)
            kb_all = jnp.concatenate([kb_c, kb_l], axis=1)
            vb_all = jnp.concatenate([vb_c, vb_l], axis=1)
            oa = _attend(fast_a, qa_l, ka_all, va_all, no_sink, bound_a, mode="global", use_sink=False)
            ob = _attend(fast_b, qb_l, kb_all, vb_all, sink, bound_b, mode="window", use_sink=True)
            x, h2 = _attn_out(oa, ob, w_out, x, mv(2, False), n2, mv(4, False), mv(3, False))
            if ctx_needed:
                oa_c = _attend(fast_a, qa_c, ka_c, va_c, no_sink, bound_a, mode="ctx", use_sink=False)
                ob_c = _attend(fast_b, qb_c, kb_c, vb_c, sink, bound_b, mode="ctx", use_sink=True)
                xc, h2c = _attn_out(oa_c, ob_c, w_out, xc, mv(2, True), n2, mv(4, True), mv(3, True))
            wg = ffn_w_gate.astype(BF16)
            wu = ffn_w_up.astype(BF16)
            wd = ffn_w_down.astype(BF16)
            x = _ffn(h2, wg, wu, wd, None, x, mv(5, False), e0=i)
            if ctx_needed:
                xc = _ffn(h2c, wg, wu, wd, None, xc, mv(5, True), e0=i)
        else:
            w_in = conv_w_in[i].astype(BF16)
            w_out = conv_w_out[i].astype(BF16)
            cw = conv_w[i].astype(F32)
            router_f32 = jnp.zeros((d, LANES), F32).at[:, :n_experts].set(router_w[i])
            router_hi = router_f32.astype(BF16)
            router_pad = jnp.stack([router_hi, (router_f32 - router_hi.astype(F32)).astype(BF16)])
            wg = moe_w_gate.reshape((-1,) + moe_w_gate.shape[2:])
            wu = moe_w_up.reshape((-1,) + moe_w_up.shape[2:])
            wd = moe_w_down.reshape((-1,) + moe_w_down.shape[2:])
            e0 = i * n_experts
            bg, z = _conv_in(x, n1, mv(1, False), mv(0, False), w_in)
            x, h2, route = _conv_out(z, bg, cw, w_out, x, mv(2, False), n2, mv(4, False), mv(3, False),
                                     router_pad, n_experts, F32)
            x = _moe_routed(h2, route, x, mv(5, False), wg, wu, wd, e0, n_experts)
            if ctx_needed:
                bg, z = _conv_in(xc, n1, mv(1, True), mv(0, True), w_in)
                xc, h2c, gates_c = _conv_out(z, bg, cw, w_out, xc, mv(2, True), n2, mv(4, True),
                                             mv(3, True), router_pad, n_experts, BF16)
                flat = lambda a: a.reshape((1, bsz * n_ctx) + a.shape[2:])
                xc = _ffn(flat(h2c), wg, wu, wd, flat(gates_c), flat(xc), mv(5, True),
                          e0=e0, n_e=n_experts).reshape(bsz, n_ctx, d)
    return x
```

```python
import functools

import jax
import jax.numpy as jnp
from jax import lax
from jax.experimental import pallas as pl
from jax.experimental.pallas import tpu as pltpu

F32 = jnp.float32
BF16 = jnp.bfloat16

HEAD_DIM = 64
N_KV = 2
GROUP = 4
N_HEADS = N_KV * GROUP
Q_WIDTH = N_HEADS * HEAD_DIM
KV_WIDTH = N_KV * HEAD_DIM
IN_PROJ_WIDTH = 2 * (Q_WIDTH + 2 * KV_WIDTH)
GRID_W = 64
WINDOW = 128
ROPE_THETA = 10000.0
ATTN_SCALE = HEAD_DIM ** -0.5
EPS = 1e-6
N_ADA = 6
LOG2E = 1.4426950408889634
LANES = 128
CHUNK = 256
K_AUG = 2 * KV_WIDTH
V_ROWS = HEAD_DIM + 16
GLOBAL_STEP = 8
FAST_MAX_LOGIT = 40.0
NEG_BIG = -1e30
VMEM_LIMIT_BYTES = 56 * 1024 * 1024


def _params(*sem):
    return pltpu.CompilerParams(dimension_semantics=sem, vmem_limit_bytes=VMEM_LIMIT_BYTES)


def _pick(n, candidates):
    for t in candidates:
        if n % t == 0:
            return t
    return n


def _sigmoid(a):
    return 1.0 / (1.0 + jnp.exp(-a))


def _norm_mod(x, g, sc, sh):
    ms = jnp.mean(x * x, axis=-1, keepdims=True)
    return x * lax.rsqrt(ms + EPS) * g * (1.0 + sc) + sh


def _mod_kernel(c_ref, w_ref, b_ref, o_ref):
    c = c_ref[...]
    s = c * _sigmoid(c)
    o_ref[...] = jnp.dot(s, w_ref[...], preferred_element_type=F32,
                         precision=lax.Precision.HIGHEST) + b_ref[...]


def _modulation(cc, ada_w, ada_b):
    depth, d, n = ada_w.shape
    tn = _pick(n, (1536, 1024, 512))
    rows = cc.shape[0]
    return pl.pallas_call(
        _mod_kernel,
        grid=(depth, n // tn),
        in_specs=[
            pl.BlockSpec((rows, d), lambda l, j: (0, 0)),
            pl.BlockSpec((None, d, tn), lambda l, j: (l, 0, j)),
            pl.BlockSpec((None, 1, tn), lambda l, j: (l, 0, j)),
        ],
        out_specs=pl.BlockSpec((None, rows, tn), lambda l, j: (l, 0, j)),
        out_shape=jax.ShapeDtypeStruct((depth, rows, n), F32),
        compiler_params=_params("arbitrary", "arbitrary"),
        name="modulation",
    )(cc, ada_w, ada_b.reshape(depth, 1, n))


def _vec_spec(arr):
    d = arr.shape[-1]
    if arr.shape[0] == 1:
        return pl.BlockSpec((None, 1, d), lambda b, i, *_: (0, 0, 0))
    return pl.BlockSpec((None, 1, d), lambda b, i, *_: (b, 0, 0))


def _attn_in_kernel(x_ref, g_ref, sc_ref, sh_ref, w_ref, gains_ref, cos_ref, sin_ref, bd_ref,
                    qa_ref, ka_ref, va_ref, qb_ref, kb_ref, vb_ref, *, rope, tm):
    lane = lax.broadcasted_iota(jnp.int32, (CHUNK, LANES), 1)
    even = (lane & 1) == 0
    bd = bd_ref[...]
    ones_rows = jnp.where(lax.broadcasted_iota(jnp.int32, (V_ROWS - HEAD_DIM, CHUNK), 0) == 0,
                          1.0, 0.0).astype(BF16)
    groups = ((0, qa_ref, ka_ref, va_ref, 0), (Q_WIDTH + 2 * KV_WIDTH, qb_ref, kb_ref, vb_ref, 2))

    for j in range(tm // CHUNK):
        rows = slice(j * CHUNK, (j + 1) * CHUNK)
        h = _norm_mod(x_ref[rows, :], g_ref[...], sc_ref[...], sh_ref[...]).astype(BF16)
        y = jnp.dot(h, w_ref[...], preferred_element_type=F32)

        def head_norm(yc, gain, rows=rows):
            ss = jnp.dot((yc * yc).astype(BF16), bd, preferred_element_type=F32)
            t = yc * lax.rsqrt(ss * (1.0 / HEAD_DIM) + EPS) * gain
            if rope:
                partner = jnp.where(even, pltpu.roll(t, LANES - 1, 1), pltpu.roll(t, 1, 1))
                t = t * cos_ref[rows, :] + partner * sin_ref[rows, :]
            return t

        for col0, q_ref, k_ref, v_ref, grow in groups:
            gq = gains_ref[grow:grow + 1, :]
            gk = gains_ref[grow + 1:grow + 2, :]
            for c in range(Q_WIDTH // LANES):
                t = head_norm(y[:, col0 + c * LANES:col0 + (c + 1) * LANES], gq)
                q_ref[c * LANES:(c + 1) * LANES, rows] = t.T.astype(BF16)
            kcol = col0 + Q_WIDTH
            k_ref[rows, 0:KV_WIDTH] = head_norm(y[:, kcol:kcol + KV_WIDTH], gk).astype(BF16)
            k_ref[rows, KV_WIDTH:K_AUG] = jnp.where(lane == 0, 1.0, 0.0).astype(BF16)
            vt = y[:, kcol + KV_WIDTH:kcol + 2 * KV_WIDTH].T.astype(BF16)
            for kv in range(N_KV):
                r0 = kv * V_ROWS
                v_ref[j, r0:r0 + HEAD_DIM, :] = vt[kv * HEAD_DIM:(kv + 1) * HEAD_DIM, :]
                v_ref[j, r0 + HEAD_DIM:r0 + V_ROWS, :] = ones_rows


def _attn_in(x, g, sc, sh, w_bf16, gains, cos_t, sin_t, bd, *, rope):
    b, t, d = x.shape
    tm = _pick(t, (512, 256))
    nch = t // CHUNK
    kern = functools.partial(_attn_in_kernel, rope=rope, tm=tm)
    q_spec = pl.BlockSpec((None, Q_WIDTH, tm), lambda bb, i: (bb, 0, i))
    k_spec = pl.BlockSpec((None, tm, K_AUG), lambda bb, i: (bb, i, 0))
    v_spec = pl.BlockSpec((None, tm // CHUNK, N_KV * V_ROWS, CHUNK), lambda bb, i: (bb, i, 0, 0))
    q_shape = jax.ShapeDtypeStruct((b, Q_WIDTH, t), BF16)
    k_shape = jax.ShapeDtypeStruct((b, t, K_AUG), BF16)
    v_shape = jax.ShapeDtypeStruct((b, nch, N_KV * V_ROWS, CHUNK), BF16)
    return pl.pallas_call(
        kern,
        grid=(b, t // tm),
        in_specs=[
            pl.BlockSpec((None, tm, d), lambda bb, i: (bb, i, 0)),
            pl.BlockSpec((1, d), lambda bb, i: (0, 0)),
            _vec_spec(sc), _vec_spec(sh),
            pl.BlockSpec((d, IN_PROJ_WIDTH), lambda bb, i: (0, 0)),
            pl.BlockSpec((4, LANES), lambda bb, i: (0, 0)),
            pl.BlockSpec((tm, LANES), lambda bb, i: (i, 0)),
            pl.BlockSpec((tm, LANES), lambda bb, i: (i, 0)),
            pl.BlockSpec((LANES, LANES), lambda bb, i: (0, 0)),
        ],
        out_specs=[q_spec, k_spec, v_spec, q_spec, k_spec, v_spec],
        out_shape=[q_shape, k_shape, v_shape, q_shape, k_shape, v_shape],
        compiler_params=_params("parallel", "parallel"),
        name="attn_in",
    )(x, g, sc, sh, w_bf16, gains, cos_t, sin_t, bd)


def _attn_kernel(q_ref, k_ref, v_ref, sink_ref, o_ref, qpad_ref, s_ref, cmax_ref, m_ref, l_ref,
                 acc_ref, *, mode, use_sink, n_lat, tq):
    i = pl.program_id(1)
    w = GROUP * tq
    zeros = jnp.zeros((HEAD_DIM, w), BF16)
    for kv in range(N_KV):
        heads = range(kv * GROUP, (kv + 1) * GROUP)
        q4 = jnp.concatenate([q_ref[h * HEAD_DIM:(h + 1) * HEAD_DIM, :] for h in heads], axis=1)
        qpad_ref[kv] = jnp.concatenate([q4, zeros] if kv == 0 else [zeros, q4], axis=0)
        if use_sink:
            m_ref[kv] = jnp.concatenate(
                [jnp.full((1, tq), sink_ref[h] * LOG2E, F32) for h in heads], axis=1)
            l_ref[kv] = jnp.ones((1, w), F32)
        else:
            m_ref[kv] = jnp.full((1, w), NEG_BIG, F32)
            l_ref[kv] = jnp.zeros((1, w), F32)
        acc_ref[kv] = jnp.zeros((HEAD_DIM, w), F32)

    def scores(c, slot, mask):
        start = c * CHUNK if isinstance(c, int) else pl.multiple_of(c * CHUNK, CHUNK)
        kc = k_ref[pl.ds(start, CHUNK), 0:KV_WIDTH]
        for kv in range(N_KV):
            s = jnp.dot(kc, qpad_ref[kv], preferred_element_type=F32)
            if mask is not None:
                s = jnp.where(mask, s, -jnp.inf)
            s_ref[slot, kv] = s
            cmax_ref[slot, kv] = jnp.max(s, axis=0, keepdims=True)

    def absorb(c, slot):
        for kv in range(N_KV):
            m = m_ref[kv]
            m_new = jnp.maximum(m, cmax_ref[slot, kv])
            alpha = jnp.exp2(m - m_new)
            p = jnp.exp2(s_ref[slot, kv] - m_new)
            l_ref[kv] = alpha * l_ref[kv] + jnp.sum(p, axis=0, keepdims=True)
            m_ref[kv] = m_new
            vc = v_ref[c, kv * V_ROWS:kv * V_ROWS + HEAD_DIM, :]
            acc_ref[kv] = alpha * acc_ref[kv] + jnp.dot(vc, p.astype(BF16),
                                                        preferred_element_type=F32)

    scores(0, 0, None)
    if mode == "global":
        def body(j, carry):
            c = 2 * j
            scores(c + 1, 1, None)
            absorb(c, 0)
            scores(c + 2, 0, None)
            absorb(c + 1, 1)
            return carry
        lax.fori_loop(0, n_lat // 2, body, 0)
        c_end = 2 * (n_lat // 2)
        if n_lat % 2:
            scores(c_end + 1, 1, None)
            absorb(c_end, 0)
            absorb(c_end + 1, 1)
        else:
            absorb(c_end, 0)
    elif mode == "window":
        row = lax.broadcasted_iota(jnp.int32, (CHUNK, w), 0)
        col = lax.broadcasted_iota(jnp.int32, (CHUNK, w), 1) & (tq - 1)
        rel0 = col - row
        prev_c, prev_slot = 0, 0
        for d in (-1, 0, 1):
            cl = i + d
            ok = jnp.logical_and(cl >= 0, cl < n_lat)
            c = 1 + jnp.clip(cl, 0, n_lat - 1)
            mask = jnp.logical_and(jnp.abs(rel0 - d * CHUNK) <= WINDOW, ok)
            scores(c, 1 - prev_slot, mask)
            absorb(prev_c, prev_slot)
            prev_c, prev_slot = c, 1 - prev_slot
        absorb(prev_c, prev_slot)
    else:
        absorb(0, 0)

    for kv in range(N_KV):
        out = acc_ref[kv] * (1.0 / l_ref[kv])
        for g in range(GROUP):
            h = kv * GROUP + g
            o_ref[h * HEAD_DIM:(h + 1) * HEAD_DIM, :] = out[:, g * tq:(g + 1) * tq].astype(BF16)


def _attention(q_t, k_all, v_all, sink, bound, *, mode, use_sink):
    del bound
    b, _, tq_total = q_t.shape
    nk = k_all.shape[1]
    nc = v_all.shape[1]
    tq = CHUNK
    kern = functools.partial(_attn_kernel, mode=mode, use_sink=use_sink, n_lat=nc - 1, tq=tq)
    return pl.pallas_call(
        kern,
        grid=(b, tq_total // tq),
        in_specs=[
            pl.BlockSpec((None, Q_WIDTH, tq), lambda bb, i: (bb, 0, i)),
            pl.BlockSpec((None, nk, K_AUG), lambda bb, i: (bb, 0, 0)),
            pl.BlockSpec((None, nc, N_KV * V_ROWS, CHUNK), lambda bb, i: (bb, 0, 0, 0)),
            pl.BlockSpec(memory_space=pltpu.SMEM),
        ],
        out_specs=pl.BlockSpec((None, Q_WIDTH, tq), lambda bb, i: (bb, 0, i)),
        out_shape=jax.ShapeDtypeStruct((b, Q_WIDTH, tq_total), BF16),
        scratch_shapes=[pltpu.VMEM((N_KV, KV_WIDTH, GROUP * tq), BF16),
                        pltpu.VMEM((2, N_KV, CHUNK, GROUP * tq), F32),
                        pltpu.VMEM((2, N_KV, 1, GROUP * tq), F32),
                        pltpu.VMEM((N_KV, 1, GROUP * tq), F32),
                        pltpu.VMEM((N_KV, 1, GROUP * tq), F32),
                        pltpu.VMEM((N_KV, HEAD_DIM, GROUP * tq), F32)],
        compiler_params=_params("parallel", "parallel"),
        name="attn_" + mode,
    )(q_t, k_all, v_all, sink)


def _attn_fast_kernel(q_ref, k_ref, v_ref, sink_ref, bound_ref, o_ref, qa_ref, acc_ref, p_ref, *,
                      mode, use_sink, n_lat, tq):
    i = pl.program_id(1)
    w = GROUP * tq
    bound = bound_ref[0]
    zeros = jnp.zeros((HEAD_DIM, w), BF16)
    row = lax.broadcasted_iota(jnp.int32, (KV_WIDTH, w), 0)
    shift = jnp.where(row == 0, -bound, 0.0).astype(BF16)
    for kv in range(N_KV):
        heads = range(kv * GROUP, (kv + 1) * GROUP)
        q4 = jnp.concatenate([q_ref[h * HEAD_DIM:(h + 1) * HEAD_DIM, :] for h in heads], axis=1)
        qa_ref[kv] = jnp.concatenate(([q4, zeros] if kv == 0 else [zeros, q4]) + [shift], axis=0)
        acc_ref[kv] = jnp.zeros((V_ROWS, w), F32)

    def update(c, mask, n=1):
        start = c * CHUNK if isinstance(c, int) else pl.multiple_of(c * CHUNK, CHUNK)
        kc = k_ref[pl.ds(start, n * CHUNK), :]
        for kv in range(N_KV):
            s = jnp.dot(kc, qa_ref[kv], preferred_element_type=F32)
            if mask is not None:
                s = jnp.where(mask, s, -jnp.inf)
            p = jnp.exp2(s).astype(BF16)
            vc = jnp.concatenate([v_ref[c + j, kv * V_ROWS:(kv + 1) * V_ROWS, :] for j in range(n)],
                                 axis=1)
            acc_ref[kv] += jnp.dot(vc, p, preferred_element_type=F32)

    def probs(c, n, slot):
        start = c * CHUNK if isinstance(c, int) else pl.multiple_of(c * CHUNK, CHUNK)
        kc = k_ref[pl.ds(start, n * CHUNK), :]
        for kv in range(N_KV):
            s = jnp.dot(kc, qa_ref[kv], preferred_element_type=F32)
            p_ref[slot, kv, 0:n * CHUNK, :] = jnp.exp2(s).astype(BF16)

    def values(c, n, slot):
        for kv in range(N_KV):
            vc = jnp.concatenate([v_ref[c + j, kv * V_ROWS:(kv + 1) * V_ROWS, :] for j in range(n)],
                                 axis=1)
            acc_ref[kv] += jnp.dot(vc, p_ref[slot, kv, 0:n * CHUNK, :], preferred_element_type=F32)

    if mode == "global":
        n_full = (1 + n_lat) // GLOBAL_STEP
        steps = [(GLOBAL_STEP * s, GLOBAL_STEP) for s in range(n_full)]
        steps += [(c, 1) for c in range(GLOBAL_STEP * n_full, 1 + n_lat)]
        n_pairs = max((n_full - 1) // 2, 0)

        probs(*steps[0], 0)

        def body(t, carry):
            c = 2 * GLOBAL_STEP * t
            probs(c + GLOBAL_STEP, GLOBAL_STEP, 1)
            values(c, GLOBAL_STEP, 0)
            probs(c + 2 * GLOBAL_STEP, GLOBAL_STEP, 0)
            values(c + GLOBAL_STEP, GLOBAL_STEP, 1)
            return carry
        lax.fori_loop(0, n_pairs, body, 0)
        pending, slot = steps[2 * n_pairs], 0
        for nxt in steps[2 * n_pairs + 1:]:
            probs(*nxt, 1 - slot)
            values(*pending, slot)
            pending, slot = nxt, 1 - slot
        values(*pending, slot)
    elif mode == "window":
        half = CHUNK - WINDOW
        c_next = jnp.minimum(i + 2, n_lat)
        starts = (0, pl.multiple_of(i * CHUNK + half, half), pl.multiple_of((i + 1) * CHUNK, CHUNK),
                  pl.multiple_of(c_next * CHUNK, CHUNK))
        sizes = (CHUNK, WINDOW, CHUNK, WINDOW)
        kc = jnp.concatenate([k_ref[pl.ds(st, sz), :] for st, sz in zip(starts, sizes)], axis=0)
        n_keys = sum(sizes)
        rowk = lax.broadcasted_iota(jnp.int32, (n_keys, w), 0) - CHUNK
        col = lax.broadcasted_iota(jnp.int32, (n_keys, w), 1) & (tq - 1)
        in_band = jnp.abs(col - rowk + WINDOW) <= WINDOW
        first = jnp.where(i > 0, 0, WINDOW)
        last = jnp.where(i + 1 < n_lat, 2 * WINDOW + CHUNK, WINDOW + CHUNK)
        exists = jnp.logical_and(rowk >= first, rowk < last)
        mask = jnp.logical_or(rowk < 0, jnp.logical_and(in_band, exists))
        for kv in range(N_KV):
            rows = slice(kv * V_ROWS, (kv + 1) * V_ROWS)
            s = jnp.where(mask, jnp.dot(kc, qa_ref[kv], preferred_element_type=F32), -jnp.inf)
            p = jnp.exp2(s).astype(BF16)
            vc = jnp.concatenate([v_ref[0, rows, :], v_ref[i, rows, half:CHUNK], v_ref[i + 1, rows, :],
                                  v_ref[c_next, rows, 0:WINDOW]], axis=1)
            acc_ref[kv] += jnp.dot(vc, p, preferred_element_type=F32)
    else:
        update(0, None)

    for kv in range(N_KV):
        acc = acc_ref[kv]
        l = acc[HEAD_DIM:HEAD_DIM + 1, :]
        if use_sink:
            l = l + jnp.concatenate(
                [jnp.full((1, tq), jnp.exp2(sink_ref[kv * GROUP + g] * LOG2E - bound), F32)
                 for g in range(GROUP)], axis=1)
        out = acc[0:HEAD_DIM, :] * (1.0 / l)
        for g in range(GROUP):
            h = kv * GROUP + g
            o_ref[h * HEAD_DIM:(h + 1) * HEAD_DIM, :] = out[:, g * tq:(g + 1) * tq].astype(BF16)


def _attention_fast(q_t, k_all, v_all, sink, bound, *, mode, use_sink):
    b, _, tq_total = q_t.shape
    nk = k_all.shape[1]
    nc = v_all.shape[1]
    tq = CHUNK
    kern = functools.partial(_attn_fast_kernel, mode=mode, use_sink=use_sink, n_lat=nc - 1, tq=tq)
    return pl.pallas_call(
        kern,
        grid=(b, tq_total // tq),
        in_specs=[
            pl.BlockSpec((None, Q_WIDTH, tq), lambda bb, i: (bb, 0, i)),
            pl.BlockSpec((None, nk, K_AUG), lambda bb, i: (bb, 0, 0)),
            pl.BlockSpec((None, nc, N_KV * V_ROWS, CHUNK), lambda bb, i: (bb, 0, 0, 0)),
            pl.BlockSpec(memory_space=pltpu.SMEM),
            pl.BlockSpec(memory_space=pltpu.SMEM),
        ],
        out_specs=pl.BlockSpec((None, Q_WIDTH, tq), lambda bb, i: (bb, 0, i)),
        out_shape=jax.ShapeDtypeStruct((b, Q_WIDTH, tq_total), BF16),
        scratch_shapes=[pltpu.VMEM((N_KV, K_AUG, GROUP * tq), BF16),
                        pltpu.VMEM((N_KV, V_ROWS, GROUP * tq), F32),
                        pltpu.VMEM((2, N_KV, (GLOBAL_STEP if mode == "global" else 1) * CHUNK,
                                    GROUP * tq), BF16)],
        compiler_params=_params("parallel", "parallel"),
        name="attn_fast_" + mode,
    )(q_t, k_all, v_all, sink, bound)


def _attend(fast_ok, *args, **kw):
    return lax.cond(fast_ok, functools.partial(_attention_fast, **kw),
                    functools.partial(_attention, **kw), *args)


_TN_DIMS = (((0,), (0,)), ((), ()))


def _attn_out_kernel(oa_ref, ob_ref, w_ref, x_ref, g1_ref, n2_ref, sc2_ref, sh2_ref, x1_ref, h2_ref):
    tm = x_ref.shape[0]
    sub = min(tm, CHUNK)
    for j in range(tm // sub):
        rows = slice(j * sub, (j + 1) * sub)
        mix = lax.dot_general(oa_ref[:, rows], w_ref[0:Q_WIDTH, :], _TN_DIMS, preferred_element_type=F32)
        mix = mix + lax.dot_general(ob_ref[:, rows], w_ref[Q_WIDTH:2 * Q_WIDTH, :], _TN_DIMS,
                                    preferred_element_type=F32)
        x1 = x_ref[rows, :] + g1_ref[...] * mix
        x1_ref[rows, :] = x1
        h2_ref[rows, :] = _norm_mod(x1, n2_ref[...], sc2_ref[...], sh2_ref[...]).astype(BF16)


def _attn_out(oa, ob, w_bf16, x, g1, n2, sc2, sh2):
    b, t, d = x.shape
    tm = _pick(t, (512, 256))
    o_spec = pl.BlockSpec((None, Q_WIDTH, tm), lambda bb, i: (bb, 0, i))
    x_spec = pl.BlockSpec((None, tm, d), lambda bb, i: (bb, i, 0))
    return pl.pallas_call(
        _attn_out_kernel,
        grid=(b, t // tm),
        in_specs=[o_spec, o_spec,
                  pl.BlockSpec((2 * Q_WIDTH, d), lambda bb, i: (0, 0)),
                  x_spec, _vec_spec(g1),
                  pl.BlockSpec((1, d), lambda bb, i: (0, 0)),
                  _vec_spec(sc2), _vec_spec(sh2)],
        out_specs=[x_spec, x_spec],
        out_shape=[jax.ShapeDtypeStruct((b, t, d), F32), jax.ShapeDtypeStruct((b, t, d), BF16)],
        compiler_params=_params("parallel", "parallel"),
        name="attn_out",
    )(oa, ob, w_bf16, x, g1, n2, sc2, sh2)


def _ffn_kernel(*refs, gated, tm):
    if gated:
        h_ref, wg_ref, wu_ref, wd_ref, gate_ref, x_ref, g2_ref, o_ref, acc_ref = refs
    else:
        h_ref, wg_ref, wu_ref, wd_ref, x_ref, g2_ref, o_ref, acc_ref = refs
    e = pl.program_id(2)
    f = pl.program_id(3)
    first = jnp.logical_and(e == 0, f == 0)
    last = jnp.logical_and(e == pl.num_programs(2) - 1, f == pl.num_programs(3) - 1)

    @pl.when(first)
    def _():
        acc_ref[...] = jnp.zeros_like(acc_ref)

    h = h_ref[...]
    a = jnp.dot(h, wg_ref[...].astype(BF16), preferred_element_type=F32)
    u = jnp.dot(h, wu_ref[...].astype(BF16), preferred_element_type=F32)
    act = (a * _sigmoid(a) * u).astype(BF16)
    y = jnp.dot(act, wd_ref[...].astype(BF16), preferred_element_type=F32)
    if gated:
        lane = lax.broadcasted_iota(jnp.int32, (tm, LANES), 1)
        ge = jnp.sum(jnp.where(lane == e, gate_ref[...], 0.0), axis=-1, keepdims=True)
        y = y * ge
    acc_ref[...] += y

    @pl.when(last)
    def _():
        o_ref[...] = x_ref[...] + g2_ref[...] * acc_ref[...]


def _ffn(h2, wg, wu, wd, gates, x1, g2, e0=0, n_e=1):
    b, t, d = x1.shape
    f = wg.shape[-1]
    gated = gates is not None
    tm = _pick(t, (1024, 512, 256) if gated else (512, 256))
    resident = n_e == 1 and 3 * d * f * 2 <= VMEM_LIMIT_BYTES // 3
    tf = f if resident else _pick(f, (512, 256))
    mode = dict(pipeline_mode=pl.Buffered(1)) if resident else {}
    x_spec = pl.BlockSpec((None, tm, d), lambda bb, i, e, j: (bb, i, 0))
    in_specs = [x_spec,
                pl.BlockSpec((None, d, tf), lambda bb, i, e, j: (e0 + e, 0, j), **mode),
                pl.BlockSpec((None, d, tf), lambda bb, i, e, j: (e0 + e, 0, j), **mode),
                pl.BlockSpec((None, tf, d), lambda bb, i, e, j: (e0 + e, j, 0), **mode)]
    args = [h2, wg, wu, wd]
    if gated:
        in_specs.append(pl.BlockSpec((None, tm, LANES), lambda bb, i, e, j: (bb, i, 0)))
        args.append(gates)
    in_specs += [x_spec, _vec_spec(g2)]
    args += [x1, g2]
    return pl.pallas_call(
        functools.partial(_ffn_kernel, gated=gated, tm=tm),
        grid=(b, t // tm, n_e, f // tf),
        in_specs=in_specs,
        out_specs=x_spec,
        out_shape=jax.ShapeDtypeStruct((b, t, d), F32),
        scratch_shapes=[pltpu.VMEM((tm, d), F32)],
        compiler_params=_params("parallel", "parallel", "arbitrary", "arbitrary"),
        name="ffn_gated" if gated else "ffn_dense",
    )(*args)


def _conv_in_kernel(x_ref, g_ref, sc_ref, sh_ref, w_ref, bg_ref, z_ref, *, d):
    h = _norm_mod(x_ref[...], g_ref[...], sc_ref[...], sh_ref[...]).astype(BF16)
    y = jnp.dot(h, w_ref[...], preferred_element_type=F32)
    bg_ref[...] = y[:, 0:d].astype(BF16)
    z_ref[...] = (y[:, d:2 * d] * y[:, 2 * d:3 * d]).astype(BF16)


def _conv_in(x, g, sc, sh, w_bf16):
    b, t, d = x.shape
    tm = _pick(t, (512, 256))
    x_spec = pl.BlockSpec((None, tm, d), lambda bb, i: (bb, i, 0))
    return pl.pallas_call(
        functools.partial(_conv_in_kernel, d=d),
        grid=(b, t // tm),
        in_specs=[x_spec, pl.BlockSpec((1, d), lambda bb, i: (0, 0)), _vec_spec(sc), _vec_spec(sh),
                  pl.BlockSpec((d, 3 * d), lambda bb, i: (0, 0))],
        out_specs=[x_spec, x_spec],
        out_shape=[jax.ShapeDtypeStruct((b, t, d), BF16), jax.ShapeDtypeStruct((b, t, d), BF16)],
        compiler_params=_params("parallel", "parallel"),
        name="conv_in",
    )(x, g, sc, sh, w_bf16)


HALO = 16


def _conv_out_kernel(z_ref, zp_ref, zn_ref, bg_ref, cw_ref, w_ref, x_ref, g1_ref, n2_ref, sc2_ref,
                     sh2_ref, rw_ref, x1_ref, h2_ref, gate_ref, *, tm, n_experts):
    i = pl.program_id(1)
    z = z_ref[...].astype(F32)
    prev = jnp.where(i > 0, zp_ref[HALO - 1:HALO, :].astype(F32), 0.0)
    nxt = jnp.where(i < pl.num_programs(1) - 1, zn_ref[0:1, :].astype(F32), 0.0)
    row = lax.broadcasted_iota(jnp.int32, z.shape, 0)
    z_dn = jnp.where(row == 0, prev, pltpu.roll(z, 1, 0))
    z_up = jnp.where(row == tm - 1, nxt, pltpu.roll(z, tm - 1, 0))
    conv = z_dn * cw_ref[0:1, :] + z * cw_ref[1:2, :] + z_up * cw_ref[2:3, :]
    v = (bg_ref[...].astype(F32) * conv).astype(BF16)
    sub = min(tm, CHUNK)
    lane = lax.broadcasted_iota(jnp.int32, (sub, LANES), 1)
    lanef = lane.astype(F32)
    for j in range(tm // sub):
        rows = slice(j * sub, (j + 1) * sub)
        mix = jnp.dot(v[rows, :], w_ref[...], preferred_element_type=F32)
        x1 = x_ref[rows, :] + g1_ref[...] * mix
        x1_ref[rows, :] = x1
        h2 = _norm_mod(x1, n2_ref[...], sc2_ref[...], sh2_ref[...])
        h2_ref[rows, :] = h2.astype(h2_ref.dtype)

        h_hi = h2.astype(BF16)
        h_lo = (h2 - h_hi.astype(F32)).astype(BF16)
        logits = (jnp.dot(h_hi, rw_ref[0], preferred_element_type=F32)
                  + jnp.dot(h_lo, rw_ref[0], preferred_element_type=F32)
                  + jnp.dot(h_hi, rw_ref[1], preferred_element_type=F32))
        logits = jnp.where(lane < n_experts, logits, -jnp.inf)
        m1 = jnp.max(logits, axis=-1, keepdims=True)
        i1 = jnp.min(jnp.where(logits == m1, lanef, float(LANES)), axis=-1, keepdims=True)
        rest = jnp.where(lanef == i1, -jnp.inf, logits)
        m2 = jnp.max(rest, axis=-1, keepdims=True)
        i2 = jnp.min(jnp.where(rest == m2, lanef, float(LANES)), axis=-1, keepdims=True)
        e2 = jnp.exp(m2 - m1)
        w1 = 1.0 / (1.0 + e2)
        w2 = e2 * w1
        rec = jnp.where(lanef == i1, w1, 0.0) + jnp.where(lanef == i2, w2, 0.0)
        for k, val in enumerate((i1, i2, w1, w2)):
            rec = jnp.where(lane == n_experts + k, val, rec)
        gate_ref[rows, :] = rec


def _conv_out(z, bg, conv_w, w_bf16, x, g1, n2, sc2, sh2, router_pad, n_experts, h2_dtype):
    b, t, d = x.shape
    tm = _pick(t, (512, 256))
    hb = tm // HALO
    n_halo = t // HALO
    x_spec = pl.BlockSpec((None, tm, d), lambda bb, i: (bb, i, 0))
    prev_spec = pl.BlockSpec((None, HALO, d), lambda bb, i: (bb, jnp.maximum(i * hb - 1, 0), 0))
    next_spec = pl.BlockSpec((None, HALO, d), lambda bb, i: (bb, jnp.minimum((i + 1) * hb, n_halo - 1), 0))
    full = lambda r, c: pl.BlockSpec((r, c), lambda bb, i: (0, 0))
    return pl.pallas_call(
        functools.partial(_conv_out_kernel, tm=tm, n_experts=n_experts),
        grid=(b, t // tm),
        in_specs=[x_spec, prev_spec, next_spec, x_spec, full(3, d), full(d, d), x_spec, _vec_spec(g1),
                  full(1, d), _vec_spec(sc2), _vec_spec(sh2),
                  pl.BlockSpec((2, d, LANES), lambda bb, i: (0, 0, 0))],
        out_specs=[x_spec, x_spec, pl.BlockSpec((None, tm, LANES), lambda bb, i: (bb, i, 0))],
        out_shape=[jax.ShapeDtypeStruct((b, t, d), F32), jax.ShapeDtypeStruct((b, t, d), h2_dtype),
                   jax.ShapeDtypeStruct((b, t, LANES), F32)],
        compiler_params=_params("parallel", "parallel"),
        name="conv_out",
    )(z, z, z, bg, conv_w, w_bf16, x, g1, n2, sc2, sh2, router_pad)


EXPERT_TILE = 1024
TOP_K = 2


def _rank_kernel(route_ref, rank_ref, count_ref, base_ref, *, tm, n_experts):
    @pl.when(pl.program_id(0) == 0)
    def _():
        base_ref[...] = jnp.zeros_like(base_ref)

    route = route_ref[...]
    lane = lax.broadcasted_iota(jnp.int32, (tm, LANES), 1)
    lanef = lane.astype(F32)
    i1 = jnp.sum(jnp.where(lane == n_experts, route, 0.0), axis=-1, keepdims=True)
    i2 = jnp.sum(jnp.where(lane == n_experts + 1, route, 0.0), axis=-1, keepdims=True)
    onehot = jnp.where(lanef == i1, 1.0, 0.0) + jnp.where(lanef == i2, 1.0, 0.0)
    r = lax.broadcasted_iota(jnp.int32, (tm, tm), 0)
    c = lax.broadcasted_iota(jnp.int32, (tm, tm), 1)
    lower = jnp.where(c < r, 1.0, 0.0).astype(BF16)
    before = jnp.dot(lower, onehot.astype(BF16), preferred_element_type=F32) + base_ref[...]
    r1 = jnp.sum(jnp.where(lanef == i1, before, 0.0), axis=-1, keepdims=True)
    r2 = jnp.sum(jnp.where(lanef == i2, before, 0.0), axis=-1, keepdims=True)
    packed = jnp.where(lane == 0, r1, jnp.where(lane == 1, r2, jnp.where(lane == 2, i1,
                       jnp.where(lane == 3, i2, 0.0))))
    rank_ref[...] = packed.T[0:8, :]
    base_ref[...] += jnp.sum(onehot, axis=0, keepdims=True)
    count_ref[...] = base_ref[...]


def _rank(route, n_experts):
    n = route.shape[0]
    tm = _pick(n, (512, 256))
    return pl.pallas_call(
        functools.partial(_rank_kernel, tm=tm, n_experts=n_experts),
        grid=(n // tm,),
        in_specs=[pl.BlockSpec((tm, LANES), lambda i: (i, 0))],
        out_specs=[pl.BlockSpec((8, tm), lambda i: (0, i)), pl.BlockSpec((1, LANES), lambda i: (0, 0))],
        out_shape=[jax.ShapeDtypeStruct((8, n), F32), jax.ShapeDtypeStruct((1, LANES), F32)],
        scratch_shapes=[pltpu.VMEM((1, LANES), F32)],
        compiler_params=_params("arbitrary"),
        name="moe_rank",
    )(route)


ZERO_ROWS = 128


def _dispatch_kernel(dest_ref, pad_ref, h_ref, xs_ref, zero_ref, sem, zsem, *, tm, n_ranges):
    copies = []
    for r in range(tm):
        for k in range(TOP_K):
            cp = pltpu.make_async_copy(h_ref.at[pl.ds(r, 1), :],
                                       xs_ref.at[pl.ds(dest_ref[0, k * tm + r], 1), :], sem)
            cp.start(priority=k)
            copies.append(cp)

    @pl.when(pl.program_id(0) == pl.num_programs(0) - 1)
    def _():
        zero_ref[...] = jnp.zeros_like(zero_ref)

        def row_clear(r):
            return pltpu.make_async_copy(zero_ref.at[pl.ds(0, 1), :], xs_ref.at[pl.ds(r, 1), :], zsem)

        def block_clear(r):
            start = pl.multiple_of(r, ZERO_ROWS)
            return pltpu.make_async_copy(zero_ref, xs_ref.at[pl.ds(start, ZERO_ROWS), :], zsem)

        for e in range(n_ranges):
            lo, mid, hi = pad_ref[0, e], pad_ref[1, e], pad_ref[2, e]
            nb = (hi - mid) // ZERO_ROWS
            lax.fori_loop(lo, mid, lambda r, c: (row_clear(r).start(), c)[1], 0)
            lax.fori_loop(0, nb, lambda t, c, mid=mid: (block_clear(mid + t * ZERO_ROWS).start(), c)[1], 0)
            lax.fori_loop(lo, mid, lambda r, c: (row_clear(r).wait(), c)[1], 0)
            lax.fori_loop(0, nb, lambda t, c, mid=mid: (block_clear(mid + t * ZERO_ROWS).wait(), c)[1], 0)

    for cp in copies:
        cp.wait()


def _dispatch(dest, pad_ranges, h2, n_rows):
    n, d = h2.shape
    tm = dest.shape[-1] // TOP_K
    n_ranges = pad_ranges.shape[1]
    return pl.pallas_call(
        functools.partial(_dispatch_kernel, tm=tm, n_ranges=n_ranges),
        grid=(n // tm,),
        in_specs=[pl.BlockSpec((None, 1, TOP_K * tm), lambda i: (i, 0, 0), memory_space=pltpu.SMEM),
                  pl.BlockSpec(memory_space=pltpu.SMEM),
                  pl.BlockSpec((tm, d), lambda i: (i, 0))],
        out_specs=pl.BlockSpec(memory_space=pl.ANY),
        out_shape=jax.ShapeDtypeStruct((n_rows, d), h2.dtype),
        scratch_shapes=[pltpu.VMEM((ZERO_ROWS, d), h2.dtype), pltpu.SemaphoreType.DMA(()),
                        pltpu.SemaphoreType.DMA(())],
        compiler_params=_params("arbitrary"),
        name="moe_dispatch",
    )(dest, pad_ranges, h2)


def _expert_kernel(te_ref, nu_ref, xs_ref, wg_ref, wu_ref, wd_ref, o_ref, hb_ref, acc_ref):
    del te_ref
    j = pl.program_id(0)
    f = pl.program_id(1)

    @pl.when(j < nu_ref[0])
    def _():
        @pl.when(f == 0)
        def _():
            hb_ref[...] = xs_ref[...].astype(BF16)
            acc_ref[...] = jnp.zeros_like(acc_ref)

        h = hb_ref[...]
        a = jnp.dot(h, wg_ref[...].astype(BF16), preferred_element_type=F32)
        u = jnp.dot(h, wu_ref[...].astype(BF16), preferred_element_type=F32)
        act = (a * _sigmoid(a) * u).astype(BF16)
        acc_ref[...] += jnp.dot(act, wd_ref[...].astype(BF16), preferred_element_type=F32)

        @pl.when(f == pl.num_programs(1) - 1)
        def _():
            o_ref[...] = acc_ref[...]

    @pl.when(jnp.logical_and(j >= nu_ref[0], f == pl.num_programs(1) - 1))
    def _():
        o_ref[...] = jnp.zeros_like(o_ref)


def _experts(tile_expert, n_used, xs, wg, wu, wd):
    n_rows, d = xs.shape
    f = wg.shape[-1]
    tf = _pick(f, (512, 256))
    nf = f // tf
    n_tiles = n_rows // EXPERT_TILE

    def row_map(j, ff, te, nu):
        return (jnp.minimum(j, nu[0] - 1), 0)

    def w_up_map(j, ff, te, nu):
        live = j < nu[0]
        return (te[jnp.minimum(j, nu[0] - 1)], 0, jnp.where(live, ff, nf - 1))

    def w_down_map(j, ff, te, nu):
        live = j < nu[0]
        return (te[jnp.minimum(j, nu[0] - 1)], jnp.where(live, ff, nf - 1), 0)

    grid_spec = pltpu.PrefetchScalarGridSpec(
        num_scalar_prefetch=2,
        grid=(n_tiles, nf),
        in_specs=[pl.BlockSpec((EXPERT_TILE, d), row_map),
                  pl.BlockSpec((None, d, tf), w_up_map),
                  pl.BlockSpec((None, d, tf), w_up_map),
                  pl.BlockSpec((None, tf, d), w_down_map)],
        out_specs=pl.BlockSpec((EXPERT_TILE, d), lambda j, ff, te, nu: (j, 0)),
        scratch_shapes=[pltpu.VMEM((EXPERT_TILE, d), BF16), pltpu.VMEM((EXPERT_TILE, d), F32)],
    )
    return pl.pallas_call(
        _expert_kernel,
        grid_spec=grid_spec,
        out_shape=jax.ShapeDtypeStruct((n_rows, d), F32),
        compiler_params=_params("arbitrary", "arbitrary"),
        name="moe_experts",
    )(tile_expert, n_used, xs, wg, wu, wd)


def _combine_kernel(dest_ref, next_ref, os_ref, route_ref, x_ref, g2_ref, o_ref, buf_ref, sem, *,
                    tm, n_experts):
    i = pl.program_id(0)
    slot = i % 2

    def gather(idx_ref, s, start):
        for r in range(tm):
            for k in range(TOP_K):
                cp = pltpu.make_async_copy(os_ref.at[pl.ds(idx_ref[0, k * tm + r], 1), :],
                                           buf_ref.at[s, k, pl.ds(r, 1), :], sem.at[s])
                if start:
                    cp.start(priority=k)
                else:
                    cp.wait()

    @pl.when(i == 0)
    def _():
        gather(dest_ref, slot, True)

    @pl.when(i + 1 < pl.num_programs(0))
    def _():
        gather(next_ref, 1 - slot, True)

    gather(dest_ref, slot, False)

    route = route_ref[...]
    lane = lax.broadcasted_iota(jnp.int32, (tm, LANES), 1)
    w1 = jnp.sum(jnp.where(lane == n_experts + 2, route, 0.0), axis=-1, keepdims=True)
    w2 = jnp.sum(jnp.where(lane == n_experts + 3, route, 0.0), axis=-1, keepdims=True)
    o_ref[...] = x_ref[...] + g2_ref[...] * (w1 * buf_ref[slot, 0] + w2 * buf_ref[slot, 1])


def _combine(dest, os, route, x1, g2, n_experts):
    b, t, d = x1.shape
    tm = dest.shape[-1] // TOP_K
    per_seq = t // tm
    n_steps = b * per_seq
    x_spec = pl.BlockSpec((None, tm, d), lambda i: (i // per_seq, i % per_seq, 0))
    if g2.shape[0] == 1:
        g_spec = pl.BlockSpec((None, 1, d), lambda i: (0, 0, 0))
    else:
        g_spec = pl.BlockSpec((None, 1, d), lambda i: (i // per_seq, 0, 0))
    return pl.pallas_call(
        functools.partial(_combine_kernel, tm=tm, n_experts=n_experts),
        grid=(n_steps,),
        in_specs=[pl.BlockSpec((None, 1, TOP_K * tm), lambda i: (i, 0, 0), memory_space=pltpu.SMEM),
                  pl.BlockSpec((None, 1, TOP_K * tm), lambda i: (jnp.minimum(i + 1, n_steps - 1), 0, 0),
                               memory_space=pltpu.SMEM),
                  pl.BlockSpec(memory_space=pl.ANY),
                  pl.BlockSpec((None, tm, LANES), lambda i: (i // per_seq, i % per_seq, 0)),
                  x_spec, g_spec],
        out_specs=x_spec,
        out_shape=jax.ShapeDtypeStruct((b, t, d), F32),
        scratch_shapes=[pltpu.VMEM((2, TOP_K, tm, d), F32), pltpu.SemaphoreType.DMA((2,))],
        compiler_params=_params("arbitrary"),
        name="moe_combine",
    )(dest, dest, os, route, x1, g2)


def _moe_routed(h2, route, x1, g2, wg, wu, wd, e0, n_experts):
    b, t, d = x1.shape
    n = b * t
    tm = _pick(t, (512, 256))
    rec = route.reshape(n, LANES)
    rank, counts = _rank(rec, n_experts)

    cnt = counts[0, :n_experts].astype(jnp.int32)
    padded = ((cnt + EXPERT_TILE - 1) // EXPERT_TILE) * EXPERT_TILE
    ends = jnp.cumsum(padded)
    offs = ends - padded
    n_tiles = (2 * n) // EXPERT_TILE + n_experts
    n_used = (ends[-1] // EXPERT_TILE).astype(jnp.int32).reshape(1)
    tile_start = jnp.arange(n_tiles, dtype=jnp.int32) * EXPERT_TILE
    tile_expert = jnp.minimum(jnp.sum(tile_start[:, None] >= ends[None, :], axis=1), n_experts - 1)
    slot = rank.astype(jnp.int32)
    d1 = jnp.take(offs, slot[2]) + slot[0]
    d2 = jnp.take(offs, slot[3]) + slot[1]
    dest = jnp.concatenate([d1.reshape(n // tm, tm), d2.reshape(n // tm, tm)], axis=1)[:, None, :]

    n_rows = n_tiles * EXPERT_TILE
    pad_lo = jnp.concatenate([offs + cnt, ends[-1:]])
    pad_hi = jnp.concatenate([ends, jnp.full((1,), n_rows, jnp.int32)])
    pad_mid = jnp.minimum(((pad_lo + ZERO_ROWS - 1) // ZERO_ROWS) * ZERO_ROWS, pad_hi)
    pad_ranges = jnp.stack([pad_lo, pad_mid, pad_hi]).astype(jnp.int32)
    xs = _dispatch(dest, pad_ranges, h2.reshape(n, d), n_rows)
    os = _experts((e0 + tile_expert).astype(jnp.int32), n_used, xs, wg, wu, wd)
    return _combine(dest, os, route, x1, g2, n_experts)


def _rope_tables(n_tokens):
    rows = n_tokens // GRID_W
    row, col = jnp.meshgrid(jnp.arange(rows, dtype=F32), jnp.arange(GRID_W, dtype=F32), indexing="ij")
    half = HEAD_DIM // 2
    inv_freq = ROPE_THETA ** (-jnp.arange(0, half, 2, dtype=F32) / half)
    ang = jnp.concatenate([row.reshape(-1, 1) * inv_freq, col.reshape(-1, 1) * inv_freq], axis=-1)
    cos = jnp.repeat(jnp.cos(ang), 2, axis=-1)
    sign = jnp.tile(jnp.array([-1.0, 1.0], F32), HEAD_DIM // 2)
    sin = jnp.repeat(jnp.sin(ang), 2, axis=-1) * sign
    reps = LANES // HEAD_DIM
    return jnp.tile(cos, (1, reps)), jnp.tile(sin, (1, reps))


def kernel(x, c, ctx, c_ctx, ada_w, ada_b, norm1_g, norm2_g, attn_w_in, attn_w_out, qnorm_a, knorm_a, qnorm_b, knorm_b, sink_b, ffn_w_gate, ffn_w_up, ffn_w_down, conv_w_in, conv_w, conv_w_out, router_w, moe_w_gate, moe_w_up, moe_w_down):
    bsz, n_tok, d = x.shape
    n_ctx = ctx.shape[1]
    depth = ada_w.shape[0]
    n_experts = router_w.shape[-1]
    assert n_ctx == CHUNK and n_tok % CHUNK == 0 and bsz + 1 <= 8
    assert attn_w_in.shape[-1] == IN_PROJ_WIDTH and d % LANES == 0 and n_experts + 4 <= LANES

    cc = jnp.zeros((8, d), F32).at[:bsz].set(c).at[bsz].set(c_ctx)
    mod = _modulation(cc, ada_w, ada_b)

    def mod_vec(layer, j, is_ctx):
        m = mod[layer, :, j * d:(j + 1) * d]
        return m[bsz:bsz + 1].reshape(1, 1, d) if is_ctx else m[:bsz].reshape(bsz, 1, d)

    cos_l, sin_l = _rope_tables(n_tok)
    cos_c = jnp.ones((n_ctx, LANES), F32)
    sin_c = jnp.zeros((n_ctx, LANES), F32)
    head_id = jnp.arange(LANES) // HEAD_DIM
    bd = (head_id[:, None] == head_id[None, :]).astype(BF16)
    no_sink = jnp.zeros((N_HEADS,), F32)
    tile2 = lambda v: jnp.tile(v, LANES // HEAD_DIM)

    xc = ctx
    for layer in range(depth):
        i = layer // 2
        ctx_needed = any(j % 2 == 0 for j in range(layer + 1, depth))
        n1 = norm1_g[layer].reshape(1, d)
        n2 = norm2_g[layer].reshape(1, d)
        mv = lambda j, is_ctx: mod_vec(layer, j, is_ctx)

        if layer % 2 == 0:
            w_in = attn_w_in[i].astype(BF16)
            w_out = attn_w_out[i].astype(BF16)
            qscale = ATTN_SCALE * LOG2E
            gains = jnp.stack([tile2(qnorm_a[i]) * qscale, tile2(knorm_a[i]),
                               tile2(qnorm_b[i]) * qscale, tile2(knorm_b[i])]).astype(F32)
            sink = sink_b[i].astype(F32)
            def logit_bound(gq, gk):
                raw = HEAD_DIM * jnp.max(jnp.abs(gq)) * jnp.max(jnp.abs(gk)) * qscale * 1.02
                return raw.astype(BF16).astype(F32)
            bound_a = logit_bound(qnorm_a[i], knorm_a[i])
            bound_b = jnp.maximum(logit_bound(qnorm_b[i], knorm_b[i]),
                                  (jnp.max(sink) * LOG2E * 1.02).astype(BF16).astype(F32))
            fast_a = bound_a <= FAST_MAX_LOGIT
            fast_b = jnp.logical_and(bound_b <= FAST_MAX_LOGIT,
                                     jnp.max(jnp.abs(sink)) * LOG2E <= FAST_MAX_LOGIT)
            bound_a = bound_a.reshape(1)
            bound_b = bound_b.reshape(1)
            qa_l, ka_l, va_l, qb_l, kb_l, vb_l = _attn_in(
                x, n1, mv(1, False), mv(0, False), w_in, gains, cos_l, sin_l, bd, rope=True)
            qa_c, ka_c, va_c, qb_c, kb_c, vb_c = _attn_in(
                xc, n1, mv(1, True), mv(0, True), w_in, gains, cos_c, sin_c, bd, rope=False)
            ka_all = jnp.concatenate([ka_c, ka_l], axis=1)
            va_all = jnp.concatenate([va_c, va_l], axis=1)
            kb_all = jnp.concatenate([kb_c, kb_l], axis=1)
            vb_all = jnp.concatenate([vb_c, vb_l], axis=1)
            oa = _attend(fast_a, qa_l, ka_all, va_all, no_sink, bound_a, mode="global", use_sink=False)
            ob = _attend(fast_b, qb_l, kb_all, vb_all, sink, bound_b, mode="window", use_sink=True)
            x, h2 = _attn_out(oa, ob, w_out, x, mv(2, False), n2, mv(4, False), mv(3, False))
            if ctx_needed:
                oa_c = _attend(fast_a, qa_c, ka_c, va_c, no_sink, bound_a, mode="ctx", use_sink=False)
                ob_c = _attend(fast_b, qb_c, kb_c, vb_c, sink, bound_b, mode="ctx", use_sink=True)
                xc, h2c = _attn_out(oa_c, ob_c, w_out, xc, mv(2, True), n2, mv(4, True), mv(3, True))
            wg = ffn_w_gate.astype(BF16)
            wu = ffn_w_up.astype(BF16)
            wd = ffn_w_down.astype(BF16)
            x = _ffn(h2, wg, wu, wd, None, x, mv(5, False), e0=i)
            if ctx_needed:
                xc = _ffn(h2c, wg, wu, wd, None, xc, mv(5, True), e0=i)
        else:
            w_in = conv_w_in[i].astype(BF16)
            w_out = conv_w_out[i].astype(BF16)
            cw = conv_w[i].astype(F32)
            router_f32 = jnp.zeros((d, LANES), F32).at[:, :n_experts].set(router_w[i])
            router_hi = router_f32.astype(BF16)
            router_pad = jnp.stack([router_hi, (router_f32 - router_hi.astype(F32)).astype(BF16)])
            wg = moe_w_gate.reshape((-1,) + moe_w_gate.shape[2:])
            wu = moe_w_up.reshape((-1,) + moe_w_up.shape[2:])
            wd = moe_w_down.reshape((-1,) + moe_w_down.shape[2:])
            e0 = i * n_experts
            bg, z = _conv_in(x, n1, mv(1, False), mv(0, False), w_in)
            x, h2, route = _conv_out(z, bg, cw, w_out, x, mv(2, False), n2, mv(4, False), mv(3, False),
                                     router_pad, n_experts, F32)
            x = _moe_routed(h2, route, x, mv(5, False), wg, wu, wd, e0, n_experts)
            if ctx_needed:
                bg, z = _conv_in(xc, n1, mv(1, True), mv(0, True), w_in)
                xc, h2c, gates_c = _conv_out(z, bg, cw, w_out, xc, mv(2, True), n2, mv(4, True),
                                             mv(3, True), router_pad, n_experts, BF16)
                flat = lambda a: a.reshape((1, bsz * n_ctx) + a.shape[2:])
                xc = _ffn(flat(h2c), wg, wu, wd, flat(gates_c), flat(xc), mv(5, True),
                          e0=e0, n_e=n_experts).reshape(bsz, n_ctx, d)
    return x
```

```python
import functools

import jax
import jax.numpy as jnp
from jax import lax
from jax.experimental import pallas as pl
from jax.experimental.pallas import tpu as pltpu

F32 = jnp.float32
BF16 = jnp.bfloat16

HEAD_DIM = 64
N_KV = 2
GROUP = 4
N_HEADS = N_KV * GROUP
Q_WIDTH = N_HEADS * HEAD_DIM
KV_WIDTH = N_KV * HEAD_DIM
IN_PROJ_WIDTH = 2 * (Q_WIDTH + 2 * KV_WIDTH)
GRID_W = 64
WINDOW = 128
ROPE_THETA = 10000.0
ATTN_SCALE = HEAD_DIM ** -0.5
EPS = 1e-6
N_ADA = 6
LOG2E = 1.4426950408889634
LANES = 128
CHUNK = 256
K_AUG = 2 * KV_WIDTH
V_ROWS = HEAD_DIM + 16
GLOBAL_STEP = 8
FAST_MAX_LOGIT = 40.0
NEG_BIG = -1e30
VMEM_LIMIT_BYTES = 56 * 1024 * 1024


def _params(*sem):
    return pltpu.CompilerParams(dimension_semantics=sem, vmem_limit_bytes=VMEM_LIMIT_BYTES)


def _pick(n, candidates):
    for t in candidates:
        if n % t == 0:
            return t
    return n


def _sigmoid(a):
    return 1.0 / (1.0 + jnp.exp(-a))


def _norm_mod(x, g, sc, sh):
    ms = jnp.mean(x * x, axis=-1, keepdims=True)
    return x * lax.rsqrt(ms + EPS) * g * (1.0 + sc) + sh


def _mod_kernel(c_ref, w_ref, b_ref, o_ref):
    c = c_ref[...]
    s = c * _sigmoid(c)
    o_ref[...] = jnp.dot(s, w_ref[...], preferred_element_type=F32,
                         precision=lax.Precision.HIGHEST) + b_ref[...]


def _modulation(cc, ada_w, ada_b):
    depth, d, n = ada_w.shape
    tn = _pick(n, (1536, 1024, 512))
    rows = cc.shape[0]
    return pl.pallas_call(
        _mod_kernel,
        grid=(depth, n // tn),
        in_specs=[
            pl.BlockSpec((rows, d), lambda l, j: (0, 0)),
            pl.BlockSpec((None, d, tn), lambda l, j: (l, 0, j)),
            pl.BlockSpec((None, 1, tn), lambda l, j: (l, 0, j)),
        ],
        out_specs=pl.BlockSpec((None, rows, tn), lambda l, j: (l, 0, j)),
        out_shape=jax.ShapeDtypeStruct((depth, rows, n), F32),
        compiler_params=_params("arbitrary", "arbitrary"),
        name="modulation",
    )(cc, ada_w, ada_b.reshape(depth, 1, n))


def _vec_spec(arr):
    d = arr.shape[-1]
    if arr.shape[0] == 1:
        return pl.BlockSpec((None, 1, d), lambda b, i, *_: (0, 0, 0))
    return pl.BlockSpec((None, 1, d), lambda b, i, *_: (b, 0, 0))


def _attn_in_kernel(x_ref, g_ref, sc_ref, sh_ref, w_ref, gains_ref, cos_ref, sin_ref, bd_ref,
                    qa_ref, ka_ref, va_ref, qb_ref, kb_ref, vb_ref, *, rope, tm):
    lane = lax.broadcasted_iota(jnp.int32, (CHUNK, LANES), 1)
    even = (lane & 1) == 0
    bd = bd_ref[...]
    ones_rows = jnp.where(lax.broadcasted_iota(jnp.int32, (V_ROWS - HEAD_DIM, CHUNK), 0) == 0,
                          1.0, 0.0).astype(BF16)
    groups = ((0, qa_ref, ka_ref, va_ref, 0), (Q_WIDTH + 2 * KV_WIDTH, qb_ref, kb_ref, vb_ref, 2))

    for j in range(tm // CHUNK):
        rows = slice(j * CHUNK, (j + 1) * CHUNK)
        h = _norm_mod(x_ref[rows, :], g_ref[...], sc_ref[...], sh_ref[...]).astype(BF16)
        y = jnp.dot(h, w_ref[...], preferred_element_type=F32)

        def head_norm(yc, gain, rows=rows):
            ss = jnp.dot((yc * yc).astype(BF16), bd, preferred_element_type=F32)
            t = yc * lax.rsqrt(ss * (1.0 / HEAD_DIM) + EPS) * gain
            if rope:
                partner = jnp.where(even, pltpu.roll(t, LANES - 1, 1), pltpu.roll(t, 1, 1))
                t = t * cos_ref[rows, :] + partner * sin_ref[rows, :]
            return t

        for col0, q_ref, k_ref, v_ref, grow in groups:
            gq = gains_ref[grow:grow + 1, :]
            gk = gains_ref[grow + 1:grow + 2, :]
            for c in range(Q_WIDTH // LANES):
                t = head_norm(y[:, col0 + c * LANES:col0 + (c + 1) * LANES], gq)
                q_ref[c * LANES:(c + 1) * LANES, rows] = t.T.astype(BF16)
            kcol = col0 + Q_WIDTH
            k_ref[rows, 0:KV_WIDTH] = head_norm(y[:, kcol:kcol + KV_WIDTH], gk).astype(BF16)
            k_ref[rows, KV_WIDTH:K_AUG] = jnp.where(lane == 0, 1.0, 0.0).astype(BF16)
            vt = y[:, kcol + KV_WIDTH:kcol + 2 * KV_WIDTH].T.astype(BF16)
            for kv in range(N_KV):
                r0 = kv * V_ROWS
                v_ref[j, r0:r0 + HEAD_DIM, :] = vt[kv * HEAD_DIM:(kv + 1) * HEAD_DIM, :]
                v_ref[j, r0 + HEAD_DIM:r0 + V_ROWS, :] = ones_rows


def _attn_in(x, g, sc, sh, w_bf16, gains, cos_t, sin_t, bd, *, rope):
    b, t, d = x.shape
    tm = _pick(t, (512, 256))
    nch = t // CHUNK
    kern = functools.partial(_attn_in_kernel, rope=rope, tm=tm)
    q_spec = pl.BlockSpec((None, Q_WIDTH, tm), lambda bb, i: (bb, 0, i))
    k_spec = pl.BlockSpec((None, tm, K_AUG), lambda bb, i: (bb, i, 0))
    v_spec = pl.BlockSpec((None, tm // CHUNK, N_KV * V_ROWS, CHUNK), lambda bb, i: (bb, i, 0, 0))
    q_shape = jax.ShapeDtypeStruct((b, Q_WIDTH, t), BF16)
    k_shape = jax.ShapeDtypeStruct((b, t, K_AUG), BF16)
    v_shape = jax.ShapeDtypeStruct((b, nch, N_KV * V_ROWS, CHUNK), BF16)
    return pl.pallas_call(
        kern,
        grid=(b, t // tm),
        in_specs=[
            pl.BlockSpec((None, tm, d), lambda bb, i: (bb, i, 0)),
            pl.BlockSpec((1, d), lambda bb, i: (0, 0)),
            _vec_spec(sc), _vec_spec(sh),
            pl.BlockSpec((d, IN_PROJ_WIDTH), lambda bb, i: (0, 0)),
            pl.BlockSpec((4, LANES), lambda bb, i: (0, 0)),
            pl.BlockSpec((tm, LANES), lambda bb, i: (i, 0)),
            pl.BlockSpec((tm, LANES), lambda bb, i: (i, 0)),
            pl.BlockSpec((LANES, LANES), lambda bb, i: (0, 0)),
        ],
        out_specs=[q_spec, k_spec, v_spec, q_spec, k_spec, v_spec],
        out_shape=[q_shape, k_shape, v_shape, q_shape, k_shape, v_shape],
        compiler_params=_params("parallel", "parallel"),
        name="attn_in",
    )(x, g, sc, sh, w_bf16, gains, cos_t, sin_t, bd)


def _attn_kernel(q_ref, k_ref, v_ref, sink_ref, o_ref, qpad_ref, s_ref, cmax_ref, m_ref, l_ref,
                 acc_ref, *, mode, use_sink, n_lat, tq):
    i = pl.program_id(1)
    w = GROUP * tq
    zeros = jnp.zeros((HEAD_DIM, w), BF16)
    for kv in range(N_KV):
        heads = range(kv * GROUP, (kv + 1) * GROUP)
        q4 = jnp.concatenate([q_ref[h * HEAD_DIM:(h + 1) * HEAD_DIM, :] for h in heads], axis=1)
        qpad_ref[kv] = jnp.concatenate([q4, zeros] if kv == 0 else [zeros, q4], axis=0)
        if use_sink:
            m_ref[kv] = jnp.concatenate(
                [jnp.full((1, tq), sink_ref[h] * LOG2E, F32) for h in heads], axis=1)
            l_ref[kv] = jnp.ones((1, w), F32)
        else:
            m_ref[kv] = jnp.full((1, w), NEG_BIG, F32)
            l_ref[kv] = jnp.zeros((1, w), F32)
        acc_ref[kv] = jnp.zeros((HEAD_DIM, w), F32)

    def scores(c, slot, mask):
        start = c * CHUNK if isinstance(c, int) else pl.multiple_of(c * CHUNK, CHUNK)
        kc = k_ref[pl.ds(start, CHUNK), 0:KV_WIDTH]
        for kv in range(N_KV):
            s = jnp.dot(kc, qpad_ref[kv], preferred_element_type=F32)
            if mask is not None:
                s = jnp.where(mask, s, -jnp.inf)
            s_ref[slot, kv] = s
            cmax_ref[slot, kv] = jnp.max(s, axis=0, keepdims=True)

    def absorb(c, slot):
        for kv in range(N_KV):
            m = m_ref[kv]
            m_new = jnp.maximum(m, cmax_ref[slot, kv])
            alpha = jnp.exp2(m - m_new)
            p = jnp.exp2(s_ref[slot, kv] - m_new)
            l_ref[kv] = alpha * l_ref[kv] + jnp.sum(p, axis=0, keepdims=True)
            m_ref[kv] = m_new
            vc = v_ref[c, kv * V_ROWS:kv * V_ROWS + HEAD_DIM, :]
            acc_ref[kv] = alpha * acc_ref[kv] + jnp.dot(vc, p.astype(BF16),
                                                        preferred_element_type=F32)

    scores(0, 0, None)
    if mode == "global":
        def body(j, carry):
            c = 2 * j
            scores(c + 1, 1, None)
            absorb(c, 0)
            scores(c + 2, 0, None)
            absorb(c + 1, 1)
            return carry
        lax.fori_loop(0, n_lat // 2, body, 0)
        c_end = 2 * (n_lat // 2)
        if n_lat % 2:
            scores(c_end + 1, 1, None)
            absorb(c_end, 0)
            absorb(c_end + 1, 1)
        else:
            absorb(c_end, 0)
    elif mode == "window":
        row = lax.broadcasted_iota(jnp.int32, (CHUNK, w), 0)
        col = lax.broadcasted_iota(jnp.int32, (CHUNK, w), 1) & (tq - 1)
        rel0 = col - row
        prev_c, prev_slot = 0, 0
        for d in (-1, 0, 1):
            cl = i + d
            ok = jnp.logical_and(cl >= 0, cl < n_lat)
            c = 1 + jnp.clip(cl, 0, n_lat - 1)
            mask = jnp.logical_and(jnp.abs(rel0 - d * CHUNK) <= WINDOW, ok)
            scores(c, 1 - prev_slot, mask)
            absorb(prev_c, prev_slot)
            prev_c, prev_slot = c, 1 - prev_slot
        absorb(prev_c, prev_slot)
    else:
        absorb(0, 0)

    for kv in range(N_KV):
        out = acc_ref[kv] * (1.0 / l_ref[kv])
        for g in range(GROUP):
            h = kv * GROUP + g
            o_ref[h * HEAD_DIM:(h + 1) * HEAD_DIM, :] = out[:, g * tq:(g + 1) * tq].astype(BF16)


def _attention(q_t, k_all, v_all, sink, bound, *, mode, use_sink):
    del bound
    b, _, tq_total = q_t.shape
    nk = k_all.shape[1]
    nc = v_all.shape[1]
    tq = CHUNK
    kern = functools.partial(_attn_kernel, mode=mode, use_sink=use_sink, n_lat=nc - 1, tq=tq)
    return pl.pallas_call(
        kern,
        grid=(b, tq_total // tq),
        in_specs=[
            pl.BlockSpec((None, Q_WIDTH, tq), lambda bb, i: (bb, 0, i)),
            pl.BlockSpec((None, nk, K_AUG), lambda bb, i: (bb, 0, 0)),
            pl.BlockSpec((None, nc, N_KV * V_ROWS, CHUNK), lambda bb, i: (bb, 0, 0, 0)),
            pl.BlockSpec(memory_space=pltpu.SMEM),
        ],
        out_specs=pl.BlockSpec((None, Q_WIDTH, tq), lambda bb, i: (bb, 0, i)),
        out_shape=jax.ShapeDtypeStruct((b, Q_WIDTH, tq_total), BF16),
        scratch_shapes=[pltpu.VMEM((N_KV, KV_WIDTH, GROUP * tq), BF16),
                        pltpu.VMEM((2, N_KV, CHUNK, GROUP * tq), F32),
                        pltpu.VMEM((2, N_KV, 1, GROUP * tq), F32),
                        pltpu.VMEM((N_KV, 1, GROUP * tq), F32),
                        pltpu.VMEM((N_KV, 1, GROUP * tq), F32),
                        pltpu.VMEM((N_KV, HEAD_DIM, GROUP * tq), F32)],
        compiler_params=_params("parallel", "parallel"),
        name="attn_" + mode,
    )(q_t, k_all, v_all, sink)


def _attn_fast_kernel(q_ref, k_ref, v_ref, sink_ref, bound_ref, o_ref, qa_ref, acc_ref, p_ref, *,
                      mode, use_sink, n_lat, tq):
    i = pl.program_id(1)
    w = GROUP * tq
    bound = bound_ref[0]
    zeros = jnp.zeros((HEAD_DIM, w), BF16)
    row = lax.broadcasted_iota(jnp.int32, (KV_WIDTH, w), 0)
    shift = jnp.where(row == 0, -bound, 0.0).astype(BF16)
    for kv in range(N_KV):
        heads = range(kv * GROUP, (kv + 1) * GROUP)
        q4 = jnp.concatenate([q_ref[h * HEAD_DIM:(h + 1) * HEAD_DIM, :] for h in heads], axis=1)
        qa_ref[kv] = jnp.concatenate(([q4, zeros] if kv == 0 else [zeros, q4]) + [shift], axis=0)
        acc_ref[kv] = jnp.zeros((V_ROWS, w), F32)

    def update(c, mask, n=1):
        start = c * CHUNK if isinstance(c, int) else pl.multiple_of(c * CHUNK, CHUNK)
        kc = k_ref[pl.ds(start, n * CHUNK), :]
        for kv in range(N_KV):
            s = jnp.dot(kc, qa_ref[kv], preferred_element_type=F32)
            if mask is not None:
                s = jnp.where(mask, s, -jnp.inf)
            p = jnp.exp2(s).astype(BF16)
            vc = jnp.concatenate([v_ref[c + j, kv * V_ROWS:(kv + 1) * V_ROWS, :] for j in range(n)],
                                 axis=1)
            acc_ref[kv] += jnp.dot(vc, p, preferred_element_type=F32)

    def probs(c, n, slot):
        start = c * CHUNK if isinstance(c, int) else pl.multiple_of(c * CHUNK, CHUNK)
        kc = k_ref[pl.ds(start, n * CHUNK), :]
        for kv in range(N_KV):
            s = jnp.dot(kc, qa_ref[kv], preferred_element_type=F32)
            p_ref[slot, kv, 0:n * CHUNK, :] = jnp.exp2(s).astype(BF16)

    def values(c, n, slot):
        for kv in range(N_KV):
            vc = jnp.concatenate([v_ref[c + j, kv * V_ROWS:(kv + 1) * V_ROWS, :] for j in range(n)],
                                 axis=1)
            acc_ref[kv] += jnp.dot(vc, p_ref[slot, kv, 0:n * CHUNK, :], preferred_element_type=F32)

    if mode == "global":
        n_full = (1 + n_lat) // GLOBAL_STEP
        steps = [(GLOBAL_STEP * s, GLOBAL_STEP) for s in range(n_full)]
        steps += [(c, 1) for c in range(GLOBAL_STEP * n_full, 1 + n_lat)]
        n_pairs = max((n_full - 1) // 2, 0)

        probs(*steps[0], 0)

        def body(t, carry):
            c = 2 * GLOBAL_STEP * t
            probs(c + GLOBAL_STEP, GLOBAL_STEP, 1)
            values(c, GLOBAL_STEP, 0)
            probs(c + 2 * GLOBAL_STEP, GLOBAL_STEP, 0)
            values(c + GLOBAL_STEP, GLOBAL_STEP, 1)
            return carry
        lax.fori_loop(0, n_pairs, body, 0)
        pending, slot = steps[2 * n_pairs], 0
        for nxt in steps[2 * n_pairs + 1:]:
            probs(*nxt, 1 - slot)
            values(*pending, slot)
            pending, slot = nxt, 1 - slot
        values(*pending, slot)
    elif mode == "window":
        half = CHUNK - WINDOW
        c_next = jnp.minimum(i + 2, n_lat)
        starts = (0, pl.multiple_of(i * CHUNK + half, half), pl.multiple_of((i + 1) * CHUNK, CHUNK),
                  pl.multiple_of(c_next * CHUNK, CHUNK))
        sizes = (CHUNK, WINDOW, CHUNK, WINDOW)
        kc = jnp.concatenate([k_ref[pl.ds(st, sz), :] for st, sz in zip(starts, sizes)], axis=0)
        n_keys = sum(sizes)
        rowk = lax.broadcasted_iota(jnp.int32, (n_keys, w), 0) - CHUNK
        col = lax.broadcasted_iota(jnp.int32, (n_keys, w), 1) & (tq - 1)
        in_band = jnp.abs(col - rowk + WINDOW) <= WINDOW
        first = jnp.where(i > 0, 0, WINDOW)
        last = jnp.where(i + 1 < n_lat, 2 * WINDOW + CHUNK, WINDOW + CHUNK)
        exists = jnp.logical_and(rowk >= first, rowk < last)
        mask = jnp.logical_or(rowk < 0, jnp.logical_and(in_band, exists))
        for kv in range(N_KV):
            rows = slice(kv * V_ROWS, (kv + 1) * V_ROWS)
            s = jnp.where(mask, jnp.dot(kc, qa_ref[kv], preferred_element_type=F32), -jnp.inf)
            p = jnp.exp2(s).astype(BF16)
            vc = jnp.concatenate([v_ref[0, rows, :], v_ref[i, rows, half:CHUNK], v_ref[i + 1, rows, :],
                                  v_ref[c_next, rows, 0:WINDOW]], axis=1)
            acc_ref[kv] += jnp.dot(vc, p, preferred_element_type=F32)
    else:
        update(0, None)

    for kv in range(N_KV):
        acc = acc_ref[kv]
        l = acc[HEAD_DIM:HEAD_DIM + 1, :]
        if use_sink:
            l = l + jnp.concatenate(
                [jnp.full((1, tq), jnp.exp2(sink_ref[kv * GROUP + g] * LOG2E - bound), F32)
                 for g in range(GROUP)], axis=1)
        out = acc[0:HEAD_DIM, :] * (1.0 / l)
        for g in range(GROUP):
            h = kv * GROUP + g
            o_ref[h * HEAD_DIM:(h + 1) * HEAD_DIM, :] = out[:, g * tq:(g + 1) * tq].astype(BF16)


def _attention_fast(q_t, k_all, v_all, sink, bound, *, mode, use_sink):
    b, _, tq_total = q_t.shape
    nk = k_all.shape[1]
    nc = v_all.shape[1]
    tq = CHUNK
    kern = functools.partial(_attn_fast_kernel, mode=mode, use_sink=use_sink, n_lat=nc - 1, tq=tq)
    return pl.pallas_call(
        kern,
        grid=(b, tq_total // tq),
        in_specs=[
            pl.BlockSpec((None, Q_WIDTH, tq), lambda bb, i: (bb, 0, i)),
            pl.BlockSpec((None, nk, K_AUG), lambda bb, i: (bb, 0, 0)),
            pl.BlockSpec((None, nc, N_KV * V_ROWS, CHUNK), lambda bb, i: (bb, 0, 0, 0)),
            pl.BlockSpec(memory_space=pltpu.SMEM),
            pl.BlockSpec(memory_space=pltpu.SMEM),
        ],
        out_specs=pl.BlockSpec((None, Q_WIDTH, tq), lambda bb, i: (bb, 0, i)),
        out_shape=jax.ShapeDtypeStruct((b, Q_WIDTH, tq_total), BF16),
        scratch_shapes=[pltpu.VMEM((N_KV, K_AUG, GROUP * tq), BF16),
                        pltpu.VMEM((N_KV, V_ROWS, GROUP * tq), F32),
                        pltpu.VMEM((2, N_KV, (GLOBAL_STEP if mode == "global" else 1) * CHUNK,
                                    GROUP * tq), BF16)],
        compiler_params=_params("parallel", "parallel"),
        name="attn_fast_" + mode,
    )(q_t, k_all, v_all, sink, bound)


def _attend(fast_ok, *args, **kw):
    return lax.cond(fast_ok, functools.partial(_attention_fast, **kw),
                    functools.partial(_attention, **kw), *args)


_TN_DIMS = (((0,), (0,)), ((), ()))


def _attn_out_kernel(oa_ref, ob_ref, w_ref, x_ref, g1_ref, n2_ref, sc2_ref, sh2_ref, x1_ref, h2_ref):
    tm = x_ref.shape[0]
    sub = min(tm, CHUNK)
    for j in range(tm // sub):
        rows = slice(j * sub, (j + 1) * sub)
        mix = lax.dot_general(oa_ref[:, rows], w_ref[0:Q_WIDTH, :], _TN_DIMS, preferred_element_type=F32)
        mix = mix + lax.dot_general(ob_ref[:, rows], w_ref[Q_WIDTH:2 * Q_WIDTH, :], _TN_DIMS,
                                    preferred_element_type=F32)
        x1 = x_ref[rows, :] + g1_ref[...] * mix
        x1_ref[rows, :] = x1
        h2_ref[rows, :] = _norm_mod(x1, n2_ref[...], sc2_ref[...], sh2_ref[...]).astype(BF16)


def _attn_out(oa, ob, w_bf16, x, g1, n2, sc2, sh2):
    b, t, d = x.shape
    tm = _pick(t, (512, 256))
    o_spec = pl.BlockSpec((None, Q_WIDTH, tm), lambda bb, i: (bb, 0, i))
    x_spec = pl.BlockSpec((None, tm, d), lambda bb, i: (bb, i, 0))
    return pl.pallas_call(
        _attn_out_kernel,
        grid=(b, t // tm),
        in_specs=[o_spec, o_spec,
                  pl.BlockSpec((2 * Q_WIDTH, d), lambda bb, i: (0, 0)),
                  x_spec, _vec_spec(g1),
                  pl.BlockSpec((1, d), lambda bb, i: (0, 0)),
                  _vec_spec(sc2), _vec_spec(sh2)],
        out_specs=[x_spec, x_spec],
        out_shape=[jax.ShapeDtypeStruct((b, t, d), F32), jax.ShapeDtypeStruct((b, t, d), BF16)],
        compiler_params=_params("parallel", "parallel"),
        name="attn_out",
    )(oa, ob, w_bf16, x, g1, n2, sc2, sh2)


def _ffn_kernel(*refs, gated, tm):
    if gated:
        h_ref, wg_ref, wu_ref, wd_ref, gate_ref, x_ref, g2_ref, o_ref, acc_ref = refs
    else:
        h_ref, wg_ref, wu_ref, wd_ref, x_ref, g2_ref, o_ref, acc_ref = refs
    e = pl.program_id(2)
    f = pl.program_id(3)
    first = jnp.logical_and(e == 0, f == 0)
    last = jnp.logical_and(e == pl.num_programs(2) - 1, f == pl.num_programs(3) - 1)

    @pl.when(first)
    def _():
        acc_ref[...] = jnp.zeros_like(acc_ref)

    h = h_ref[...]
    a = jnp.dot(h, wg_ref[...].astype(BF16), preferred_element_type=F32)
    u = jnp.dot(h, wu_ref[...].astype(BF16), preferred_element_type=F32)
    act = (a * _sigmoid(a) * u).astype(BF16)
    y = jnp.dot(act, wd_ref[...].astype(BF16), preferred_element_type=F32)
    if gated:
        lane = lax.broadcasted_iota(jnp.int32, (tm, LANES), 1)
        ge = jnp.sum(jnp.where(lane == e, gate_ref[...], 0.0), axis=-1, keepdims=True)
        y = y * ge
    acc_ref[...] += y

    @pl.when(last)
    def _():
        o_ref[...] = x_ref[...] + g2_ref[...] * acc_ref[...]


def _ffn(h2, wg, wu, wd, gates, x1, g2, e0=0, n_e=1):
    b, t, d = x1.shape
    f = wg.shape[-1]
    gated = gates is not None
    tm = _pick(t, (1024, 512, 256) if gated else (512, 256))
    resident = n_e == 1 and 3 * d * f * 2 <= VMEM_LIMIT_BYTES // 3
    tf = f if resident else _pick(f, (512, 256))
    mode = dict(pipeline_mode=pl.Buffered(1)) if resident else {}
    x_spec = pl.BlockSpec((None, tm, d), lambda bb, i, e, j: (bb, i, 0))
    in_specs = [x_spec,
                pl.BlockSpec((None, d, tf), lambda bb, i, e, j: (e0 + e, 0, j), **mode),
                pl.BlockSpec((None, d, tf), lambda bb, i, e, j: (e0 + e, 0, j), **mode),
                pl.BlockSpec((None, tf, d), lambda bb, i, e, j: (e0 + e, j, 0), **mode)]
    args = [h2, wg, wu, wd]
    if gated:
        in_specs.append(pl.BlockSpec((None, tm, LANES), lambda bb, i, e, j: (bb, i, 0)))
        args.append(gates)
    in_specs += [x_spec, _vec_spec(g2)]
    args += [x1, g2]
    return pl.pallas_call(
        functools.partial(_ffn_kernel, gated=gated, tm=tm),
        grid=(b, t // tm, n_e, f // tf),
        in_specs=in_specs,
        out_specs=x_spec,
        out_shape=jax.ShapeDtypeStruct((b, t, d), F32),
        scratch_shapes=[pltpu.VMEM((tm, d), F32)],
        compiler_params=_params("parallel", "parallel", "arbitrary", "arbitrary"),
        name="ffn_gated" if gated else "ffn_dense",
    )(*args)


def _conv_in_kernel(x_ref, g_ref, sc_ref, sh_ref, w_ref, bg_ref, z_ref, *, d):
    h = _norm_mod(x_ref[...], g_ref[...], sc_ref[...], sh_ref[...]).astype(BF16)
    y = jnp.dot(h, w_ref[...], preferred_element_type=F32)
    bg_ref[...] = y[:, 0:d].astype(BF16)
    z_ref[...] = (y[:, d:2 * d] * y[:, 2 * d:3 * d]).astype(BF16)


def _conv_in(x, g, sc, sh, w_bf16):
    b, t, d = x.shape
    tm = _pick(t, (512, 256))
    x_spec = pl.BlockSpec((None, tm, d), lambda bb, i: (bb, i, 0))
    return pl.pallas_call(
        functools.partial(_conv_in_kernel, d=d),
        grid=(b, t // tm),
        in_specs=[x_spec, pl.BlockSpec((1, d), lambda bb, i: (0, 0)), _vec_spec(sc), _vec_spec(sh),
                  pl.BlockSpec((d, 3 * d), lambda bb, i: (0, 0))],
        out_specs=[x_spec, x_spec],
        out_shape=[jax.ShapeDtypeStruct((b, t, d), BF16), jax.ShapeDtypeStruct((b, t, d), BF16)],
        compiler_params=_params("parallel", "parallel"),
        name="conv_in",
    )(x, g, sc, sh, w_bf16)


HALO = 16


def _conv_out_kernel(z_ref, zp_ref, zn_ref, bg_ref, cw_ref, w_ref, x_ref, g1_ref, n2_ref, sc2_ref,
                     sh2_ref, rw_ref, x1_ref, h2_ref, gate_ref, *, tm, n_experts):
    i = pl.program_id(1)
    z = z_ref[...].astype(F32)
    prev = jnp.where(i > 0, zp_ref[HALO - 1:HALO, :].astype(F32), 0.0)
    nxt = jnp.where(i < pl.num_programs(1) - 1, zn_ref[0:1, :].astype(F32), 0.0)
    row = lax.broadcasted_iota(jnp.int32, z.shape, 0)
    z_dn = jnp.where(row == 0, prev, pltpu.roll(z, 1, 0))
    z_up = jnp.where(row == tm - 1, nxt, pltpu.roll(z, tm - 1, 0))
    conv = z_dn * cw_ref[0:1, :] + z * cw_ref[1:2, :] + z_up * cw_ref[2:3, :]
    v = (bg_ref[...].astype(F32) * conv).astype(BF16)
    sub = min(tm, CHUNK)
    lane = lax.broadcasted_iota(jnp.int32, (sub, LANES), 1)
    lanef = lane.astype(F32)
    for j in range(tm // sub):
        rows = slice(j * sub, (j + 1) * sub)
        mix = jnp.dot(v[rows, :], w_ref[...], preferred_element_type=F32)
        x1 = x_ref[rows, :] + g1_ref[...] * mix
        x1_ref[rows, :] = x1
        h2 = _norm_mod(x1, n2_ref[...], sc2_ref[...], sh2_ref[...])
        h2_ref[rows, :] = h2.astype(h2_ref.dtype)

        h_hi = h2.astype(BF16)
        h_lo = (h2 - h_hi.astype(F32)).astype(BF16)
        logits = (jnp.dot(h_hi, rw_ref[0], preferred_element_type=F32)
                  + jnp.dot(h_lo, rw_ref[0], preferred_element_type=F32)
                  + jnp.dot(h_hi, rw_ref[1], preferred_element_type=F32))
        logits = jnp.where(lane < n_experts, logits, -jnp.inf)
        m1 = jnp.max(logits, axis=-1, keepdims=True)
        i1 = jnp.min(jnp.where(logits == m1, lanef, float(LANES)), axis=-1, keepdims=True)
        rest = jnp.where(lanef == i1, -jnp.inf, logits)
        m2 = jnp.max(rest, axis=-1, keepdims=True)
        i2 = jnp.min(jnp.where(rest == m2, lanef, float(LANES)), axis=-1, keepdims=True)
        e2 = jnp.exp(m2 - m1)
        w1 = 1.0 / (1.0 + e2)
        w2 = e2 * w1
        rec = jnp.where(lanef == i1, w1, 0.0) + jnp.where(lanef == i2, w2, 0.0)
        for k, val in enumerate((i1, i2, w1, w2)):
            rec = jnp.where(lane == n_experts + k, val, rec)
        gate_ref[rows, :] = rec


def _conv_out(z, bg, conv_w, w_bf16, x, g1, n2, sc2, sh2, router_pad, n_experts, h2_dtype):
    b, t, d = x.shape
    tm = _pick(t, (512, 256))
    hb = tm // HALO
    n_halo = t // HALO
    x_spec = pl.BlockSpec((None, tm, d), lambda bb, i: (bb, i, 0))
    prev_spec = pl.BlockSpec((None, HALO, d), lambda bb, i: (bb, jnp.maximum(i * hb - 1, 0), 0))
    next_spec = pl.BlockSpec((None, HALO, d), lambda bb, i: (bb, jnp.minimum((i + 1) * hb, n_halo - 1), 0))
    full = lambda r, c: pl.BlockSpec((r, c), lambda bb, i: (0, 0))
    return pl.pallas_call(
        functools.partial(_conv_out_kernel, tm=tm, n_experts=n_experts),
        grid=(b, t // tm),
        in_specs=[x_spec, prev_spec, next_spec, x_spec, full(3, d), full(d, d), x_spec, _vec_spec(g1),
                  full(1, d), _vec_spec(sc2), _vec_spec(sh2),
                  pl.BlockSpec((2, d, LANES), lambda bb, i: (0, 0, 0))],
        out_specs=[x_spec, x_spec, pl.BlockSpec((None, tm, LANES), lambda bb, i: (bb, i, 0))],
        out_shape=[jax.ShapeDtypeStruct((b, t, d), F32), jax.ShapeDtypeStruct((b, t, d), h2_dtype),
                   jax.ShapeDtypeStruct((b, t, LANES), F32)],
        compiler_params=_params("parallel", "parallel"),
        name="conv_out",
    )(z, z, z, bg, conv_w, w_bf16, x, g1, n2, sc2, sh2, router_pad)


EXPERT_TILE = 1024
TOP_K = 2


def _rank_kernel(route_ref, rank_ref, count_ref, base_ref, *, tm, n_experts):
    @pl.when(pl.program_id(0) == 0)
    def _():
        base_ref[...] = jnp.zeros_like(base_ref)

    route = route_ref[...]
    lane = lax.broadcasted_iota(jnp.int32, (tm, LANES), 1)
    lanef = lane.astype(F32)
    i1 = jnp.sum(jnp.where(lane == n_experts, route, 0.0), axis=-1, keepdims=True)
    i2 = jnp.sum(jnp.where(lane == n_experts + 1, route, 0.0), axis=-1, keepdims=True)
    onehot = jnp.where(lanef == i1, 1.0, 0.0) + jnp.where(lanef == i2, 1.0, 0.0)
    r = lax.broadcasted_iota(jnp.int32, (tm, tm), 0)
    c = lax.broadcasted_iota(jnp.int32, (tm, tm), 1)
    lower = jnp.where(c < r, 1.0, 0.0).astype(BF16)
    before = jnp.dot(lower, onehot.astype(BF16), preferred_element_type=F32) + base_ref[...]
    r1 = jnp.sum(jnp.where(lanef == i1, before, 0.0), axis=-1, keepdims=True)
    r2 = jnp.sum(jnp.where(lanef == i2, before, 0.0), axis=-1, keepdims=True)
    packed = jnp.where(lane == 0, r1, jnp.where(lane == 1, r2, jnp.where(lane == 2, i1,
                       jnp.where(lane == 3, i2, 0.0))))
    rank_ref[...] = packed.T[0:8, :]
    base_ref[...] += jnp.sum(onehot, axis=0, keepdims=True)
    count_ref[...] = base_ref[...]


def _rank(route, n_experts):
    n = route.shape[0]
    tm = _pick(n, (512, 256))
    return pl.pallas_call(
        functools.partial(_rank_kernel, tm=tm, n_experts=n_experts),
        grid=(n // tm,),
        in_specs=[pl.BlockSpec((tm, LANES), lambda i: (i, 0))],
        out_specs=[pl.BlockSpec((8, tm), lambda i: (0, i)), pl.BlockSpec((1, LANES), lambda i: (0, 0))],
        out_shape=[jax.ShapeDtypeStruct((8, n), F32), jax.ShapeDtypeStruct((1, LANES), F32)],
        scratch_shapes=[pltpu.VMEM((1, LANES), F32)],
        compiler_params=_params("arbitrary"),
        name="moe_rank",
    )(route)


ZERO_ROWS = 128


def _dispatch_kernel(dest_ref, pad_ref, h_ref, xs_ref, zero_ref, sem, zsem, *, tm, n_ranges):
    copies = []
    for r in range(tm):
        for k in range(TOP_K):
            cp = pltpu.make_async_copy(h_ref.at[pl.ds(r, 1), :],
                                       xs_ref.at[pl.ds(dest_ref[0, k * tm + r], 1), :], sem)
            cp.start(priority=k)
            copies.append(cp)

    @pl.when(pl.program_id(0) == pl.num_programs(0) - 1)
    def _():
        zero_ref[...] = jnp.zeros_like(zero_ref)

        def row_clear(r):
            return pltpu.make_async_copy(zero_ref.at[pl.ds(0, 1), :], xs_ref.at[pl.ds(r, 1), :], zsem)

        def block_clear(r):
            start = pl.multiple_of(r, ZERO_ROWS)
            return pltpu.make_async_copy(zero_ref, xs_ref.at[pl.ds(start, ZERO_ROWS), :], zsem)

        for e in range(n_ranges):
            lo, mid, hi = pad_ref[0, e], pad_ref[1, e], pad_ref[2, e]
            nb = (hi - mid) // ZERO_ROWS
            lax.fori_loop(lo, mid, lambda r, c: (row_clear(r).start(), c)[1], 0)
            lax.fori_loop(0, nb, lambda t, c, mid=mid: (block_clear(mid + t * ZERO_ROWS).start(), c)[1], 0)
            lax.fori_loop(lo, mid, lambda r, c: (row_clear(r).wait(), c)[1], 0)
            lax.fori_loop(0, nb, lambda t, c, mid=mid: (block_clear(mid + t * ZERO_ROWS).wait(), c)[1], 0)

    for cp in copies:
        cp.wait()


def _dispatch(dest, pad_ranges, h2, n_rows):
    n, d = h2.shape
    tm = dest.shape[-1] // TOP_K
    n_ranges = pad_ranges.shape[1]
    return pl.pallas_call(
        functools.partial(_dispatch_kernel, tm=tm, n_ranges=n_ranges),
        grid=(n // tm,),
        in_specs=[pl.BlockSpec((None, 1, TOP_K * tm), lambda i: (i, 0, 0), memory_space=pltpu.SMEM),
                  pl.BlockSpec(memory_space=pltpu.SMEM),
                  pl.BlockSpec((tm, d), lambda i: (i, 0))],
        out_specs=pl.BlockSpec(memory_space=pl.ANY),
        out_shape=jax.ShapeDtypeStruct((n_rows, d), h2.dtype),
        scratch_shapes=[pltpu.VMEM((ZERO_ROWS, d), h2.dtype), pltpu.SemaphoreType.DMA(()),
                        pltpu.SemaphoreType.DMA(())],
        compiler_params=_params("arbitrary"),
        name="moe_dispatch",
    )(dest, pad_ranges, h2)


def _expert_kernel(te_ref, nu_ref, xs_ref, wg_ref, wu_ref, wd_ref, o_ref, hb_ref):
    del te_ref
    j = pl.program_id(0)
    f = pl.program_id(1)

    @pl.when(f == 0)
    def _():
        o_ref[...] = jnp.zeros_like(o_ref)

    @pl.when(j < nu_ref[0])
    def _():
        @pl.when(f == 0)
        def _():
            hb_ref[...] = xs_ref[...].astype(BF16)

        h = hb_ref[...]
        a = jnp.dot(h, wg_ref[...].astype(BF16), preferred_element_type=F32)
        u = jnp.dot(h, wu_ref[...].astype(BF16), preferred_element_type=F32)
        act = (a * _sigmoid(a) * u).astype(BF16)
        o_ref[...] += jnp.dot(act, wd_ref[...].astype(BF16), preferred_element_type=F32)


def _experts(tile_expert, n_used, xs, wg, wu, wd):
    n_rows, d = xs.shape
    f = wg.shape[-1]
    tf = _pick(f, (512, 256))
    nf = f // tf
    n_tiles = n_rows // EXPERT_TILE

    def row_map(j, ff, te, nu):
        return (jnp.minimum(j, nu[0] - 1), 0)

    def w_up_map(j, ff, te, nu):
        live = j < nu[0]
        return (te[jnp.minimum(j, nu[0] - 1)], 0, jnp.where(live, ff, nf - 1))

    def w_down_map(j, ff, te, nu):
        live = j < nu[0]
        return (te[jnp.minimum(j, nu[0] - 1)], jnp.where(live, ff, nf - 1), 0)

    grid_spec = pltpu.PrefetchScalarGridSpec(
        num_scalar_prefetch=2,
        grid=(n_tiles, nf),
        in_specs=[pl.BlockSpec((EXPERT_TILE, d), row_map),
                  pl.BlockSpec((None, d, tf), w_up_map),
                  pl.BlockSpec((None, d, tf), w_up_map),
                  pl.BlockSpec((None, tf, d), w_down_map)],
        out_specs=pl.BlockSpec((EXPERT_TILE, d), lambda j, ff, te, nu: (j, 0)),
        scratch_shapes=[pltpu.VMEM((EXPERT_TILE, d), BF16)],
    )
    return pl.pallas_call(
        _expert_kernel,
        grid_spec=grid_spec,
        out_shape=jax.ShapeDtypeStruct((n_rows, d), F32),
        compiler_params=_params("arbitrary", "arbitrary"),
        name="moe_experts",
    )(tile_expert, n_used, xs, wg, wu, wd)


def _combine_kernel(dest_ref, next_ref, os_ref, route_ref, x_ref, g2_ref, o_ref, buf_ref, sem, *,
                    tm, n_experts):
    i = pl.program_id(0)
    slot = i % 2

    def gather(idx_ref, s, start):
        for r in range(tm):
            for k in range(TOP_K):
                cp = pltpu.make_async_copy(os_ref.at[pl.ds(idx_ref[0, k * tm + r], 1), :],
                                           buf_ref.at[s, k, pl.ds(r, 1), :], sem.at[s])
                if start:
                    cp.start(priority=k)
                else:
                    cp.wait()

    @pl.when(i == 0)
    def _():
        gather(dest_ref, slot, True)

    @pl.when(i + 1 < pl.num_programs(0))
    def _():
        gather(next_ref, 1 - slot, True)

    gather(dest_ref, slot, False)

    route = route_ref[...]
    lane = lax.broadcasted_iota(jnp.int32, (tm, LANES), 1)
    w1 = jnp.sum(jnp.where(lane == n_experts + 2, route, 0.0), axis=-1, keepdims=True)
    w2 = jnp.sum(jnp.where(lane == n_experts + 3, route, 0.0), axis=-1, keepdims=True)
    o_ref[...] = x_ref[...] + g2_ref[...] * (w1 * buf_ref[slot, 0] + w2 * buf_ref[slot, 1])


def _combine(dest, os, route, x1, g2, n_experts):
    b, t, d = x1.shape
    tm = dest.shape[-1] // TOP_K
    per_seq = t // tm
    n_steps = b * per_seq
    x_spec = pl.BlockSpec((None, tm, d), lambda i: (i // per_seq, i % per_seq, 0))
    if g2.shape[0] == 1:
        g_spec = pl.BlockSpec((None, 1, d), lambda i: (0, 0, 0))
    else:
        g_spec = pl.BlockSpec((None, 1, d), lambda i: (i // per_seq, 0, 0))
    return pl.pallas_call(
        functools.partial(_combine_kernel, tm=tm, n_experts=n_experts),
        grid=(n_steps,),
        in_specs=[pl.BlockSpec((None, 1, TOP_K * tm), lambda i: (i, 0, 0), memory_space=pltpu.SMEM),
                  pl.BlockSpec((None, 1, TOP_K * tm), lambda i: (jnp.minimum(i + 1, n_steps - 1), 0, 0),
                               memory_space=pltpu.SMEM),
                  pl.BlockSpec(memory_space=pl.ANY),
                  pl.BlockSpec((None, tm, LANES), lambda i: (i // per_seq, i % per_seq, 0)),
                  x_spec, g_spec],
        out_specs=x_spec,
        out_shape=jax.ShapeDtypeStruct((b, t, d), F32),
        scratch_shapes=[pltpu.VMEM((2, TOP_K, tm, d), F32), pltpu.SemaphoreType.DMA((2,))],
        compiler_params=_params("arbitrary"),
        name="moe_combine",
    )(dest, dest, os, route, x1, g2)


def _moe_routed(h2, route, x1, g2, wg, wu, wd, e0, n_experts):
    b, t, d = x1.shape
    n = b * t
    tm = _pick(t, (512, 256))
    rec = route.reshape(n, LANES)
    rank, counts = _rank(rec, n_experts)

    cnt = counts[0, :n_experts].astype(jnp.int32)
    padded = ((cnt + EXPERT_TILE - 1) // EXPERT_TILE) * EXPERT_TILE
    ends = jnp.cumsum(padded)
    offs = ends - padded
    n_tiles = (2 * n) // EXPERT_TILE + n_experts
    n_used = (ends[-1] // EXPERT_TILE).astype(jnp.int32).reshape(1)
    tile_start = jnp.arange(n_tiles, dtype=jnp.int32) * EXPERT_TILE
    tile_expert = jnp.minimum(jnp.sum(tile_start[:, None] >= ends[None, :], axis=1), n_experts - 1)
    slot = rank.astype(jnp.int32)
    d1 = jnp.take(offs, slot[2]) + slot[0]
    d2 = jnp.take(offs, slot[3]) + slot[1]
    dest = jnp.concatenate([d1.reshape(n // tm, tm), d2.reshape(n // tm, tm)], axis=1)[:, None, :]

    n_rows = n_tiles * EXPERT_TILE
    pad_lo = jnp.concatenate([offs + cnt, ends[-1:]])
    pad_hi = jnp.concatenate([ends, jnp.full((1,), n_rows, jnp.int32)])
    pad_mid = jnp.minimum(((pad_lo + ZERO_ROWS - 1) // ZERO_ROWS) * ZERO_ROWS, pad_hi)
    pad_ranges = jnp.stack([pad_lo, pad_mid, pad_hi]).astype(jnp.int32)
    xs = _dispatch(dest, pad_ranges, h2.reshape(n, d), n_rows)
    os = _experts((e0 + tile_expert).astype(jnp.int32), n_used, xs, wg, wu, wd)
    return _combine(dest, os, route, x1, g2, n_experts)


def _rope_tables(n_tokens):
    rows = n_tokens // GRID_W
    row, col = jnp.meshgrid(jnp.arange(rows, dtype=F32), jnp.arange(GRID_W, dtype=F32), indexing="ij")
    half = HEAD_DIM // 2
    inv_freq = ROPE_THETA ** (-jnp.arange(0, half, 2, dtype=F32) / half)
    ang = jnp.concatenate([row.reshape(-1, 1) * inv_freq, col.reshape(-1, 1) * inv_freq], axis=-1)
    cos = jnp.repeat(jnp.cos(ang), 2, axis=-1)
    sign = jnp.tile(jnp.array([-1.0, 1.0], F32), HEAD_DIM // 2)
    sin = jnp.repeat(jnp.sin(ang), 2, axis=-1) * sign
    reps = LANES // HEAD_DIM
    return jnp.tile(cos, (1, reps)), jnp.tile(sin, (1, reps))


def kernel(x, c, ctx, c_ctx, ada_w, ada_b, norm1_g, norm2_g, attn_w_in, attn_w_out, qnorm_a, knorm_a, qnorm_b, knorm_b, sink_b, ffn_w_gate, ffn_w_up, ffn_w_down, conv_w_in, conv_w, conv_w_out, router_w, moe_w_gate, moe_w_up, moe_w_down):
    bsz, n_tok, d = x.shape
    n_ctx = ctx.shape[1]
    depth = ada_w.shape[0]
    n_experts = router_w.shape[-1]
    assert n_ctx == CHUNK and n_tok % CHUNK == 0 and bsz + 1 <= 8
    assert attn_w_in.shape[-1] == IN_PROJ_WIDTH and d % LANES == 0 and n_experts + 4 <= LANES

    cc = jnp.zeros((8, d), F32).at[:bsz].set(c).at[bsz].set(c_ctx)
    mod = _modulation(cc, ada_w, ada_b)

    def mod_vec(layer, j, is_ctx):
        m = mod[layer, :, j * d:(j + 1) * d]
        return m[bsz:bsz + 1].reshape(1, 1, d) if is_ctx else m[:bsz].reshape(bsz, 1, d)

    cos_l, sin_l = _rope_tables(n_tok)
    cos_c = jnp.ones((n_ctx, LANES), F32)
    sin_c = jnp.zeros((n_ctx, LANES), F32)
    head_id = jnp.arange(LANES) // HEAD_DIM
    bd = (head_id[:, None] == head_id[None, :]).astype(BF16)
    no_sink = jnp.zeros((N_HEADS,), F32)
    tile2 = lambda v: jnp.tile(v, LANES // HEAD_DIM)

    xc = ctx
    for layer in range(depth):
        i = layer // 2
        ctx_needed = any(j % 2 == 0 for j in range(layer + 1, depth))
        n1 = norm1_g[layer].reshape(1, d)
        n2 = norm2_g[layer].reshape(1, d)
        mv = lambda j, is_ctx: mod_vec(layer, j, is_ctx)

        if layer % 2 == 0:
            w_in = attn_w_in[i].astype(BF16)
            w_out = attn_w_out[i].astype(BF16)
            qscale = ATTN_SCALE * LOG2E
            gains = jnp.stack([tile2(qnorm_a[i]) * qscale, tile2(knorm_a[i]),
                               tile2(qnorm_b[i]) * qscale, tile2(knorm_b[i])]).astype(F32)
            sink = sink_b[i].astype(F32)
            def logit_bound(gq, gk):
                raw = HEAD_DIM * jnp.max(jnp.abs(gq)) * jnp.max(jnp.abs(gk)) * qscale * 1.02
                return raw.astype(BF16).astype(F32)
            bound_a = logit_bound(qnorm_a[i], knorm_a[i])
            bound_b = jnp.maximum(logit_bound(qnorm_b[i], knorm_b[i]),
                                  (jnp.max(sink) * LOG2E * 1.02).astype(BF16).astype(F32))
            fast_a = bound_a <= FAST_MAX_LOGIT
            fast_b = jnp.logical_and(bound_b <= FAST_MAX_LOGIT,
                                     jnp.max(jnp.abs(sink)) * LOG2E <= FAST_MAX_LOGIT)
            bound_a = bound_a.reshape(1)
            bound_b = bound_b.reshape(1)
            qa_l, ka_l, va_l, qb_l, kb_l, vb_l = _attn_in(
                x, n1, mv(1, False), mv(0, False), w_in, gains, cos_l, sin_l, bd, rope=True)
            qa_c, ka_c, va_c, qb_c, kb_c, vb_c = _attn_in(
                xc, n1, mv(1, True), mv(0, True), w_in, gains, cos_c, sin_c, bd, rope=False)
            ka_all = jnp.concatenate([ka_c, ka_l], axis=1)
            va_all = jnp.concatenate([va_c, va_l], axis=1)
            kb_all = jnp.concatenate([kb_c, kb_l], axis=1)
            vb_all = jnp.concatenate([vb_c, vb_l], axis=1)
            oa = _attend(fast_a, qa_l, ka_all, va_all, no_sink, bound_a, mode="global", use_sink=False)
            ob = _attend(fast_b, qb_l, kb_all, vb_all, sink, bound_b, mode="window", use_sink=True)
            x, h2 = _attn_out(oa, ob, w_out, x, mv(2, False), n2, mv(4, False), mv(3, False))
            if ctx_needed:
                oa_c = _attend(fast_a, qa_c, ka_c, va_c, no_sink, bound_a, mode="ctx", use_sink=False)
                ob_c = _attend(fast_b, qb_c, kb_c, vb_c, sink, bound_b, mode="ctx", use_sink=True)
                xc, h2c = _attn_out(oa_c, ob_c, w_out, xc, mv(2, True), n2, mv(4, True), mv(3, True))
            wg = ffn_w_gate.astype(BF16)
            wu = ffn_w_up.astype(BF16)
            wd = ffn_w_down.astype(BF16)
            x = _ffn(h2, wg, wu, wd, None, x, mv(5, False), e0=i)
            if ctx_needed:
                xc = _ffn(h2c, wg, wu, wd, None, xc, mv(5, True), e0=i)
        else:
            w_in = conv_w_in[i].astype(BF16)
            w_out = conv_w_out[i].astype(BF16)
            cw = conv_w[i].astype(F32)
            router_f32 = jnp.zeros((d, LANES), F32).at[:, :n_experts].set(router_w[i])
            router_hi = router_f32.astype(BF16)
            router_pad = jnp.stack([router_hi, (router_f32 - router_hi.astype(F32)).astype(BF16)])
            wg = moe_w_gate.reshape((-1,) + moe_w_gate.shape[2:])
            wu = moe_w_up.reshape((-1,) + moe_w_up.shape[2:])
            wd = moe_w_down.reshape((-1,) + moe_w_down.shape[2:])
            e0 = i * n_experts
            bg, z = _conv_in(x, n1, mv(1, False), mv(0, False), w_in)
            x, h2, route = _conv_out(z, bg, cw, w_out, x, mv(2, False), n2, mv(4, False), mv(3, False),
                                     router_pad, n_experts, F32)
            x = _moe_routed(h2, route, x, mv(5, False), wg, wu, wd, e0, n_experts)
            if ctx_needed:
                bg, z = _conv_in(xc, n1, mv(1, True), mv(0, True), w_in)
                xc, h2c, gates_c = _conv_out(z, bg, cw, w_out, xc, mv(2, True), n2, mv(4, True),
                                             mv(3, True), router_pad, n_experts, BF16)
                flat = lambda a: a.reshape((1, bsz * n_ctx) + a.shape[2:])
                xc = _ffn(flat(h2c), wg, wu, wd, flat(gates_c), flat(xc), mv(5, True),
                          e0=e0, n_e=n_experts).reshape(bsz, n_ctx, d)
    return x
```
